```python
import math
import jax, jax.numpy as jnp
from jax import lax
import numpy as np

D_MODEL = 1024
BATCH = 4
SEQ = 8192
DEPTH = 4

N_HEADS = 4
BRANCH_W = D_MODEL // 2
N_BRANCH = 3
GLA_DK = D_MODEL // 16
GLA_DV = BRANCH_W // N_HEADS
GLA_RANK = 16
GLA_TAU = 16.0
GLA_CHUNK = 64
RET_DK = D_MODEL // 16
RET_DV = BRANCH_W // N_HEADS
RET_CHUNK = 128
RET_ROPE_BASE = 10000.0
DIL_HD = BRANCH_W // N_HEADS
DIL_GROUPS = ((128, 1), (512, 4), (2048, 16))
N_DIL = len(DIL_GROUPS)
ROPE_THETA = 500000.0
ROPE_DIMS = DIL_HD // 4
D_FF = -(-8 * D_MODEL // (3 * 256)) * 256
ALPHA = (2 * DEPTH) ** 0.25
BETA = (8 * DEPTH) ** -0.25

IN_WIDTHS = (N_HEADS * GLA_DK, N_HEADS * GLA_DK, BRANCH_W, BRANCH_W, GLA_RANK,
             N_HEADS * RET_DK, N_HEADS * RET_DK, BRANCH_W, BRANCH_W,
             N_DIL * BRANCH_W, N_DIL * BRANCH_W, N_DIL * BRANCH_W,
             N_BRANCH * D_MODEL)
D_IN = sum(IN_WIDTHS)
IN_SPLITS = tuple(int(s) for s in np.cumsum(IN_WIDTHS)[:-1])

kernel_name = 'hybrid_gated_gla_retention_dilated_attn'

F32 = jnp.float32


def _layer_norm(x, g, b, eps=1e-5):
    xf = x.astype(F32)
    mu = jnp.mean(xf, axis=-1, keepdims=True)
    var = jnp.mean(jnp.square(xf - mu), axis=-1, keepdims=True)
    return ((xf - mu) * lax.rsqrt(var + eps) * g.astype(F32) + b.astype(F32)).astype(x.dtype)


def _head_rms_norm(o, g, eps=1e-6):
    B, S, H, D = o.shape
    y = o * lax.rsqrt(jnp.mean(jnp.square(o), axis=-1, keepdims=True) + eps)
    return (y * g.astype(F32).reshape(H, D)).reshape(B, S, H * D)


def _head_group_norm(o, g, eps=1e-5):
    B, S, H, D = o.shape
    mu = jnp.mean(o, axis=-1, keepdims=True)
    var = jnp.mean(jnp.square(o - mu), axis=-1, keepdims=True)
    y = (o - mu) * lax.rsqrt(var + eps)
    return (y * g.astype(F32).reshape(H, D)).reshape(B, S, H * D)


def _rotary(x, pos, n_rot, base):
    half = n_rot // 2
    inv = base ** (-jnp.arange(half, dtype=F32) * 2.0 / n_rot)
    ang = pos[:, None] * inv[None, :]
    cos = jnp.cos(ang)[None, :, None, :]
    sin = jnp.sin(ang)[None, :, None, :]
    xf = x.astype(F32)
    x1 = xf[..., :half]
    x2 = xf[..., half:n_rot]
    return jnp.concatenate([x1 * cos - x2 * sin, x2 * cos + x1 * sin, xf[..., n_rot:]], axis=-1)


def _gla_chunked(q, k, v, log_a):
    B, S, H, DK = q.shape
    DV = v.shape[-1]
    C = GLA_CHUNK
    N = S // C
    q = q.astype(F32).reshape(B, N, C, H, DK) * (DK ** -0.5)
    k = k.astype(F32).reshape(B, N, C, H, DK)
    v = v.astype(F32).reshape(B, N, C, H, DV)
    b = jnp.cumsum(log_a.astype(F32).reshape(B, N, C, H, DK), axis=2)
    b_last = b[:, :, -1:]
    q_s = q * jnp.exp(b)
    k_s = k * jnp.exp(-b)
    causal = jnp.tril(jnp.ones((C, C), dtype=bool))
    att = jnp.where(causal, jnp.einsum('bnihd,bnjhd->bnhij', q_s, k_s), 0.0)
    o_intra = jnp.einsum('bnhij,bnjhe->bnihe', att, v)
    kv = jnp.einsum('bnjhd,bnjhe->bnhde', k * jnp.exp(b_last - b), v)
    decay = jnp.exp(b_last[:, :, 0])

    def step(state, inp):
        dec, kv_n = inp
        return dec[..., None] * state + kv_n, state

    s0 = jnp.zeros((B, H, DK, DV), F32)
    _, s_prev = lax.scan(step, s0, (jnp.moveaxis(decay, 1, 0), jnp.moveaxis(kv, 1, 0)))
    s_prev = jnp.moveaxis(s_prev, 0, 1)
    o_inter = jnp.einsum('bnihd,bnhde->bnihe', q_s, s_prev)
    return (o_intra + o_inter).reshape(B, S, H, DV)


def _retention_chunked(q, k, v):
    B, S, H, DK = q.shape
    DV = v.shape[-1]
    C = RET_CHUNK
    N = S // C
    log_g = jnp.log1p(-jnp.exp2(-5.0 - jnp.arange(H, dtype=F32)))
    q = q.astype(F32).reshape(B, N, C, H, DK)
    k = k.astype(F32).reshape(B, N, C, H, DK) * (DK ** -0.5)
    v = v.astype(F32).reshape(B, N, C, H, DV)
    idx = jnp.arange(C, dtype=F32)
    diff = idx[:, None] - idx[None, :]
    dmat = jnp.where(diff >= 0, jnp.exp(jnp.maximum(diff, 0.0)[None] * log_g[:, None, None]), 0.0)
    att = jnp.einsum('bnihd,bnjhd->bnhij', q, k) * dmat
    o_intra = jnp.einsum('bnhij,bnjhe->bnihe', att, v)
    q_dec = q * jnp.exp((idx + 1.0)[:, None] * log_g[None, :])[:, :, None]
    k_dec = k * jnp.exp((C - 1.0 - idx)[:, None] * log_g[None, :])[:, :, None]
    kv = jnp.einsum('bnjhd,bnjhe->bnhde', k_dec, v)
    chunk_decay = jnp.exp(C * log_g)[:, None, None]

    def step(state, kv_n):
        return chunk_decay * state + kv_n, state

    s0 = jnp.zeros((B, H, DK, DV), F32)
    _, s_prev = lax.scan(step, s0, jnp.moveaxis(kv, 1, 0))
    s_prev = jnp.moveaxis(s_prev, 0, 1)
    o_inter = jnp.einsum('bnihd,bnhde->bnihe', q_dec, s_prev)
    return (o_intra + o_inter).reshape(B, S, H, DV)


def _dilated_window_attention(q, k, v, window, dilation):
    B, S, H, D = q.shape
    span = window // dilation
    n = S // dilation
    nb = -(-n // span)
    n_pad = nb * span

    def strided(t):
        t = t.astype(F32).reshape(B, n, dilation, H, D)
        return jnp.pad(t, ((0, 0), (0, n_pad - n), (0, 0), (0, 0), (0, 0)))

    def band(t):
        tp = jnp.pad(t, ((0, 0), (span, 0), (0, 0), (0, 0), (0, 0))).reshape(B, nb + 1, span, dilation, H, D)
        return jnp.concatenate([tp[:, :-1], tp[:, 1:]], axis=2)

    qb = strided(q).reshape(B, nb, span, dilation, H, D)
    kb = band(strided(k))
    vb = band(strided(v))
    s = jnp.einsum('bnichd,bnjchd->bnchij', qb, kb) * (D ** -0.5)
    i = jnp.arange(span)[:, None]
    j = jnp.arange(2 * span)[None, :]
    dist = i + span - j
    key_idx = jnp.arange(nb)[:, None, None] * span + j[None] - span
    mask = (dist >= 0)[None] & (dist <= span)[None] & (key_idx >= 0)
    s = jnp.where(mask[None, :, None, None], s, -jnp.inf)
    m = jnp.max(s, axis=-1)
    p = jnp.exp(s - m[..., None])
    den = jnp.sum(p, axis=-1)
    o = jnp.einsum('bnchij,bnjchd->bnichd', p, vb) / jnp.moveaxis(den, -1, 2)[..., None]
    lse = jnp.moveaxis(m + jnp.log(den), -1, 2)
    o = o.reshape(B, n_pad, dilation, H, D)[:, :n].reshape(B, S, H, D)
    lse = lse.reshape(B, n_pad, dilation, H)[:, :n].reshape(B, S, H)
    return o, lse


def _mixer(x, w_in, w_gla_a2, b_gla_a, gla_norm_g, ret_norm_g, w_branch, b_gate, w_out):
    B, S, _ = x.shape
    pos = jnp.arange(S, dtype=F32)
    proj = x @ w_in
    (gq, gk, gv, gr, ga, rq, rk, rv, rg, dq, dk, dv, gate) = jnp.split(proj, IN_SPLITS, axis=-1)

    log_a = jax.nn.log_sigmoid((ga @ w_gla_a2 + b_gla_a).astype(F32)) / GLA_TAU
    o_a = _gla_chunked(gq.reshape(B, S, N_HEADS, GLA_DK), gk.reshape(B, S, N_HEADS, GLA_DK),
                       gv.reshape(B, S, N_HEADS, GLA_DV), log_a.reshape(B, S, N_HEADS, GLA_DK))
    o_a = _head_rms_norm(o_a, gla_norm_g) * jax.nn.silu(gr.astype(F32))

    q_r = _rotary(rq.reshape(B, S, N_HEADS, RET_DK), pos, RET_DK, RET_ROPE_BASE)
    k_r = _rotary(rk.reshape(B, S, N_HEADS, RET_DK), pos, RET_DK, RET_ROPE_BASE)
    o_b = _retention_chunked(q_r, k_r, rv.reshape(B, S, N_HEADS, RET_DV))
    o_b = _head_group_norm(o_b, ret_norm_g) * jax.nn.silu(rg.astype(F32))

    dq = dq.reshape(B, S, N_DIL, N_HEADS, DIL_HD)
    dk = dk.reshape(B, S, N_DIL, N_HEADS, DIL_HD)
    dv = dv.reshape(B, S, N_DIL, N_HEADS, DIL_HD)
    outs = []
    lses = []
    for g, (window, dilation) in enumerate(DIL_GROUPS):
        qg = _rotary(dq[:, :, g], pos, ROPE_DIMS, ROPE_THETA)
        kg = _rotary(dk[:, :, g], pos, ROPE_DIMS, ROPE_THETA)
        o_g, lse_g = _dilated_window_attention(qg, kg, dv[:, :, g], window, dilation)
        outs.append(o_g)
        lses.append(lse_g)
    wts = jax.nn.softmax(jnp.stack(lses, axis=0), axis=0)
    o_c = jnp.sum(wts[..., None] * jnp.stack(outs, axis=0), axis=0).reshape(B, S, BRANCH_W)

    y_a = o_a.astype(x.dtype) @ w_branch[0]
    y_b = o_b.astype(x.dtype) @ w_branch[1]
    y_c = o_c.astype(x.dtype) @ w_branch[2]
    gates = jax.nn.sigmoid((gate + b_gate).astype(F32)).reshape(B, S, N_BRANCH, D_MODEL).astype(x.dtype)
    merged = gates[:, :, 0] * y_a + gates[:, :, 1] * y_b + gates[:, :, 2] * y_c
    return merged @ w_out


def _swiglu(x, w_gate, w_up, w_down):
    return (jax.nn.silu(x @ w_gate) * (x @ w_up)) @ w_down


def setup_inputs(seed: int = 0) -> dict:
    key = jax.random.key(seed)
    ks = jax.random.split(key, 16)
    nrm = jax.random.normal
    return {
        'x': nrm(ks[0], (BATCH, SEQ, D_MODEL), F32),
        'w_in': nrm(ks[1], (DEPTH, D_MODEL, D_IN), F32) * D_MODEL ** -0.5,
        'w_gla_a2': nrm(ks[2], (DEPTH, GLA_RANK, N_HEADS * GLA_DK), F32) * GLA_RANK ** -0.5,
        'b_gla_a': 0.1 * nrm(ks[3], (DEPTH, N_HEADS * GLA_DK), F32),
        'gla_norm_g': 1.0 + 0.02 * nrm(ks[4], (DEPTH, BRANCH_W), F32),
        'ret_norm_g': 1.0 + 0.02 * nrm(ks[5], (DEPTH, BRANCH_W), F32),
        'w_branch': nrm(ks[6], (DEPTH, N_BRANCH, BRANCH_W, D_MODEL), F32) * BRANCH_W ** -0.5,
        'b_gate': 0.02 * nrm(ks[7], (DEPTH, N_BRANCH * D_MODEL), F32),
        'w_out': nrm(ks[8], (DEPTH, D_MODEL, D_MODEL), F32) * (D_MODEL ** -0.5 * BETA),
        'ln1_g': 1.0 + 0.02 * nrm(ks[9], (DEPTH, D_MODEL), F32),
        'ln1_b': 0.02 * nrm(ks[10], (DEPTH, D_MODEL), F32),
        'w_ffn_gate': nrm(ks[11], (DEPTH, D_MODEL, D_FF), F32) * D_MODEL ** -0.5,
        'w_ffn_up': nrm(ks[12], (DEPTH, D_MODEL, D_FF), F32) * D_MODEL ** -0.5,
        'w_ffn_down': nrm(ks[13], (DEPTH, D_FF, D_MODEL), F32) * (D_FF ** -0.5 * BETA),
        'ln2_g': 1.0 + 0.02 * nrm(ks[14], (DEPTH, D_MODEL), F32),
        'ln2_b': 0.02 * nrm(ks[15], (DEPTH, D_MODEL), F32),
    }


def reference(x, w_in, w_gla_a2, b_gla_a, gla_norm_g, ret_norm_g, w_branch, b_gate, w_out,
              ln1_g, ln1_b, w_ffn_gate, w_ffn_up, w_ffn_down, ln2_g, ln2_b):
    for l in range(DEPTH):
        h = _mixer(x, w_in[l], w_gla_a2[l], b_gla_a[l], gla_norm_g[l], ret_norm_g[l],
                   w_branch[l], b_gate[l], w_out[l])
        x = _layer_norm(ALPHA * x + h, ln1_g[l], ln1_b[l])
        f = _swiglu(x, w_ffn_gate[l], w_ffn_up[l], w_ffn_down[l])
        x = _layer_norm(ALPHA * x + f, ln2_g[l], ln2_b[l])
    return x
```

```python
import functools
import math

import jax
import jax.numpy as jnp
from jax import lax
from jax.experimental import pallas as pl
from jax.experimental.pallas import tpu as pltpu

F32 = jnp.float32
BF16 = jnp.bfloat16

N_HEADS = 4
GLA_DK = 64
GLA_DV = 128
GLA_RANK = 16
GLA_TAU = 16.0
GLA_CHUNK = 64
RET_DK = 64
RET_DV = 128
RET_CHUNK = 128
RET_ROPE_BASE = 10000.0
DIL_HD = 128
DIL_GROUPS = ((128, 1), (512, 4), (2048, 16))
DIL_SPAN = 128
ROPE_THETA = 500000.0
ROPE_DIMS = 32
BRANCH_W = 512
GA_PAD = 256
MASK_VALUE = -1e30

VMEM_LIMIT = 56 * 1024 * 1024
SEQ_BLOCK = 512
ROW_TILE = 512
FF_CHUNK = 256


def _sigmoid(x):
    return 1.0 / (1.0 + jnp.exp(-x))


def _nt_dot(a, b):
    return lax.dot_general(a, b, (((1,), (1,)), ((), ())), preferred_element_type=F32)


def _tn_dot(a, b):
    return lax.dot_general(a, b, (((0,), (0,)), ((), ())), preferred_element_type=F32)


def _resident(shape):
    nd = len(shape)
    return pl.BlockSpec(shape, lambda *_: (0,) * nd, pipeline_mode=pl.Buffered(1))


def _proj_kernel(x_ref, w1_ref, w2_ref, o1_ref, o2_ref, *, tn):
    x = x_ref[...]
    for w_ref, o_ref in ((w1_ref, o1_ref), (w2_ref, o2_ref)):
        n = o_ref.shape[1]
        for j0 in range(0, n, tn):
            j1 = min(j0 + tn, n)
            o_ref[:, j0:j1] = jnp.dot(x, w_ref[:, j0:j1], preferred_element_type=F32).astype(o_ref.dtype)


def _proj(xb, w1, w2, tm):
    t, k = xb.shape
    n1, n2 = w1.shape[1], w2.shape[1]
    return pl.pallas_call(
        functools.partial(_proj_kernel, tn=512),
        grid=(t // tm,),
        in_specs=[pl.BlockSpec((tm, k), lambda i: (i, 0)), _resident((k, n1)), _resident((k, n2))],
        out_specs=[pl.BlockSpec((tm, n1), lambda i: (i, 0)), pl.BlockSpec((tm, n2), lambda i: (i, 0))],
        out_shape=[jax.ShapeDtypeStruct((t, n1), BF16), jax.ShapeDtypeStruct((t, n2), BF16)],
        compiler_params=pltpu.CompilerParams(dimension_semantics=("parallel",), vmem_limit_bytes=VMEM_LIMIT),
        name="in_proj",
    )(xb, w1, w2)


def _gla_kernel(q_ref, k_ref, v_ref, r_ref, a_ref, wa_ref, ba_ref, g_ref, o_ref, st_ref):
    c = GLA_CHUNK

    @pl.when(pl.program_id(1) == 0)
    def _():
        st_ref[...] = jnp.zeros_like(st_ref)

    row = lax.broadcasted_iota(jnp.int32, (c, c), 0)
    col = lax.broadcasted_iota(jnp.int32, (c, c), 1)
    causal = row >= col
    tri = causal.astype(F32)

    def chunk(ci, carry):
        r0 = pl.multiple_of(ci * c, c)
        rows = pl.ds(r0, c)
        q = q_ref[rows, :].astype(F32) * (GLA_DK ** -0.5)
        k = k_ref[rows, :].astype(F32)
        v = v_ref[rows, :]
        z = jnp.dot(a_ref[rows, :], wa_ref[...], preferred_element_type=F32) + ba_ref[...]
        log_a = (jnp.minimum(z, 0.0) - jnp.log1p(jnp.exp(-jnp.abs(z)))) * (1.0 / GLA_TAU)
        b = jnp.dot(tri, log_a, precision=lax.Precision.HIGHEST, preferred_element_type=F32)
        b_last = b[c - 1:c, :]
        q_s = (q * jnp.exp(b)).astype(BF16)
        k_s = (k * jnp.exp(-b)).astype(BF16)
        k_d = (k * jnp.exp(b_last - b)).astype(BF16)
        decay = jnp.exp(b_last)
        outs = []
        for h in range(N_HEADS):
            ks = slice(h * GLA_DK, (h + 1) * GLA_DK)
            vh = v[:, h * GLA_DV:(h + 1) * GLA_DV]
            att = jnp.where(causal, _nt_dot(q_s[:, ks], k_s[:, ks]), 0.0)
            st = st_ref[h]
            o = jnp.dot(att.astype(BF16), vh, preferred_element_type=F32) + _nt_dot(q_s[:, ks], st.astype(BF16))
            st_ref[h] = st * decay[:, ks] + _tn_dot(vh, k_d[:, ks])
            outs.append(o * lax.rsqrt(jnp.mean(o * o, axis=-1, keepdims=True) + 1e-6))
        y = jnp.concatenate(outs, axis=-1) * g_ref[...]
        r = r_ref[rows, :].astype(F32)
        o_ref[rows, :] = (y * (r * _sigmoid(r))).astype(o_ref.dtype)
        return carry

    lax.fori_loop(0, q_ref.shape[0] // c, chunk, 0)


def _gla(p1, w_a2, b_a, norm_g, batch, seq):
    t = batch * seq
    tb = SEQ_BLOCK
    nblk = seq // tb

    def at(width, col):
        return pl.BlockSpec((tb, width), lambda b, i: (b * nblk + i, col))

    return pl.pallas_call(
        _gla_kernel,
        grid=(batch, nblk),
        in_specs=[at(256, 0), at(256, 1), at(512, 1), at(512, 2), at(GA_PAD, 3072 // GA_PAD),
                  _resident(w_a2.shape), _resident(b_a.shape), _resident(norm_g.shape)],
        out_specs=pl.BlockSpec((tb, BRANCH_W), lambda b, i: (b * nblk + i, 0)),
        out_shape=jax.ShapeDtypeStruct((t, BRANCH_W), BF16),
        scratch_shapes=[pltpu.VMEM((N_HEADS, GLA_DV, GLA_DK), F32)],
        compiler_params=pltpu.CompilerParams(dimension_semantics=("parallel", "arbitrary"),
                                             vmem_limit_bytes=VMEM_LIMIT),
        name="gla",
    )(p1, p1, p1, p1, p1, w_a2, b_a, norm_g)


def _ret_kernel(q_ref, k_ref, v_ref, r_ref, cos_ref, sin_ref, g_ref, o_ref, st_ref):
    c = RET_CHUNK
    width = N_HEADS * RET_DK

    @pl.when(pl.program_id(1) == 0)
    def _():
        st_ref[...] = jnp.zeros_like(st_ref)

    log_g = [math.log1p(-(2.0 ** (-5.0 - h))) for h in range(N_HEADS)]
    diff = (lax.broadcasted_iota(jnp.int32, (c, c), 0) - lax.broadcasted_iota(jnp.int32, (c, c), 1)).astype(F32)
    ridx = lax.broadcasted_iota(jnp.int32, (c, 1), 0).astype(F32)
    first_half = (lax.broadcasted_iota(jnp.int32, (c, width), 1) % RET_DK) < (RET_DK // 2)

    def chunk(ci, carry):
        r0 = pl.multiple_of(ci * c, c)
        rows = pl.ds(r0, c)
        cos = cos_ref[rows, :]
        sin = sin_ref[rows, :]

        def rot(x):
            partner = jnp.where(first_half, pltpu.roll(x, width - RET_DK // 2, 1), pltpu.roll(x, RET_DK // 2, 1))
            return x * cos + partner * sin

        q = rot(q_ref[rows, :].astype(F32))
        k = rot(k_ref[rows, :].astype(F32)) * (RET_DK ** -0.5)
        v = v_ref[rows, :]
        outs = []
        for h in range(N_HEADS):
            lg = log_g[h]
            ks = slice(h * RET_DK, (h + 1) * RET_DK)
            vh = v[:, h * RET_DV:(h + 1) * RET_DV]
            qh = q[:, ks]
            kh = k[:, ks]
            dmat = jnp.where(diff >= 0.0, jnp.exp(jnp.maximum(diff, 0.0) * lg), 0.0)
            att = _nt_dot(qh.astype(BF16), kh.astype(BF16)) * dmat
            q_dec = (qh * jnp.exp((ridx + 1.0) * lg)).astype(BF16)
            k_dec = (kh * jnp.exp((c - 1.0 - ridx) * lg)).astype(BF16)
            st = st_ref[h]
            o = jnp.dot(att.astype(BF16), vh, preferred_element_type=F32) + _nt_dot(q_dec, st.astype(BF16))
            st_ref[h] = st * math.exp(c * lg) + _tn_dot(vh, k_dec)
            mu = jnp.mean(o, axis=-1, keepdims=True)
            d = o - mu
            outs.append(d * lax.rsqrt(jnp.mean(d * d, axis=-1, keepdims=True) + 1e-5))
        y = jnp.concatenate(outs, axis=-1) * g_ref[...]
        r = r_ref[rows, :].astype(F32)
        o_ref[rows, :] = (y * (r * _sigmoid(r))).astype(o_ref.dtype)
        return carry

    lax.fori_loop(0, q_ref.shape[0] // c, chunk, 0)


def _ret(p1, cos, sin, norm_g, batch, seq):
    t = batch * seq
    tb = SEQ_BLOCK
    nblk = seq // tb

    def at(width, col):
        return pl.BlockSpec((tb, width), lambda b, i: (b * nblk + i, col))

    tab = pl.BlockSpec((tb, N_HEADS * RET_DK), lambda b, i: (i, 0))
    return pl.pallas_call(
        _ret_kernel,
        grid=(batch, nblk),
        in_specs=[at(256, 6), at(256, 7), at(512, 4), at(512, 5), tab, tab, _resident(norm_g.shape)],
        out_specs=pl.BlockSpec((tb, BRANCH_W), lambda b, i: (b * nblk + i, 0)),
        out_shape=jax.ShapeDtypeStruct((t, BRANCH_W), BF16),
        scratch_shapes=[pltpu.VMEM((N_HEADS, RET_DV, RET_DK), F32)],
        compiler_params=pltpu.CompilerParams(dimension_semantics=("parallel", "arbitrary"),
                                             vmem_limit_bytes=VMEM_LIMIT),
        name="retention",
    )(p1, p1, p1, p1, cos, sin, norm_g)


def _dil_kernel(q_ref, kc_ref, kp_ref, vc_ref, vp_ref, cc_ref, sc_ref, cp_ref, sp_ref, o_ref, l_ref):
    n = DIL_SPAN
    width = N_HEADS * DIL_HD
    half = ROPE_DIMS // 2
    low = (lax.broadcasted_iota(jnp.int32, (n, width), 1) % DIL_HD) < half

    def rot(x_ref, cos_ref, sin_ref):
        x = x_ref[...].astype(F32)
        cos = jnp.concatenate([cos_ref[...]] * N_HEADS, axis=1)
        sin = jnp.concatenate([sin_ref[...]] * N_HEADS, axis=1)
        partner = jnp.where(low, pltpu.roll(x, width - half, 1), pltpu.roll(x, half, 1))
        return (x * cos + partner * sin).astype(BF16)

    q = rot(q_ref, cc_ref, sc_ref)
    kc = rot(kc_ref, cc_ref, sc_ref)
    kp = rot(kp_ref, cp_ref, sp_ref)
    row = lax.broadcasted_iota(jnp.int32, (n, 2 * n), 0)
    col = lax.broadcasted_iota(jnp.int32, (n, 2 * n), 1)
    dist = row + n - col
    has_prev = pl.program_id(2) > 0
    mask = (dist >= 0) & (dist <= n) & ((col >= n) | has_prev)
    lane_head = lax.broadcasted_iota(jnp.int32, (n, 128), 1) // (128 // N_HEADS)
    lse_tile = jnp.zeros((n, 128), F32)
    for h in range(N_HEADS):
        hs = slice(h * DIL_HD, (h + 1) * DIL_HD)
        kcat = jnp.concatenate([kp[:, hs], kc[:, hs]], axis=0)
        vcat = jnp.concatenate([vp_ref[:, hs], vc_ref[:, hs]], axis=0)
        s = jnp.where(mask, _nt_dot(q[:, hs], kcat) * (DIL_HD ** -0.5), MASK_VALUE)
        m = jnp.max(s, axis=-1, keepdims=True)
        p = jnp.exp(s - m)
        den = jnp.sum(p, axis=-1, keepdims=True)
        o = jnp.dot(p.astype(BF16), vcat, preferred_element_type=F32) / den
        o_ref[:, hs] = o.astype(o_ref.dtype)
        lse_tile = jnp.where(lane_head == h, m + jnp.log(den), lse_tile)
    l_ref[...] = lse_tile


def _dil(p2, cos, sin, group, dilation, batch, seq):
    n = seq // dilation
    nb = n // DIL_SPAN
    ncol = p2.shape[1]
    p2v = p2.reshape(batch, n, dilation * ncol)
    cosv = cos.reshape(n, dilation * 128)
    sinv = sin.reshape(n, dilation * 128)
    per_c = ncol // BRANCH_W
    base = group * 3

    def cur(off):
        return pl.BlockSpec((None, DIL_SPAN, BRANCH_W), lambda b, c, i: (b, i, c * per_c + base + off))

    def prev(off):
        return pl.BlockSpec((None, DIL_SPAN, BRANCH_W),
                            lambda b, c, i: (b, jnp.maximum(i - 1, 0), c * per_c + base + off))

    tab_c = pl.BlockSpec((DIL_SPAN, 128), lambda b, c, i: (i, c))
    tab_p = pl.BlockSpec((DIL_SPAN, 128), lambda b, c, i: (jnp.maximum(i - 1, 0), c))
    o, lse = pl.pallas_call(
        _dil_kernel,
        grid=(batch, dilation, nb),
        in_specs=[cur(0), cur(1), prev(1), cur(2), prev(2), tab_c, tab_c, tab_p, tab_p],
        out_specs=[pl.BlockSpec((None, DIL_SPAN, BRANCH_W), lambda b, c, i: (b, i, c)),
                   pl.BlockSpec((None, DIL_SPAN, 128), lambda b, c, i: (b, i, c))],
        out_shape=[jax.ShapeDtypeStruct((batch, n, dilation * BRANCH_W), BF16),
                   jax.ShapeDtypeStruct((batch, n, dilation * 128), F32)],
        compiler_params=pltpu.CompilerParams(dimension_semantics=("parallel", "parallel", "arbitrary"),
                                             vmem_limit_bytes=VMEM_LIMIT),
        name=f"dilated_attn_{dilation}",
    )(p2v, p2v, p2v, p2v, p2v, cosv, sinv, cosv, sinv)
    return o.reshape(batch * seq, BRANCH_W), lse.reshape(batch * seq, 128)


def _layer_norm(v, g, b):
    mu = jnp.mean(v, axis=-1, keepdims=True)
    d = v - mu
    return d * lax.rsqrt(jnp.mean(d * d, axis=-1, keepdims=True) + 1e-5) * g + b


def _merge_kernel(x_ref, xb_ref, oa_ref, ob_ref, o1_ref, o2_ref, o3_ref, l1_ref, l2_ref, l3_ref,
                  wg_ref, bg_ref, wb_ref, wo_ref, lg_ref, lb_ref, y_ref, yb_ref, *, alpha):
    d = x_ref.shape[1]
    l1, l2, l3 = l1_ref[...], l2_ref[...], l3_ref[...]
    m = jnp.maximum(jnp.maximum(l1, l2), l3)
    e1, e2, e3 = jnp.exp(l1 - m), jnp.exp(l2 - m), jnp.exp(l3 - m)
    tot = e1 + e2 + e3
    w1, w2, w3 = e1 / tot, e2 / tot, e3 / tot
    lanes_per_head = 128 // N_HEADS
    parts = []
    for h in range(N_HEADS):
        hs = slice(h * DIL_HD, (h + 1) * DIL_HD)
        ls = slice(h * lanes_per_head, h * lanes_per_head + 1)
        parts.append(w1[:, ls] * o1_ref[:, hs].astype(F32) + w2[:, ls] * o2_ref[:, hs].astype(F32)
                     + w3[:, ls] * o3_ref[:, hs].astype(F32))
    oc = jnp.concatenate(parts, axis=-1).astype(BF16)
    xb = xb_ref[...]
    merged = None
    for j, o in enumerate((oa_ref[...], ob_ref[...], oc)):
        gate = _sigmoid(jnp.dot(xb, wg_ref[:, j * d:(j + 1) * d], preferred_element_type=F32)
                        + bg_ref[:, j * d:(j + 1) * d])
        term = gate * jnp.dot(o, wb_ref[j], preferred_element_type=F32)
        merged = term if merged is None else merged + term
    hmix = jnp.dot(merged.astype(BF16), wo_ref[...], preferred_element_type=F32)
    y = _layer_norm(alpha * x_ref[...] + hmix, lg_ref[...], lb_ref[...])
    y_ref[...] = y
    yb_ref[...] = y.astype(BF16)


def _merge(x, xb, oa, ob, o_dil, l_dil, w_gate, b_gate, w_branch, w_out, ln_g, ln_b, alpha, tm):
    t, d = x.shape

    def rows(width):
        return pl.BlockSpec((tm, width), lambda i: (i, 0))

    return pl.pallas_call(
        functools.partial(_merge_kernel, alpha=alpha),
        grid=(t // tm,),
        in_specs=[rows(d), rows(d)] + [rows(BRANCH_W)] * 5 + [rows(128)] * 3
                 + [_resident(w_gate.shape), _resident(b_gate.shape), _resident(w_branch.shape),
                    _resident(w_out.shape), _resident(ln_g.shape), _resident(ln_b.shape)],
        out_specs=[rows(d), rows(d)],
        out_shape=[jax.ShapeDtypeStruct((t, d), F32), jax.ShapeDtypeStruct((t, d), BF16)],
        compiler_params=pltpu.CompilerParams(dimension_semantics=("parallel",), vmem_limit_bytes=VMEM_LIMIT),
        name="merge_out_ln",
    )(x, xb, oa, ob, *o_dil, *l_dil, w_gate, b_gate, w_branch, w_out, ln_g, ln_b)


def _ffn_kernel(x_ref, xb_ref, wg_ref, wu_ref, wd_ref, lg_ref, lb_ref, y_ref, yb_ref, act_ref, *, alpha):
    xb = xb_ref[...]
    d_ff = wg_ref.shape[1]
    for j0 in range(0, d_ff, FF_CHUNK):
        cs = slice(j0, j0 + FF_CHUNK)
        g = jnp.dot(xb, wg_ref[:, cs], preferred_element_type=F32)
        u = jnp.dot(xb, wu_ref[:, cs], preferred_element_type=F32)
        act_ref[:, cs] = (g * _sigmoid(g) * u).astype(BF16)
    f = jnp.dot(act_ref[...], wd_ref[...], preferred_element_type=F32)
    y = _layer_norm(alpha * x_ref[...] + f, lg_ref[...], lb_ref[...])
    y_ref[...] = y
    yb_ref[...] = y.astype(BF16)


def _ffn(x, xb, w_gate, w_up, w_down, ln_g, ln_b, alpha, tm):
    t, d = x.shape
    d_ff = w_gate.shape[1]
    rows = pl.BlockSpec((tm, d), lambda i: (i, 0))
    return pl.pallas_call(
        functools.partial(_ffn_kernel, alpha=alpha),
        grid=(t // tm,),
        in_specs=[rows, rows, _resident(w_gate.shape), _resident(w_up.shape), _resident(w_down.shape),
                  _resident(ln_g.shape), _resident(ln_b.shape)],
        out_specs=[rows, rows],
        out_shape=[jax.ShapeDtypeStruct((t, d), F32), jax.ShapeDtypeStruct((t, d), BF16)],
        scratch_shapes=[pltpu.VMEM((tm, d_ff), BF16)],
        compiler_params=pltpu.CompilerParams(dimension_semantics=("parallel",), vmem_limit_bytes=VMEM_LIMIT),
        name="swiglu_ln",
    )(x, xb, w_gate, w_up, w_down, ln_g, ln_b)


def _rope_tables(seq):
    pos = jnp.arange(seq, dtype=F32)

    def tables(n_rot, base, width):
        half = n_rot // 2
        inv = base ** (-jnp.arange(half, dtype=F32) * 2.0 / n_rot)
        ang = pos[:, None] * inv[None, :]
        cos = jnp.concatenate([jnp.cos(ang), jnp.cos(ang), jnp.ones((seq, width - n_rot), F32)], axis=1)
        sin = jnp.concatenate([-jnp.sin(ang), jnp.sin(ang), jnp.zeros((seq, width - n_rot), F32)], axis=1)
        return cos, sin

    rc, rs = tables(RET_DK, RET_ROPE_BASE, RET_DK)
    dc, ds = tables(ROPE_DIMS, ROPE_THETA, DIL_HD)
    return jnp.tile(rc, (1, N_HEADS)), jnp.tile(rs, (1, N_HEADS)), dc, ds


def _split_w_in(w_in):
    d = w_in.shape[0]
    widths = (256, 256, 512, 512, GLA_RANK, 256, 256, 512, 512, 1536, 1536, 1536, 3 * d)
    offs = [0]
    for w in widths:
        offs.append(offs[-1] + w)
    seg = [w_in[:, offs[i]:offs[i + 1]] for i in range(len(widths))]
    gq, gk, gv, gr, ga, rq, rk, rv, rg, dq, dk, dv, gate = seg
    ga = jnp.pad(ga, ((0, 0), (0, GA_PAD - GLA_RANK)))
    w1 = jnp.concatenate([gq, gk, gv, gr, rq, rk, rv, rg, ga], axis=1)
    dil = []
    for g in range(len(DIL_GROUPS)):
        gs = slice(g * BRANCH_W, (g + 1) * BRANCH_W)
        dil += [dq[:, gs], dk[:, gs], dv[:, gs]]
    w2 = jnp.concatenate(dil, axis=1)
    return w1.astype(BF16), w2.astype(BF16), gate.astype(BF16)


def kernel(x, w_in, w_gla_a2, b_gla_a, gla_norm_g, ret_norm_g, w_branch, b_gate, w_out, ln1_g, ln1_b,
           w_ffn_gate, w_ffn_up, w_ffn_down, ln2_g, ln2_b):
    batch, seq, d = x.shape
    depth = w_in.shape[0]
    alpha = (2 * depth) ** 0.25
    t = batch * seq
    ret_cos, ret_sin, dil_cos, dil_sin = _rope_tables(seq)
    xf = x.reshape(t, d)
    xb = xf.astype(BF16)
    for l in range(depth):
        w1, w2, w_gate = _split_w_in(w_in[l])
        w_a2 = jnp.pad(w_gla_a2[l], ((0, GA_PAD - GLA_RANK), (0, 0))).astype(BF16)
        p1, p2 = _proj(xb, w1, w2, ROW_TILE)
        oa = _gla(p1, w_a2, b_gla_a[l][None, :], gla_norm_g[l][None, :], batch, seq)
        ob = _ret(p1, ret_cos, ret_sin, ret_norm_g[l][None, :], batch, seq)
        o_dil, l_dil = [], []
        for g, (_, dilation) in enumerate(DIL_GROUPS):
            o, lse = _dil(p2, dil_cos, dil_sin, g, dilation, batch, seq)
            o_dil.append(o)
            l_dil.append(lse)
        xf, xb = _merge(xf, xb, oa, ob, o_dil, l_dil, w_gate, b_gate[l][None, :], w_branch[l].astype(BF16),
                        w_out[l].astype(BF16), ln1_g[l][None, :], ln1_b[l][None, :], alpha, ROW_TILE)
        xf, xb = _ffn(xf, xb, w_ffn_gate[l].astype(BF16), w_ffn_up[l].astype(BF16), w_ffn_down[l].astype(BF16),
                      ln2_g[l][None, :], ln2_b[l][None, :], alpha, ROW_TILE)
    return xf.reshape(batch, seq, d)
```

```python
import functools
import math

import jax
import jax.numpy as jnp
from jax import lax
from jax.experimental import pallas as pl
from jax.experimental.pallas import tpu as pltpu

F32 = jnp.float32
BF16 = jnp.bfloat16

LANES = 128
N_HEADS = 4
GLA_DK = 64
GLA_DV = 128
GLA_RANK = 16
GLA_TAU = 16.0
GLA_CHUNK = 64
RET_DK = 64
RET_DV = 128
RET_CHUNK = 128
RET_ROPE_BASE = 10000.0
DIL_HD = 128
DIL_GROUPS = ((128, 1), (512, 4), (2048, 16))
DIL_SPAN = 128
DIL_BLOCK = DIL_SPAN * max(d for _, d in DIL_GROUPS)
ROPE_THETA = 500000.0
ROPE_DIMS = 32
BRANCH_W = 512
GA_PAD = 256
RET_QK_COL = 1536
MASK_VALUE = -1e30

VMEM_LIMIT = 56 * 1024 * 1024
SEQ_BLOCK = 512
ROW_TILE = 512
FF_CHUNK = 256


def _sigmoid(x):
    return 1.0 / (1.0 + jnp.exp(-x))


def _nt_dot(a, b):
    return lax.dot_general(a, b, (((1,), (1,)), ((), ())), preferred_element_type=F32)


def _tn_dot(a, b):
    return lax.dot_general(a, b, (((0,), (0,)), ((), ())), preferred_element_type=F32)


def _resident(shape):
    nd = len(shape)
    return pl.BlockSpec(shape, lambda *_: (0,) * nd, pipeline_mode=pl.Buffered(1))


def _rotary(x, cos, sin, head_dim, half):
    width = x.shape[1]
    low = (lax.broadcasted_iota(jnp.int32, x.shape, 1) % head_dim) < half
    partner = jnp.where(low, pltpu.roll(x, width - half, 1), pltpu.roll(x, half, 1))
    return x * cos + partner * sin


def _proj_kernel(*refs, n_slabs):
    n_groups = len(DIL_GROUPS)
    xb_ref = refs[0]
    slabs = refs[1:1 + n_slabs]
    w1_ref, w2_ref, rc_ref, rs_ref = refs[1 + n_slabs:5 + n_slabs]
    tabs = refs[5 + n_slabs:5 + n_slabs + 2 * n_groups]
    p1_ref = refs[5 + n_slabs + 2 * n_groups]
    g_refs = refs[6 + n_slabs + 2 * n_groups:6 + n_slabs + 3 * n_groups]
    xp_ref = refs[-1]
    tm = xb_ref.shape[0]
    tn = BRANCH_W
    x = xb_ref[...]
    n1 = p1_ref.shape[1]
    for j0 in range(0, n1, tn):
        j1 = min(j0 + tn, n1)
        res = jnp.dot(x, w1_ref[:, j0:j1], preferred_element_type=F32)
        if j0 == RET_QK_COL:
            cos = jnp.concatenate([rc_ref[...]] * 2, axis=1)
            sin = jnp.concatenate([rs_ref[...]] * 2, axis=1)
            res = _rotary(res, cos, sin, RET_DK, RET_DK // 2)
        p1_ref[:, j0:j1] = res.astype(BF16)
    for g, (_, dil) in enumerate(DIL_GROUPS):
        rows = tm // dil
        if dil == 1:
            lhs = x
        else:
            for c in range(dil):
                for j, slab in enumerate(slabs):
                    xp_ref[c * rows:(c + 1) * rows, j * LANES:(j + 1) * LANES] = (
                        slab[pl.ds(c, rows, stride=dil), :].astype(BF16))
            lhs = xp_ref[...]
        cos = jnp.concatenate([tabs[2 * g][...].reshape(tm, LANES)] * N_HEADS, axis=1)
        sin = jnp.concatenate([tabs[2 * g + 1][...].reshape(tm, LANES)] * N_HEADS, axis=1)
        for part in range(3):
            col = (3 * g + part) * tn
            res = jnp.dot(lhs, w2_ref[:, col:col + tn], preferred_element_type=F32)
            if part < 2:
                res = _rotary(res, cos, sin, DIL_HD, ROPE_DIMS // 2)
            g_refs[g][:, :, part * tn:(part + 1) * tn] = res.astype(BF16).reshape(dil, rows, tn)


def _proj(xf, xb, w1, w2, ret_tabs, dil_tabs, batch, seq, tm):
    t, k = xb.shape
    n1 = w1.shape[1]
    nblk = seq // tm
    n_slabs = k // LANES

    def row(b, i):
        return (b * nblk + i, 0)

    slab_specs = [pl.BlockSpec((tm, LANES), functools.partial(lambda b, i, j: (b * nblk + i, j), j=j))
                  for j in range(n_slabs)]
    tab_specs, out_specs, out_shapes = [], [], []
    for _, dil in DIL_GROUPS:
        tab_specs += [pl.BlockSpec((dil, tm // dil, LANES), lambda b, i: (0, i, 0))] * 2
        out_specs.append(pl.BlockSpec((None, dil, tm // dil, 3 * BRANCH_W), lambda b, i: (b, 0, i, 0)))
        out_shapes.append(jax.ShapeDtypeStruct((batch, dil, seq // dil, 3 * BRANCH_W), BF16))
    ret_spec = pl.BlockSpec((tm, N_HEADS * RET_DK), lambda b, i: (i, 0))
    return pl.pallas_call(
        functools.partial(_proj_kernel, n_slabs=n_slabs),
        grid=(batch, nblk),
        in_specs=[pl.BlockSpec((tm, k), row)] + slab_specs + [_resident(w1.shape), _resident(w2.shape),
                                                             ret_spec, ret_spec] + tab_specs,
        out_specs=[pl.BlockSpec((tm, n1), row)] + out_specs,
        out_shape=[jax.ShapeDtypeStruct((t, n1), BF16)] + out_shapes,
        scratch_shapes=[pltpu.VMEM((tm, k), BF16)],
        compiler_params=pltpu.CompilerParams(dimension_semantics=("parallel", "parallel"),
                                             vmem_limit_bytes=VMEM_LIMIT),
        name="in_proj",
    )(xb, *([xf] * n_slabs), w1, w2, *ret_tabs, *dil_tabs)


def _gla_kernel(q_ref, k_ref, v_ref, r_ref, a_ref, wa_ref, ba_ref, g_ref, o_ref, st_ref):
    c = GLA_CHUNK

    @pl.when(pl.program_id(1) == 0)
    def _():
        st_ref[...] = jnp.zeros_like(st_ref)

    row = lax.broadcasted_iota(jnp.int32, (c, c), 0)
    col = lax.broadcasted_iota(jnp.int32, (c, c), 1)
    causal = row >= col
    tri = causal.astype(F32)

    def chunk(ci, carry):
        r0 = pl.multiple_of(ci * c, c)
        rows = pl.ds(r0, c)
        q = q_ref[rows, :].astype(F32) * (GLA_DK ** -0.5)
        k = k_ref[rows, :].astype(F32)
        v = v_ref[rows, :]
        z = jnp.dot(a_ref[rows, :], wa_ref[...], preferred_element_type=F32) + ba_ref[...]
        log_a = (jnp.minimum(z, 0.0) - jnp.log1p(jnp.exp(-jnp.abs(z)))) * (1.0 / GLA_TAU)
        b = jnp.dot(tri, log_a, precision=lax.Precision.HIGHEST, preferred_element_type=F32)
        b_last = b[c - 1:c, :]
        q_s = (q * jnp.exp(b)).astype(BF16)
        k_s = (k * jnp.exp(-b)).astype(BF16)
        k_d = (k * jnp.exp(b_last - b)).astype(BF16)
        decay = jnp.exp(b_last)
        outs = []
        for h in range(N_HEADS):
            ks = slice(h * GLA_DK, (h + 1) * GLA_DK)
            vh = v[:, h * GLA_DV:(h + 1) * GLA_DV]
            att = jnp.where(causal, _nt_dot(q_s[:, ks], k_s[:, ks]), 0.0)
            st = st_ref[h]
            o = jnp.dot(att.astype(BF16), vh, preferred_element_type=F32) + _nt_dot(q_s[:, ks], st.astype(BF16))
            st_ref[h] = st * decay[:, ks] + _tn_dot(vh, k_d[:, ks])
            outs.append(o * lax.rsqrt(jnp.mean(o * o, axis=-1, keepdims=True) + 1e-6))
        y = jnp.concatenate(outs, axis=-1) * g_ref[...]
        r = r_ref[rows, :].astype(F32)
        o_ref[rows, :] = (y * (r * _sigmoid(r))).astype(o_ref.dtype)
        return carry

    lax.fori_loop(0, q_ref.shape[0] // c, chunk, 0)


def _gla(p1, w_a2, b_a, norm_g, batch, seq):
    t = batch * seq
    tb = SEQ_BLOCK
    nblk = seq // tb

    def at(width, col):
        return pl.BlockSpec((tb, width), lambda b, i: (b * nblk + i, col))

    return pl.pallas_call(
        _gla_kernel,
        grid=(batch, nblk),
        in_specs=[at(256, 0), at(256, 1), at(512, 1), at(512, 2), at(GA_PAD, 3072 // GA_PAD),
                  _resident(w_a2.shape), _resident(b_a.shape), _resident(norm_g.shape)],
        out_specs=pl.BlockSpec((tb, BRANCH_W), lambda b, i: (b * nblk + i, 0)),
        out_shape=jax.ShapeDtypeStruct((t, BRANCH_W), BF16),
        scratch_shapes=[pltpu.VMEM((N_HEADS, GLA_DV, GLA_DK), F32)],
        compiler_params=pltpu.CompilerParams(dimension_semantics=("parallel", "arbitrary"),
                                             vmem_limit_bytes=VMEM_LIMIT),
        name="gla",
    )(p1, p1, p1, p1, p1, w_a2, b_a, norm_g)


def _ret_kernel(q_ref, k_ref, v_ref, r_ref, g_ref, o_ref, st_ref):
    c = RET_CHUNK

    @pl.when(pl.program_id(1) == 0)
    def _():
        st_ref[...] = jnp.zeros_like(st_ref)

    log_g = [math.log1p(-(2.0 ** (-5.0 - h))) for h in range(N_HEADS)]
    diff = (lax.broadcasted_iota(jnp.int32, (c, c), 0) - lax.broadcasted_iota(jnp.int32, (c, c), 1)).astype(F32)
    ridx = lax.broadcasted_iota(jnp.int32, (c, 1), 0).astype(F32)

    def chunk(ci, carry):
        r0 = pl.multiple_of(ci * c, c)
        rows = pl.ds(r0, c)
        q = q_ref[rows, :].astype(F32)
        k = k_ref[rows, :].astype(F32) * (RET_DK ** -0.5)
        v = v_ref[rows, :]
        outs = []
        for h in range(N_HEADS):
            lg = log_g[h]
            ks = slice(h * RET_DK, (h + 1) * RET_DK)
            vh = v[:, h * RET_DV:(h + 1) * RET_DV]
            qh = q[:, ks]
            kh = k[:, ks]
            dmat = jnp.where(diff >= 0.0, jnp.exp(jnp.maximum(diff, 0.0) * lg), 0.0)
            att = _nt_dot(qh.astype(BF16), kh.astype(BF16)) * dmat
            q_dec = (qh * jnp.exp((ridx + 1.0) * lg)).astype(BF16)
            k_dec = (kh * jnp.exp((c - 1.0 - ridx) * lg)).astype(BF16)
            st = st_ref[h]
            o = jnp.dot(att.astype(BF16), vh, preferred_element_type=F32) + _nt_dot(q_dec, st.astype(BF16))
            st_ref[h] = st * math.exp(c * lg) + _tn_dot(vh, k_dec)
            mu = jnp.mean(o, axis=-1, keepdims=True)
            d = o - mu
            outs.append(d * lax.rsqrt(jnp.mean(d * d, axis=-1, keepdims=True) + 1e-5))
        y = jnp.concatenate(outs, axis=-1) * g_ref[...]
        r = r_ref[rows, :].astype(F32)
        o_ref[rows, :] = (y * (r * _sigmoid(r))).astype(o_ref.dtype)
        return carry

    lax.fori_loop(0, q_ref.shape[0] // c, chunk, 0)


def _ret(p1, norm_g, batch, seq):
    t = batch * seq
    tb = SEQ_BLOCK
    nblk = seq // tb

    def at(width, col):
        return pl.BlockSpec((tb, width), lambda b, i: (b * nblk + i, col))

    return pl.pallas_call(
        _ret_kernel,
        grid=(batch, nblk),
        in_specs=[at(256, 6), at(256, 7), at(512, 4), at(512, 5), _resident(norm_g.shape)],
        out_specs=pl.BlockSpec((tb, BRANCH_W), lambda b, i: (b * nblk + i, 0)),
        out_shape=jax.ShapeDtypeStruct((t, BRANCH_W), BF16),
        scratch_shapes=[pltpu.VMEM((N_HEADS, RET_DV, RET_DK), F32)],
        compiler_params=pltpu.CompilerParams(dimension_semantics=("parallel", "arbitrary"),
                                             vmem_limit_bytes=VMEM_LIMIT),
        name="retention",
    )(p1, p1, p1, p1, norm_g)


def _dil_kernel(q_ref, k_ref, v_ref, kp_ref, vp_ref, o_ref, l_ref, o_scr, l_scr, *, dilation):
    n = DIL_SPAN
    nj = q_ref.shape[1] // n
    row = lax.broadcasted_iota(jnp.int32, (n, 2 * n), 0)
    col = lax.broadcasted_iota(jnp.int32, (n, 2 * n), 1)
    dist = row + n - col
    band = (dist >= 0) & (dist <= n)
    band_first = band & ((col >= n) | (pl.program_id(1) > 0))
    lane_head = lax.broadcasted_iota(jnp.int32, (n, LANES), 1) // (LANES // N_HEADS)
    scale = DIL_HD ** -0.5
    scale_log2 = scale * math.log2(math.e)

    def unit(c, j, mask, keys, values):
        start = j * (n * dilation) + c
        dst = pl.ds(start, n, stride=dilation) if dilation > 1 else pl.ds(start, n)
        q = q_ref[c, pl.ds(j * n, n), :]
        lse_tile = jnp.zeros((n, LANES), F32)
        for h in range(N_HEADS):
            hs = slice(h * DIL_HD, (h + 1) * DIL_HD)
            s = jnp.where(mask, _nt_dot(q[:, hs], keys(hs)), MASK_VALUE)
            m = jnp.max(s, axis=-1, keepdims=True)
            p = jnp.exp2((s - m) * scale_log2)
            den = jnp.sum(p, axis=-1, keepdims=True)
            o = jnp.dot(p.astype(BF16), values(hs), preferred_element_type=F32) / den
            o_scr[h, dst, :] = o
            lse_tile = jnp.where(lane_head == h, m * scale + jnp.log(den), lse_tile)
        l_scr[dst, :] = lse_tile

    def first_unit(c, carry):
        unit(c, 0, band_first,
             lambda hs: jnp.concatenate([kp_ref[c, :, hs], k_ref[c, pl.ds(0, n), hs]], axis=0),
             lambda hs: jnp.concatenate([vp_ref[c, :, hs], v_ref[c, pl.ds(0, n), hs]], axis=0))
        return carry

    def later_unit(u, carry):
        c = u // (nj - 1)
        j = 1 + u % (nj - 1)
        r0 = pl.multiple_of((j - 1) * n, n)
        unit(c, j, band,
             lambda hs: k_ref[c, pl.ds(r0, 2 * n), hs],
             lambda hs: v_ref[c, pl.ds(r0, 2 * n), hs])
        return carry

    lax.fori_loop(0, dilation, first_unit, 0)
    if nj > 1:
        lax.fori_loop(0, dilation * (nj - 1), later_unit, 0)
    for h in range(N_HEADS):
        o_ref[:, h * DIL_HD:(h + 1) * DIL_HD] = o_scr[h].astype(o_ref.dtype)
    l_ref[...] = l_scr[...]


def _dil(pg, dilation, batch, seq):
    t = batch * seq
    blk = DIL_BLOCK
    nblk = seq // blk
    rows = blk // dilation
    prev_per_blk = rows // DIL_SPAN

    def cur(part):
        return pl.BlockSpec((None, dilation, rows, BRANCH_W), lambda b, i: (b, 0, i, part))

    def prev(part):
        return pl.BlockSpec((None, dilation, DIL_SPAN, BRANCH_W),
                            lambda b, i: (b, 0, jnp.maximum(i * prev_per_blk - 1, 0), part))

    return pl.pallas_call(
        functools.partial(_dil_kernel, dilation=dilation),
        grid=(batch, nblk),
        in_specs=[cur(0), cur(1), cur(2), prev(1), prev(2)],
        out_specs=[pl.BlockSpec((blk, BRANCH_W), lambda b, i: (b * nblk + i, 0)),
                   pl.BlockSpec((blk, LANES), lambda b, i: (b * nblk + i, 0))],
        out_shape=[jax.ShapeDtypeStruct((t, BRANCH_W), BF16), jax.ShapeDtypeStruct((t, LANES), F32)],
        scratch_shapes=[pltpu.VMEM((N_HEADS, blk, DIL_HD), F32), pltpu.VMEM((blk, LANES), F32)],
        compiler_params=pltpu.CompilerParams(dimension_semantics=("parallel", "arbitrary"),
                                             vmem_limit_bytes=VMEM_LIMIT),
        name=f"dilated_attn_{dilation}",
    )(pg, pg, pg, pg, pg)


def _layer_norm(v, g, b):
    mu = jnp.mean(v, axis=-1, keepdims=True)
    d = v - mu
    return d * lax.rsqrt(jnp.mean(d * d, axis=-1, keepdims=True) + 1e-5) * g + b


def _merge_kernel(x_ref, xb_ref, oa_ref, ob_ref, o1_ref, o2_ref, o3_ref, l1_ref, l2_ref, l3_ref,
                  wg_ref, bg_ref, wb_ref, wo_ref, lg_ref, lb_ref, y_ref, yb_ref, *, alpha):
    d = x_ref.shape[1]
    l1, l2, l3 = l1_ref[...], l2_ref[...], l3_ref[...]
    m = jnp.maximum(jnp.maximum(l1, l2), l3)
    e1, e2, e3 = jnp.exp(l1 - m), jnp.exp(l2 - m), jnp.exp(l3 - m)
    tot = e1 + e2 + e3
    w1, w2, w3 = e1 / tot, e2 / tot, e3 / tot
    lanes_per_head = LANES // N_HEADS
    parts = []
    for h in range(N_HEADS):
        hs = slice(h * DIL_HD, (h + 1) * DIL_HD)
        ls = slice(h * lanes_per_head, h * lanes_per_head + 1)
        parts.append(w1[:, ls] * o1_ref[:, hs].astype(F32) + w2[:, ls] * o2_ref[:, hs].astype(F32)
                     + w3[:, ls] * o3_ref[:, hs].astype(F32))
    oc = jnp.concatenate(parts, axis=-1).astype(BF16)
    xb = xb_ref[...]
    merged = None
    for j, o in enumerate((oa_ref[...], ob_ref[...], oc)):
        gate = _sigmoid(jnp.dot(xb, wg_ref[:, j * d:(j + 1) * d], preferred_element_type=F32)
                        + bg_ref[:, j * d:(j + 1) * d])
        term = gate * jnp.dot(o, wb_ref[j], preferred_element_type=F32)
        merged = term if merged is None else merged + term
    hmix = jnp.dot(merged.astype(BF16), wo_ref[...], preferred_element_type=F32)
    y = _layer_norm(alpha * x_ref[...] + hmix, lg_ref[...], lb_ref[...])
    y_ref[...] = y
    yb_ref[...] = y.astype(BF16)


def _merge(x, xb, oa, ob, o_dil, l_dil, w_gate, b_gate, w_branch, w_out, ln_g, ln_b, alpha, tm):
    t, d = x.shape

    def rows(width):
        return pl.BlockSpec((tm, width), lambda i: (i, 0))

    return pl.pallas_call(
        functools.partial(_merge_kernel, alpha=alpha),
        grid=(t // tm,),
        in_specs=[rows(d), rows(d)] + [rows(BRANCH_W)] * 5 + [rows(LANES)] * 3
                 + [_resident(w_gate.shape), _resident(b_gate.shape), _resident(w_branch.shape),
                    _resident(w_out.shape), _resident(ln_g.shape), _resident(ln_b.shape)],
        out_specs=[rows(d), rows(d)],
        out_shape=[jax.ShapeDtypeStruct((t, d), F32), jax.ShapeDtypeStruct((t, d), BF16)],
        compiler_params=pltpu.CompilerParams(dimension_semantics=("parallel",), vmem_limit_bytes=VMEM_LIMIT),
        name="merge_out_ln",
    )(x, xb, oa, ob, *o_dil, *l_dil, w_gate, b_gate, w_branch, w_out, ln_g, ln_b)


def _ffn_kernel(x_ref, xb_ref, wg_ref, wu_ref, wd_ref, lg_ref, lb_ref, y_ref, yb_ref, act_ref, *, alpha):
    xb = xb_ref[...]
    d_ff = wg_ref.shape[1]
    for j0 in range(0, d_ff, FF_CHUNK):
        cs = slice(j0, j0 + FF_CHUNK)
        g = jnp.dot(xb, wg_ref[:, cs], preferred_element_type=F32)
        u = jnp.dot(xb, wu_ref[:, cs], preferred_element_type=F32)
        act_ref[:, cs] = (g * _sigmoid(g) * u).astype(BF16)
    f = jnp.dot(act_ref[...], wd_ref[...], preferred_element_type=F32)
    y = _layer_norm(alpha * x_ref[...] + f, lg_ref[...], lb_ref[...])
    y_ref[...] = y
    yb_ref[...] = y.astype(BF16)


def _ffn(x, xb, w_gate, w_up, w_down, ln_g, ln_b, alpha, tm):
    t, d = x.shape
    d_ff = w_gate.shape[1]
    rows = pl.BlockSpec((tm, d), lambda i: (i, 0))
    return pl.pallas_call(
        functools.partial(_ffn_kernel, alpha=alpha),
        grid=(t // tm,),
        in_specs=[rows, rows, _resident(w_gate.shape), _resident(w_up.shape), _resident(w_down.shape),
                  _resident(ln_g.shape), _resident(ln_b.shape)],
        out_specs=[rows, rows],
        out_shape=[jax.ShapeDtypeStruct((t, d), F32), jax.ShapeDtypeStruct((t, d), BF16)],
        scratch_shapes=[pltpu.VMEM((tm, d_ff), BF16)],
        compiler_params=pltpu.CompilerParams(dimension_semantics=("parallel",), vmem_limit_bytes=VMEM_LIMIT),
        name="swiglu_ln",
    )(x, xb, w_gate, w_up, w_down, ln_g, ln_b)


def _rope_tables(seq):
    pos = jnp.arange(seq, dtype=F32)

    def tables(n_rot, base, width):
        half = n_rot // 2
        inv = base ** (-jnp.arange(half, dtype=F32) * 2.0 / n_rot)
        ang = pos[:, None] * inv[None, :]
        cos = jnp.concatenate([jnp.cos(ang), jnp.cos(ang), jnp.ones((seq, width - n_rot), F32)], axis=1)
        sin = jnp.concatenate([-jnp.sin(ang), jnp.sin(ang), jnp.zeros((seq, width - n_rot), F32)], axis=1)
        return cos, sin

    rc, rs = tables(RET_DK, RET_ROPE_BASE, RET_DK)
    dc, ds = tables(ROPE_DIMS, ROPE_THETA, DIL_HD)
    dil_tabs = []
    for _, dil in DIL_GROUPS:
        for tab in (dc, ds):
            dil_tabs.append(tab.reshape(seq // dil, dil, DIL_HD).transpose(1, 0, 2))
    return (jnp.tile(rc, (1, N_HEADS)), jnp.tile(rs, (1, N_HEADS))), dil_tabs


def _split_w_in(w_in):
    d = w_in.shape[0]
    widths = (256, 256, 512, 512, GLA_RANK, 256, 256, 512, 512, 1536, 1536, 1536, 3 * d)
    offs = [0]
    for w in widths:
        offs.append(offs[-1] + w)
    seg = [w_in[:, offs[i]:offs[i + 1]] for i in range(len(widths))]
    gq, gk, gv, gr, ga, rq, rk, rv, rg, dq, dk, dv, gate = seg
    ga = jnp.pad(ga, ((0, 0), (0, GA_PAD - GLA_RANK)))
    w1 = jnp.concatenate([gq, gk, gv, gr, rq, rk, rv, rg, ga], axis=1)
    dil = []
    for g in range(len(DIL_GROUPS)):
        gs = slice(g * BRANCH_W, (g + 1) * BRANCH_W)
        dil += [dq[:, gs], dk[:, gs], dv[:, gs]]
    w2 = jnp.concatenate(dil, axis=1)
    return w1.astype(BF16), w2.astype(BF16), gate.astype(BF16)


def kernel(x, w_in, w_gla_a2, b_gla_a, gla_norm_g, ret_norm_g, w_branch, b_gate, w_out, ln1_g, ln1_b,
           w_ffn_gate, w_ffn_up, w_ffn_down, ln2_g, ln2_b):
    batch, seq, d = x.shape
    depth = w_in.shape[0]
    alpha = (2 * depth) ** 0.25
    t = batch * seq
    ret_tabs, dil_tabs = _rope_tables(seq)
    xf = x.reshape(t, d)
    xb = xf.astype(BF16)
    for l in range(depth):
        w1, w2, w_gate = _split_w_in(w_in[l])
        w_a2 = jnp.pad(w_gla_a2[l], ((0, GA_PAD - GLA_RANK), (0, 0))).astype(BF16)
        p1, *p_dil = _proj(xf, xb, w1, w2, ret_tabs, dil_tabs, batch, seq, ROW_TILE)
        oa = _gla(p1, w_a2, b_gla_a[l][None, :], gla_norm_g[l][None, :], batch, seq)
        ob = _ret(p1, ret_norm_g[l][None, :], batch, seq)
        o_dil, l_dil = [], []
        for pg, (_, dilation) in zip(p_dil, DIL_GROUPS):
            o, lse = _dil(pg, dilation, batch, seq)
            o_dil.append(o)
            l_dil.append(lse)
        xf, xb = _merge(xf, xb, oa, ob, o_dil, l_dil, w_gate, b_gate[l][None, :], w_branch[l].astype(BF16),
                        w_out[l].astype(BF16), ln1_g[l][None, :], ln1_b[l][None, :], alpha, ROW_TILE)
        xf, xb = _ffn(xf, xb, w_ffn_gate[l].astype(BF16), w_ffn_up[l].astype(BF16), w_ffn_down[l].astype(BF16),
                      ln2_g[l][None, :], ln2_b[l][None, :], alpha, ROW_TILE)
    return xf.reshape(batch, seq, d)
```

```python
import functools
import math

import jax
import jax.numpy as jnp
from jax import lax
from jax.experimental import pallas as pl
from jax.experimental.pallas import tpu as pltpu

F32 = jnp.float32
BF16 = jnp.bfloat16

LANES = 128
N_HEADS = 4
GLA_DK = 64
GLA_DV = 128
GLA_RANK = 16
GLA_TAU = 16.0
GLA_CHUNK = 64
GLA_GROUP = 256
RET_DK = 64
RET_DV = 128
RET_CHUNK = 128
RET_GROUP = 256
RET_ROPE_BASE = 10000.0
DIL_HD = 128
DIL_GROUPS = ((128, 1), (512, 4), (2048, 16))
DIL_SPAN = 128
DIL_BLOCK = DIL_SPAN * max(d for _, d in DIL_GROUPS)
ROPE_THETA = 500000.0
ROPE_DIMS = 32
BRANCH_W = 512
GA_PAD = 256
RET_QK_COL = 1536
MASK_VALUE = -1e30

VMEM_LIMIT = 56 * 1024 * 1024
SEQ_BLOCK = 512
ROW_TILE = 512
FF_CHUNK = 256
DIL_UNROLL = 4


def _sigmoid(x):
    return 1.0 / (1.0 + jnp.exp(-x))


def _nt_dot(a, b):
    return lax.dot_general(a, b, (((1,), (1,)), ((), ())), preferred_element_type=F32)


def _tn_dot(a, b):
    return lax.dot_general(a, b, (((0,), (0,)), ((), ())), preferred_element_type=F32)


def _resident(shape):
    nd = len(shape)
    return pl.BlockSpec(shape, lambda *_: (0,) * nd, pipeline_mode=pl.Buffered(1))


def _rotary(x, cos, sin, head_dim, half):
    width = x.shape[1]
    low = (lax.broadcasted_iota(jnp.int32, x.shape, 1) % head_dim) < half
    partner = jnp.where(low, pltpu.roll(x, width - half, 1), pltpu.roll(x, half, 1))
    return x * cos + partner * sin


def _proj_kernel(*refs, n_slabs):
    n_groups = len(DIL_GROUPS)
    xb_ref = refs[0]
    slabs = refs[1:1 + n_slabs]
    w1_ref, w2_ref, rc_ref, rs_ref = refs[1 + n_slabs:5 + n_slabs]
    tabs = refs[5 + n_slabs:5 + n_slabs + 2 * n_groups]
    p1_ref = refs[5 + n_slabs + 2 * n_groups]
    g_refs = refs[6 + n_slabs + 2 * n_groups:6 + n_slabs + 3 * n_groups]
    xp_ref = refs[-1]
    tm = xb_ref.shape[0]
    tn = BRANCH_W
    x = xb_ref[...]
    n1 = p1_ref.shape[1]
    for j0 in range(0, n1, tn):
        j1 = min(j0 + tn, n1)
        res = jnp.dot(x, w1_ref[:, j0:j1], preferred_element_type=F32)
        if j0 == RET_QK_COL:
            cos = jnp.concatenate([rc_ref[...]] * 2, axis=1)
            sin = jnp.concatenate([rs_ref[...]] * 2, axis=1)
            res = _rotary(res, cos, sin, RET_DK, RET_DK // 2)
        p1_ref[:, j0:j1] = res.astype(BF16)
    for g, (_, dil) in enumerate(DIL_GROUPS):
        rows = tm // dil
        if dil == 1:
            lhs = x
        else:
            for c in range(dil):
                for j, slab in enumerate(slabs):
                    xp_ref[c * rows:(c + 1) * rows, j * LANES:(j + 1) * LANES] = (
                        slab[pl.ds(c, rows, stride=dil), :].astype(BF16))
            lhs = xp_ref[...]
        cos = jnp.concatenate([tabs[2 * g][...].reshape(tm, LANES)] * N_HEADS, axis=1)
        sin = jnp.concatenate([tabs[2 * g + 1][...].reshape(tm, LANES)] * N_HEADS, axis=1)
        for part in range(3):
            col = (3 * g + part) * tn
            res = jnp.dot(lhs, w2_ref[:, col:col + tn], preferred_element_type=F32)
            if part < 2:
                res = _rotary(res, cos, sin, DIL_HD, ROPE_DIMS // 2)
            g_refs[g][:, :, part * tn:(part + 1) * tn] = res.astype(BF16).reshape(dil, rows, tn)


def _proj(xf, xb, w1, w2, ret_tabs, dil_tabs, batch, seq, tm):
    t, k = xb.shape
    n1 = w1.shape[1]
    nblk = seq // tm
    n_slabs = k // LANES

    def row(b, i):
        return (b * nblk + i, 0)

    slab_specs = [pl.BlockSpec((tm, LANES), functools.partial(lambda b, i, j: (b * nblk + i, j), j=j))
                  for j in range(n_slabs)]
    tab_specs, out_specs, out_shapes = [], [], []
    for _, dil in DIL_GROUPS:
        tab_specs += [pl.BlockSpec((dil, tm // dil, LANES), lambda b, i: (0, i, 0))] * 2
        out_specs.append(pl.BlockSpec((None, dil, tm // dil, 3 * BRANCH_W), lambda b, i: (b, 0, i, 0)))
        out_shapes.append(jax.ShapeDtypeStruct((batch, dil, seq // dil, 3 * BRANCH_W), BF16))
    ret_spec = pl.BlockSpec((tm, N_HEADS * RET_DK), lambda b, i: (i, 0))
    return pl.pallas_call(
        functools.partial(_proj_kernel, n_slabs=n_slabs),
        grid=(batch, nblk),
        in_specs=[pl.BlockSpec((tm, k), row)] + slab_specs + [_resident(w1.shape), _resident(w2.shape),
                                                             ret_spec, ret_spec] + tab_specs,
        out_specs=[pl.BlockSpec((tm, n1), row)] + out_specs,
        out_shape=[jax.ShapeDtypeStruct((t, n1), BF16)] + out_shapes,
        scratch_shapes=[pltpu.VMEM((tm, k), BF16)],
        compiler_params=pltpu.CompilerParams(dimension_semantics=("parallel", "parallel"),
                                             vmem_limit_bytes=VMEM_LIMIT),
        name="in_proj",
    )(xb, *([xf] * n_slabs), w1, w2, *ret_tabs, *dil_tabs)


def _gla_kernel(q_ref, k_ref, v_ref, r_ref, a_ref, wa_ref, ba_ref, g_ref, o_ref, st_ref, qs_ref, kd_ref, acc_ref):
    c = GLA_CHUNK
    grp = GLA_GROUP
    tb = q_ref.shape[0]
    width = N_HEADS * GLA_DK

    @pl.when(pl.program_id(1) == 0)
    def _():
        st_ref[...] = jnp.zeros_like(st_ref)

    z = jnp.dot(a_ref[...], wa_ref[...], preferred_element_type=F32) + ba_ref[...]
    log_a = (jnp.minimum(z, 0.0) - jnp.log1p(jnp.exp(-jnp.abs(z)))) * (1.0 / GLA_TAU)
    hi = log_a.astype(BF16)
    lo = (log_a - hi.astype(F32)).astype(BF16)
    hl = jnp.concatenate([hi, lo], axis=1)

    row = lax.broadcasted_iota(jnp.int32, (grp, grp), 0)
    col = lax.broadcasted_iota(jnp.int32, (grp, grp), 1)
    same_chunk = (row // c) == (col // c)
    causal = same_chunk & (row >= col)
    cum_and_total = jnp.concatenate([jnp.where(causal, 1.0, 0.0), jnp.where(same_chunk, 1.0, 0.0)],
                                    axis=0).astype(BF16)
    lane_head = lax.broadcasted_iota(jnp.int32, (grp, width), 1) // GLA_DK
    st_row_head = lax.broadcasted_iota(jnp.int32, st_ref.shape, 0) // GLA_DV
    st_col_head = lax.broadcasted_iota(jnp.int32, st_ref.shape, 1) // GLA_DK
    st_diag = st_row_head == st_col_head

    decays = []
    for g0 in range(0, tb, grp):
        rows = slice(g0, g0 + grp)
        r2 = jnp.dot(cum_and_total, hl[rows], preferred_element_type=F32)
        b = r2[:grp, :width] + r2[:grp, width:]
        b_tot = r2[grp:, :width] + r2[grp:, width:]
        q = q_ref[rows, :].astype(F32) * (GLA_DK ** -0.5)
        k = k_ref[rows, :].astype(F32)
        q_s = (q * jnp.exp(b)).astype(BF16)
        k_s = k * jnp.exp(-b)
        qs_ref[rows, :] = q_s
        kd_ref[rows, :] = (k * jnp.exp(b_tot - b)).astype(BF16)
        for n0 in range(0, grp, c):
            decays.append(jnp.exp(b_tot[n0:n0 + 1, :]))
        for h in range(N_HEADS):
            hs = slice(h * GLA_DV, (h + 1) * GLA_DV)
            k_h = jnp.where(lane_head == h, k_s, 0.0).astype(BF16)
            att = jnp.where(causal, _nt_dot(q_s, k_h), 0.0).astype(BF16)
            acc_ref[rows, hs] = jnp.dot(att, v_ref[rows, hs], preferred_element_type=F32)

    for n, decay in enumerate(decays):
        rows = slice(n * c, (n + 1) * c)
        st = st_ref[...]
        acc_ref[rows, :] += _nt_dot(qs_ref[rows, :], st.astype(BF16))
        kv_t = _tn_dot(v_ref[rows, :], kd_ref[rows, :])
        st_ref[...] = st * decay + jnp.where(st_diag, kv_t, 0.0)

    outs = []
    for h in range(N_HEADS):
        o = acc_ref[:, h * GLA_DV:(h + 1) * GLA_DV]
        outs.append(o * lax.rsqrt(jnp.mean(o * o, axis=-1, keepdims=True) + 1e-6))
    y = jnp.concatenate(outs, axis=-1) * g_ref[...]
    r = r_ref[...].astype(F32)
    o_ref[...] = (y * (r * _sigmoid(r))).astype(o_ref.dtype)


def _gla(p1, w_a2, b_a, norm_g, batch, seq):
    t = batch * seq
    tb = SEQ_BLOCK
    nblk = seq // tb

    def at(width, col):
        return pl.BlockSpec((tb, width), lambda b, i: (b * nblk + i, col))

    return pl.pallas_call(
        _gla_kernel,
        grid=(batch, nblk),
        in_specs=[at(256, 0), at(256, 1), at(512, 1), at(512, 2), at(GA_PAD, 3072 // GA_PAD),
                  _resident(w_a2.shape), _resident(b_a.shape), _resident(norm_g.shape)],
        out_specs=pl.BlockSpec((tb, BRANCH_W), lambda b, i: (b * nblk + i, 0)),
        out_shape=jax.ShapeDtypeStruct((t, BRANCH_W), BF16),
        scratch_shapes=[pltpu.VMEM((N_HEADS * GLA_DV, N_HEADS * GLA_DK), F32),
                        pltpu.VMEM((tb, N_HEADS * GLA_DK), BF16), pltpu.VMEM((tb, N_HEADS * GLA_DK), BF16),
                        pltpu.VMEM((tb, BRANCH_W), F32)],
        compiler_params=pltpu.CompilerParams(dimension_semantics=("parallel", "arbitrary"),
                                             vmem_limit_bytes=VMEM_LIMIT),
        name="gla",
    )(p1, p1, p1, p1, p1, w_a2, b_a, norm_g)


def _ret_kernel(q_ref, k_ref, v_ref, r_ref, g_ref, o_ref, st_ref, qd_ref, kd_ref, acc_ref):
    c = RET_CHUNK
    grp = RET_GROUP
    tb = q_ref.shape[0]
    width = N_HEADS * RET_DK

    @pl.when(pl.program_id(1) == 0)
    def _():
        st_ref[...] = jnp.zeros_like(st_ref)

    log_g = [math.log1p(-(2.0 ** (-5.0 - h))) for h in range(N_HEADS)]
    lane_head = lax.broadcasted_iota(jnp.int32, (grp, width), 1) // RET_DK
    lg_lane = jnp.zeros((grp, width), F32)
    for h in range(N_HEADS):
        lg_lane = jnp.where(lane_head == h, log_g[h], lg_lane)
    ridx = (lax.broadcasted_iota(jnp.int32, (grp, width), 0) % c).astype(F32)
    q_scale = jnp.exp((ridx + 1.0) * lg_lane)
    k_scale = jnp.exp((c - 1.0 - ridx) * lg_lane)
    chunk_decay = jnp.exp(c * lg_lane[0:1, :])
    row = lax.broadcasted_iota(jnp.int32, (grp, grp), 0)
    col = lax.broadcasted_iota(jnp.int32, (grp, grp), 1)
    causal = ((row // c) == (col // c)) & (row >= col)
    diff = (row - col).astype(F32)
    st_diag = (lax.broadcasted_iota(jnp.int32, st_ref.shape, 0) // RET_DV
               == lax.broadcasted_iota(jnp.int32, st_ref.shape, 1) // RET_DK)

    for g0 in range(0, tb, grp):
        rows = slice(g0, g0 + grp)
        q = q_ref[rows, :].astype(F32)
        k = k_ref[rows, :].astype(F32) * (RET_DK ** -0.5)
        qb = q.astype(BF16)
        qd_ref[rows, :] = (q * q_scale).astype(BF16)
        kd_ref[rows, :] = (k * k_scale).astype(BF16)
        for h in range(N_HEADS):
            hs = slice(h * RET_DV, (h + 1) * RET_DV)
            dmat = jnp.where(causal, jnp.exp(jnp.maximum(diff, 0.0) * log_g[h]), 0.0)
            k_h = jnp.where(lane_head == h, k, 0.0).astype(BF16)
            att = (_nt_dot(qb, k_h) * dmat).astype(BF16)
            acc_ref[rows, hs] = jnp.dot(att, v_ref[rows, hs], preferred_element_type=F32)

    for n in range(tb // c):
        rows = slice(n * c, (n + 1) * c)
        st = st_ref[...]
        acc_ref[rows, :] += _nt_dot(qd_ref[rows, :], st.astype(BF16))
        kv_t = _tn_dot(v_ref[rows, :], kd_ref[rows, :])
        st_ref[...] = st * chunk_decay + jnp.where(st_diag, kv_t, 0.0)

    outs = []
    for h in range(N_HEADS):
        o = acc_ref[:, h * RET_DV:(h + 1) * RET_DV]
        mu = jnp.mean(o, axis=-1, keepdims=True)
        d = o - mu
        outs.append(d * lax.rsqrt(jnp.mean(d * d, axis=-1, keepdims=True) + 1e-5))
    y = jnp.concatenate(outs, axis=-1) * g_ref[...]
    r = r_ref[...].astype(F32)
    o_ref[...] = (y * (r * _sigmoid(r))).astype(o_ref.dtype)


def _ret(p1, norm_g, batch, seq):
    t = batch * seq
    tb = SEQ_BLOCK
    nblk = seq // tb

    def at(width, col):
        return pl.BlockSpec((tb, width), lambda b, i: (b * nblk + i, col))

    return pl.pallas_call(
        _ret_kernel,
        grid=(batch, nblk),
        in_specs=[at(256, 6), at(256, 7), at(512, 4), at(512, 5), _resident(norm_g.shape)],
        out_specs=pl.BlockSpec((tb, BRANCH_W), lambda b, i: (b * nblk + i, 0)),
        out_shape=jax.ShapeDtypeStruct((t, BRANCH_W), BF16),
        scratch_shapes=[pltpu.VMEM((N_HEADS * RET_DV, N_HEADS * RET_DK), F32),
                        pltpu.VMEM((tb, N_HEADS * RET_DK), BF16), pltpu.VMEM((tb, N_HEADS * RET_DK), BF16),
                        pltpu.VMEM((tb, BRANCH_W), F32)],
        compiler_params=pltpu.CompilerParams(dimension_semantics=("parallel", "arbitrary"),
                                             vmem_limit_bytes=VMEM_LIMIT),
        name="retention",
    )(p1, p1, p1, p1, norm_g)


def _dil_kernel(q_ref, k_ref, v_ref, kp_ref, vp_ref, o_ref, l_ref, o_scr, l_scr, *, dilation):
    n = DIL_SPAN
    nj = q_ref.shape[1] // n
    row = lax.broadcasted_iota(jnp.int32, (n, 2 * n), 0)
    col = lax.broadcasted_iota(jnp.int32, (n, 2 * n), 1)
    dist = row + n - col
    band = (dist >= 0) & (dist <= n)
    band_first = band & ((col >= n) | (pl.program_id(1) > 0))
    lane_head = lax.broadcasted_iota(jnp.int32, (n, LANES), 1) // (LANES // N_HEADS)
    scale = DIL_HD ** -0.5
    scale_log2 = scale * math.log2(math.e)

    def unit(c, j, mask, keys, values):
        start = j * (n * dilation) + c
        dst = pl.ds(start, n, stride=dilation) if dilation > 1 else pl.ds(start, n)
        q = q_ref[c, pl.ds(j * n, n), :]
        lse_tile = jnp.zeros((n, LANES), F32)
        for h in range(N_HEADS):
            hs = slice(h * DIL_HD, (h + 1) * DIL_HD)
            s = jnp.where(mask, _nt_dot(q[:, hs], keys(hs)), MASK_VALUE)
            m = jnp.max(s, axis=-1, keepdims=True)
            p = jnp.exp2((s - m) * scale_log2)
            den = jnp.sum(p, axis=-1, keepdims=True)
            o = jnp.dot(p.astype(BF16), values(hs), preferred_element_type=F32) / den
            o_scr[h, dst, :] = o
            lse_tile = jnp.where(lane_head == h, m * scale + jnp.log(den), lse_tile)
        l_scr[dst, :] = lse_tile

    def first_unit(c, carry):
        unit(c, 0, band_first,
             lambda hs: jnp.concatenate([kp_ref[c, :, hs], k_ref[c, pl.ds(0, n), hs]], axis=0),
             lambda hs: jnp.concatenate([vp_ref[c, :, hs], v_ref[c, pl.ds(0, n), hs]], axis=0))
        return carry

    def later_unit(u, carry):
        c = u // (nj - 1)
        j = 1 + u % (nj - 1)
        r0 = pl.multiple_of((j - 1) * n, n)
        unit(c, j, band,
             lambda hs: k_ref[c, pl.ds(r0, 2 * n), hs],
             lambda hs: v_ref[c, pl.ds(r0, 2 * n), hs])
        return carry

    lax.fori_loop(0, dilation, first_unit, 0, unroll=min(dilation, DIL_UNROLL))
    if nj > 1:
        lax.fori_loop(0, dilation * (nj - 1), later_unit, 0, unroll=DIL_UNROLL)
    for h in range(N_HEADS):
        o_ref[:, h * DIL_HD:(h + 1) * DIL_HD] = o_scr[h].astype(o_ref.dtype)
    l_ref[...] = l_scr[...]


def _dil(pg, dilation, batch, seq):
    t = batch * seq
    blk = DIL_BLOCK
    nblk = seq // blk
    rows = blk // dilation
    prev_per_blk = rows // DIL_SPAN

    def cur(part):
        return pl.BlockSpec((None, dilation, rows, BRANCH_W), lambda b, i: (b, 0, i, part))

    def prev(part):
        return pl.BlockSpec((None, dilation, DIL_SPAN, BRANCH_W),
                            lambda b, i: (b, 0, jnp.maximum(i * prev_per_blk - 1, 0), part))

    return pl.pallas_call(
        functools.partial(_dil_kernel, dilation=dilation),
        grid=(batch, nblk),
        in_specs=[cur(0), cur(1), cur(2), prev(1), prev(2)],
        out_specs=[pl.BlockSpec((blk, BRANCH_W), lambda b, i: (b * nblk + i, 0)),
                   pl.BlockSpec((blk, LANES), lambda b, i: (b * nblk + i, 0))],
        out_shape=[jax.ShapeDtypeStruct((t, BRANCH_W), BF16), jax.ShapeDtypeStruct((t, LANES), F32)],
        scratch_shapes=[pltpu.VMEM((N_HEADS, blk, DIL_HD), F32), pltpu.VMEM((blk, LANES), F32)],
        compiler_params=pltpu.CompilerParams(dimension_semantics=("parallel", "arbitrary"),
                                             vmem_limit_bytes=VMEM_LIMIT),
        name=f"dilated_attn_{dilation}",
    )(pg, pg, pg, pg, pg)


def _layer_norm(v, g, b):
    mu = jnp.mean(v, axis=-1, keepdims=True)
    d = v - mu
    return d * lax.rsqrt(jnp.mean(d * d, axis=-1, keepdims=True) + 1e-5) * g + b


def _merge_kernel(x_ref, xb_ref, oa_ref, ob_ref, o1_ref, o2_ref, o3_ref, l1_ref, l2_ref, l3_ref,
                  wg_ref, bg_ref, wb_ref, wo_ref, lg_ref, lb_ref, y_ref, yb_ref, *, alpha):
    d = x_ref.shape[1]
    l1, l2, l3 = l1_ref[...], l2_ref[...], l3_ref[...]
    m = jnp.maximum(jnp.maximum(l1, l2), l3)
    e1, e2, e3 = jnp.exp(l1 - m), jnp.exp(l2 - m), jnp.exp(l3 - m)
    tot = e1 + e2 + e3
    w1, w2, w3 = e1 / tot, e2 / tot, e3 / tot
    lanes_per_head = LANES // N_HEADS
    parts = []
    for h in range(N_HEADS):
        hs = slice(h * DIL_HD, (h + 1) * DIL_HD)
        ls = slice(h * lanes_per_head, h * lanes_per_head + 1)
        parts.append(w1[:, ls] * o1_ref[:, hs].astype(F32) + w2[:, ls] * o2_ref[:, hs].astype(F32)
                     + w3[:, ls] * o3_ref[:, hs].astype(F32))
    oc = jnp.concatenate(parts, axis=-1).astype(BF16)
    xb = xb_ref[...]
    merged = None
    for j, o in enumerate((oa_ref[...], ob_ref[...], oc)):
        gate = _sigmoid(jnp.dot(xb, wg_ref[:, j * d:(j + 1) * d], preferred_element_type=F32)
                        + bg_ref[:, j * d:(j + 1) * d])
        term = gate * jnp.dot(o, wb_ref[j], preferred_element_type=F32)
        merged = term if merged is None else merged + term
    hmix = jnp.dot(merged.astype(BF16), wo_ref[...], preferred_element_type=F32)
    y = _layer_norm(alpha * x_ref[...] + hmix, lg_ref[...], lb_ref[...])
    y_ref[...] = y
    yb_ref[...] = y.astype(BF16)


def _merge(x, xb, oa, ob, o_dil, l_dil, w_gate, b_gate, w_branch, w_out, ln_g, ln_b, alpha, tm):
    t, d = x.shape

    def rows(width):
        return pl.BlockSpec((tm, width), lambda i: (i, 0))

    return pl.pallas_call(
        functools.partial(_merge_kernel, alpha=alpha),
        grid=(t // tm,),
        in_specs=[rows(d), rows(d)] + [rows(BRANCH_W)] * 5 + [rows(LANES)] * 3
                 + [_resident(w_gate.shape), _resident(b_gate.shape), _resident(w_branch.shape),
                    _resident(w_out.shape), _resident(ln_g.shape), _resident(ln_b.shape)],
        out_specs=[rows(d), rows(d)],
        out_shape=[jax.ShapeDtypeStruct((t, d), F32), jax.ShapeDtypeStruct((t, d), BF16)],
        compiler_params=pltpu.CompilerParams(dimension_semantics=("parallel",), vmem_limit_bytes=VMEM_LIMIT),
        name="merge_out_ln",
    )(x, xb, oa, ob, *o_dil, *l_dil, w_gate, b_gate, w_branch, w_out, ln_g, ln_b)


def _ffn_kernel(x_ref, xb_ref, wg_ref, wu_ref, wd_ref, lg_ref, lb_ref, y_ref, yb_ref, act_ref, *, alpha):
    xb = xb_ref[...]
    d_ff = wg_ref.shape[1]
    for j0 in range(0, d_ff, FF_CHUNK):
        cs = slice(j0, j0 + FF_CHUNK)
        g = jnp.dot(xb, wg_ref[:, cs], preferred_element_type=F32)
        u = jnp.dot(xb, wu_ref[:, cs], preferred_element_type=F32)
        act_ref[:, cs] = (g * _sigmoid(g) * u).astype(BF16)
    f = jnp.dot(act_ref[...], wd_ref[...], preferred_element_type=F32)
    y = _layer_norm(alpha * x_ref[...] + f, lg_ref[...], lb_ref[...])
    y_ref[...] = y
    yb_ref[...] = y.astype(BF16)


def _ffn(x, xb, w_gate, w_up, w_down, ln_g, ln_b, alpha, tm):
    t, d = x.shape
    d_ff = w_gate.shape[1]
    rows = pl.BlockSpec((tm, d), lambda i: (i, 0))
    return pl.pallas_call(
        functools.partial(_ffn_kernel, alpha=alpha),
        grid=(t // tm,),
        in_specs=[rows, rows, _resident(w_gate.shape), _resident(w_up.shape), _resident(w_down.shape),
                  _resident(ln_g.shape), _resident(ln_b.shape)],
        out_specs=[rows, rows],
        out_shape=[jax.ShapeDtypeStruct((t, d), F32), jax.ShapeDtypeStruct((t, d), BF16)],
        scratch_shapes=[pltpu.VMEM((tm, d_ff), BF16)],
        compiler_params=pltpu.CompilerParams(dimension_semantics=("parallel",), vmem_limit_bytes=VMEM_LIMIT),
        name="swiglu_ln",
    )(x, xb, w_gate, w_up, w_down, ln_g, ln_b)


def _rope_tables(seq):
    pos = jnp.arange(seq, dtype=F32)

    def tables(n_rot, base, width):
        half = n_rot // 2
        inv = base ** (-jnp.arange(half, dtype=F32) * 2.0 / n_rot)
        ang = pos[:, None] * inv[None, :]
        cos = jnp.concatenate([jnp.cos(ang), jnp.cos(ang), jnp.ones((seq, width - n_rot), F32)], axis=1)
        sin = jnp.concatenate([-jnp.sin(ang), jnp.sin(ang), jnp.zeros((seq, width - n_rot), F32)], axis=1)
        return cos, sin

    rc, rs = tables(RET_DK, RET_ROPE_BASE, RET_DK)
    dc, ds = tables(ROPE_DIMS, ROPE_THETA, DIL_HD)
    dil_tabs = []
    for _, dil in DIL_GROUPS:
        for tab in (dc, ds):
            dil_tabs.append(tab.reshape(seq // dil, dil, DIL_HD).transpose(1, 0, 2))
    return (jnp.tile(rc, (1, N_HEADS)), jnp.tile(rs, (1, N_HEADS))), dil_tabs


def _split_w_in(w_in):
    d = w_in.shape[0]
    widths = (256, 256, 512, 512, GLA_RANK, 256, 256, 512, 512, 1536, 1536, 1536, 3 * d)
    offs = [0]
    for w in widths:
        offs.append(offs[-1] + w)
    seg = [w_in[:, offs[i]:offs[i + 1]] for i in range(len(widths))]
    gq, gk, gv, gr, ga, rq, rk, rv, rg, dq, dk, dv, gate = seg
    ga = jnp.pad(ga, ((0, 0), (0, GA_PAD - GLA_RANK)))
    w1 = jnp.concatenate([gq, gk, gv, gr, rq, rk, rv, rg, ga], axis=1)
    dil = []
    for g in range(len(DIL_GROUPS)):
        gs = slice(g * BRANCH_W, (g + 1) * BRANCH_W)
        dil += [dq[:, gs], dk[:, gs], dv[:, gs]]
    w2 = jnp.concatenate(dil, axis=1)
    return w1.astype(BF16), w2.astype(BF16), gate.astype(BF16)


def kernel(x, w_in, w_gla_a2, b_gla_a, gla_norm_g, ret_norm_g, w_branch, b_gate, w_out, ln1_g, ln1_b,
           w_ffn_gate, w_ffn_up, w_ffn_down, ln2_g, ln2_b):
    batch, seq, d = x.shape
    depth = w_in.shape[0]
    alpha = (2 * depth) ** 0.25
    t = batch * seq
    ret_tabs, dil_tabs = _rope_tables(seq)
    xf = x.reshape(t, d)
    xb = xf.astype(BF16)
    for l in range(depth):
        w1, w2, w_gate = _split_w_in(w_in[l])
        w_a2 = jnp.pad(w_gla_a2[l], ((0, GA_PAD - GLA_RANK), (0, 0))).astype(BF16)
        p1, *p_dil = _proj(xf, xb, w1, w2, ret_tabs, dil_tabs, batch, seq, ROW_TILE)
        oa = _gla(p1, w_a2, b_gla_a[l][None, :], gla_norm_g[l][None, :], batch, seq)
        ob = _ret(p1, ret_norm_g[l][None, :], batch, seq)
        o_dil, l_dil = [], []
        for pg, (_, dilation) in zip(p_dil, DIL_GROUPS):
            o, lse = _dil(pg, dilation, batch, seq)
            o_dil.append(o)
            l_dil.append(lse)
        xf, xb = _merge(xf, xb, oa, ob, o_dil, l_dil, w_gate, b_gate[l][None, :], w_branch[l].astype(BF16),
                        w_out[l].astype(BF16), ln1_g[l][None, :], ln1_b[l][None, :], alpha, ROW_TILE)
        xf, xb = _ffn(xf, xb, w_ffn_gate[l].astype(BF16), w_ffn_up[l].astype(BF16), w_ffn_down[l].astype(BF16),
                      ln2_g[l][None, :], ln2_b[l][None, :], alpha, ROW_TILE)
    return xf.reshape(batch, seq, d)
```

```python
import functools
import math

import jax
import jax.numpy as jnp
from jax import lax
from jax.experimental import pallas as pl
from jax.experimental.pallas import tpu as pltpu

F32 = jnp.float32
BF16 = jnp.bfloat16

LANES = 128
N_HEADS = 4
GLA_DK = 64
GLA_DV = 128
GLA_RANK = 16
GLA_TAU = 16.0
GLA_CHUNK = 64
GLA_GROUP = 256
RET_DK = 64
RET_DV = 128
RET_CHUNK = 128
RET_GROUP = 256
RET_ROPE_BASE = 10000.0
DIL_HD = 128
DIL_GROUPS = ((128, 1), (512, 4), (2048, 16))
DIL_SPAN = 128
DIL_BLOCK = DIL_SPAN * max(d for _, d in DIL_GROUPS)
ROPE_THETA = 500000.0
ROPE_DIMS = 32
BRANCH_W = 512
GA_PAD = 256
GA_COL = 3072
RET_QK_COL = 1536
MASK_VALUE = -1e30

VMEM_LIMIT = 56 * 1024 * 1024
SEQ_BLOCK = 512
ROW_TILE = 512
FF_CHUNK = 256
DIL_UNROLL = 4


def _sigmoid(x):
    return 1.0 / (1.0 + jnp.exp(-x))


def _nt_dot(a, b):
    return lax.dot_general(a, b, (((1,), (1,)), ((), ())), preferred_element_type=F32)


def _tn_dot(a, b):
    return lax.dot_general(a, b, (((0,), (0,)), ((), ())), preferred_element_type=F32)


def _layer(stacked, layer):
    rest = stacked.shape[1:]
    return pl.BlockSpec((None,) + rest, lambda *_: (layer,) + (0,) * len(rest), pipeline_mode=pl.Buffered(1))


def _rotary(x, cos, sin, head_dim, half):
    width = x.shape[1]
    low = (lax.broadcasted_iota(jnp.int32, x.shape, 1) % head_dim) < half
    partner = jnp.where(low, pltpu.roll(x, width - half, 1), pltpu.roll(x, half, 1))
    return x * cos + partner * sin


def _proj_kernel(*refs):
    n_groups = len(DIL_GROUPS)
    x_ref, w1_ref, w2_ref, rc_ref, rs_ref = refs[:5]
    tabs = refs[5:5 + 2 * n_groups]
    p1_ref = refs[5 + 2 * n_groups]
    g_refs = refs[6 + 2 * n_groups:6 + 3 * n_groups]
    slab_ref, xp_ref = refs[-2:]
    tm, k = x_ref.shape
    tn = BRANCH_W
    x = x_ref[...].astype(BF16)
    n1 = p1_ref.shape[1]
    for j0 in range(0, n1, tn):
        j1 = min(j0 + tn, n1)
        res = jnp.dot(x, w1_ref[:, j0:j1], preferred_element_type=F32)
        if j0 == RET_QK_COL:
            cos = jnp.concatenate([rc_ref[...]] * 2, axis=1)
            sin = jnp.concatenate([rs_ref[...]] * 2, axis=1)
            res = _rotary(res, cos, sin, RET_DK, RET_DK // 2)
        p1_ref[:, j0:j1] = res.astype(BF16)
    for j in range(k // LANES):
        slab_ref[j] = x_ref[:, j * LANES:(j + 1) * LANES]
    for g, (_, dil) in enumerate(DIL_GROUPS):
        rows = tm // dil
        if dil == 1:
            lhs = x
        else:
            for c in range(dil):
                for j in range(k // LANES):
                    xp_ref[c * rows:(c + 1) * rows, j * LANES:(j + 1) * LANES] = (
                        slab_ref[j, pl.ds(c, rows, stride=dil), :].astype(BF16))
            lhs = xp_ref[...]
        cos = jnp.concatenate([tabs[2 * g][...].reshape(tm, LANES)] * N_HEADS, axis=1)
        sin = jnp.concatenate([tabs[2 * g + 1][...].reshape(tm, LANES)] * N_HEADS, axis=1)
        for part in range(3):
            col = (3 * g + part) * tn
            res = jnp.dot(lhs, w2_ref[:, col:col + tn], preferred_element_type=F32)
            if part < 2:
                res = _rotary(res, cos, sin, DIL_HD, ROPE_DIMS // 2)
            g_refs[g][:, :, part * tn:(part + 1) * tn] = res.astype(BF16).reshape(dil, rows, tn)


def _proj(x, w1, w2, layer, ret_tabs, dil_tabs, batch, seq, tm):
    t, k = x.shape
    n1 = w1.shape[2]
    nblk = seq // tm

    def row(b, i):
        return (b * nblk + i, 0)

    tab_specs, out_specs, out_shapes = [], [], []
    for _, dil in DIL_GROUPS:
        tab_specs += [pl.BlockSpec((dil, tm // dil, LANES), lambda b, i: (0, i, 0))] * 2
        out_specs.append(pl.BlockSpec((None, dil, tm // dil, 3 * BRANCH_W), lambda b, i: (b, 0, i, 0)))
        out_shapes.append(jax.ShapeDtypeStruct((batch, dil, seq // dil, 3 * BRANCH_W), BF16))
    ret_spec = pl.BlockSpec((tm, N_HEADS * RET_DK), lambda b, i: (i, 0))
    return pl.pallas_call(
        _proj_kernel,
        grid=(batch, nblk),
        in_specs=[pl.BlockSpec((tm, k), row), _layer(w1, layer), _layer(w2, layer), ret_spec, ret_spec] + tab_specs,
        out_specs=[pl.BlockSpec((tm, n1), row)] + out_specs,
        out_shape=[jax.ShapeDtypeStruct((t, n1), BF16)] + out_shapes,
        scratch_shapes=[pltpu.VMEM((k // LANES, tm, LANES), F32), pltpu.VMEM((tm, k), BF16)],
        compiler_params=pltpu.CompilerParams(dimension_semantics=("parallel", "parallel"),
                                             vmem_limit_bytes=VMEM_LIMIT),
        name="in_proj",
    )(x, w1, w2, *ret_tabs, *dil_tabs)


def _gla_kernel(q_ref, k_ref, v_ref, r_ref, a_ref, wa_ref, ba_ref, g_ref, o_ref, st_ref, qs_ref, kd_ref, acc_ref):
    c = GLA_CHUNK
    grp = GLA_GROUP
    tb = q_ref.shape[0]
    width = N_HEADS * GLA_DK

    @pl.when(pl.program_id(1) == 0)
    def _():
        st_ref[...] = jnp.zeros_like(st_ref)

    z = jnp.dot(a_ref[...], wa_ref[...], preferred_element_type=F32) + ba_ref[...]
    log_a = (jnp.minimum(z, 0.0) - jnp.log1p(jnp.exp(-jnp.abs(z)))) * (1.0 / GLA_TAU)
    hi = log_a.astype(BF16)
    lo = (log_a - hi.astype(F32)).astype(BF16)
    hl = jnp.concatenate([hi, lo], axis=1)

    row = lax.broadcasted_iota(jnp.int32, (grp, grp), 0)
    col = lax.broadcasted_iota(jnp.int32, (grp, grp), 1)
    same_chunk = (row // c) == (col // c)
    causal = same_chunk & (row >= col)
    cum_and_total = jnp.concatenate([jnp.where(causal, 1.0, 0.0), jnp.where(same_chunk, 1.0, 0.0)],
                                    axis=0).astype(BF16)
    lane_head = lax.broadcasted_iota(jnp.int32, (grp, width), 1) // GLA_DK
    st_row_head = lax.broadcasted_iota(jnp.int32, st_ref.shape, 0) // GLA_DV
    st_col_head = lax.broadcasted_iota(jnp.int32, st_ref.shape, 1) // GLA_DK
    st_diag = st_row_head == st_col_head

    decays = []
    for g0 in range(0, tb, grp):
        rows = slice(g0, g0 + grp)
        r2 = jnp.dot(cum_and_total, hl[rows], preferred_element_type=F32)
        b = r2[:grp, :width] + r2[:grp, width:]
        b_tot = r2[grp:, :width] + r2[grp:, width:]
        q = q_ref[rows, :].astype(F32) * (GLA_DK ** -0.5)
        k = k_ref[rows, :].astype(F32)
        q_s = (q * jnp.exp(b)).astype(BF16)
        k_s = k * jnp.exp(-b)
        qs_ref[rows, :] = q_s
        kd_ref[rows, :] = (k * jnp.exp(b_tot - b)).astype(BF16)
        for n0 in range(0, grp, c):
            decays.append(jnp.exp(b_tot[n0:n0 + 1, :]))
        for h in range(N_HEADS):
            hs = slice(h * GLA_DV, (h + 1) * GLA_DV)
            k_h = jnp.where(lane_head == h, k_s, 0.0).astype(BF16)
            att = jnp.where(causal, _nt_dot(q_s, k_h), 0.0).astype(BF16)
            acc_ref[rows, hs] = jnp.dot(att, v_ref[rows, hs], preferred_element_type=F32)

    for n, decay in enumerate(decays):
        rows = slice(n * c, (n + 1) * c)
        st = st_ref[...]
        acc_ref[rows, :] += _nt_dot(qs_ref[rows, :], st.astype(BF16))
        kv_t = _tn_dot(v_ref[rows, :], kd_ref[rows, :])
        st_ref[...] = st * decay + jnp.where(st_diag, kv_t, 0.0)

    outs = []
    for h in range(N_HEADS):
        o = acc_ref[:, h * GLA_DV:(h + 1) * GLA_DV]
        outs.append(o * lax.rsqrt(jnp.mean(o * o, axis=-1, keepdims=True) + 1e-6))
    y = jnp.concatenate(outs, axis=-1) * g_ref[...]
    r = r_ref[...].astype(F32)
    o_ref[...] = (y * (r * _sigmoid(r))).astype(o_ref.dtype)


def _gla(p1, w_a2, b_a, norm_g, layer, batch, seq):
    t = batch * seq
    tb = SEQ_BLOCK
    nblk = seq // tb

    def at(width, col):
        return pl.BlockSpec((tb, width), lambda b, i: (b * nblk + i, col))

    return pl.pallas_call(
        _gla_kernel,
        grid=(batch, nblk),
        in_specs=[at(256, 0), at(256, 1), at(512, 1), at(512, 2), at(GA_PAD, GA_COL // GA_PAD),
                  _layer(w_a2, layer), _layer(b_a, layer), _layer(norm_g, layer)],
        out_specs=pl.BlockSpec((tb, BRANCH_W), lambda b, i: (b * nblk + i, 0)),
        out_shape=jax.ShapeDtypeStruct((t, BRANCH_W), BF16),
        scratch_shapes=[pltpu.VMEM((N_HEADS * GLA_DV, N_HEADS * GLA_DK), F32),
                        pltpu.VMEM((tb, N_HEADS * GLA_DK), BF16), pltpu.VMEM((tb, N_HEADS * GLA_DK), BF16),
                        pltpu.VMEM((tb, BRANCH_W), F32)],
        compiler_params=pltpu.CompilerParams(dimension_semantics=("parallel", "arbitrary"),
                                             vmem_limit_bytes=VMEM_LIMIT),
        name="gla",
    )(p1, p1, p1, p1, p1, w_a2, b_a, norm_g)


def _ret_kernel(q_ref, k_ref, v_ref, r_ref, g_ref, o_ref, st_ref, qd_ref, kd_ref, acc_ref):
    c = RET_CHUNK
    grp = RET_GROUP
    tb = q_ref.shape[0]
    width = N_HEADS * RET_DK

    @pl.when(pl.program_id(1) == 0)
    def _():
        st_ref[...] = jnp.zeros_like(st_ref)

    log_g = [math.log1p(-(2.0 ** (-5.0 - h))) for h in range(N_HEADS)]
    lane_head = lax.broadcasted_iota(jnp.int32, (grp, width), 1) // RET_DK
    lg_lane = jnp.zeros((grp, width), F32)
    for h in range(N_HEADS):
        lg_lane = jnp.where(lane_head == h, log_g[h], lg_lane)
    ridx = (lax.broadcasted_iota(jnp.int32, (grp, width), 0) % c).astype(F32)
    q_scale = jnp.exp((ridx + 1.0) * lg_lane)
    k_scale = jnp.exp((c - 1.0 - ridx) * lg_lane)
    chunk_decay = jnp.exp(c * lg_lane[0:1, :])
    row = lax.broadcasted_iota(jnp.int32, (grp, grp), 0)
    col = lax.broadcasted_iota(jnp.int32, (grp, grp), 1)
    causal = ((row // c) == (col // c)) & (row >= col)
    diff = (row - col).astype(F32)
    st_diag = (lax.broadcasted_iota(jnp.int32, st_ref.shape, 0) // RET_DV
               == lax.broadcasted_iota(jnp.int32, st_ref.shape, 1) // RET_DK)

    for g0 in range(0, tb, grp):
        rows = slice(g0, g0 + grp)
        q = q_ref[rows, :].astype(F32)
        k = k_ref[rows, :].astype(F32) * (RET_DK ** -0.5)
        qb = q.astype(BF16)
        qd_ref[rows, :] = (q * q_scale).astype(BF16)
        kd_ref[rows, :] = (k * k_scale).astype(BF16)
        for h in range(N_HEADS):
            hs = slice(h * RET_DV, (h + 1) * RET_DV)
            dmat = jnp.where(causal, jnp.exp(jnp.maximum(diff, 0.0) * log_g[h]), 0.0)
            k_h = jnp.where(lane_head == h, k, 0.0).astype(BF16)
            att = (_nt_dot(qb, k_h) * dmat).astype(BF16)
            acc_ref[rows, hs] = jnp.dot(att, v_ref[rows, hs], preferred_element_type=F32)

    for n in range(tb // c):
        rows = slice(n * c, (n + 1) * c)
        st = st_ref[...]
        acc_ref[rows, :] += _nt_dot(qd_ref[rows, :], st.astype(BF16))
        kv_t = _tn_dot(v_ref[rows, :], kd_ref[rows, :])
        st_ref[...] = st * chunk_decay + jnp.where(st_diag, kv_t, 0.0)

    outs = []
    for h in range(N_HEADS):
        o = acc_ref[:, h * RET_DV:(h + 1) * RET_DV]
        mu = jnp.mean(o, axis=-1, keepdims=True)
        d = o - mu
        outs.append(d * lax.rsqrt(jnp.mean(d * d, axis=-1, keepdims=True) + 1e-5))
    y = jnp.concatenate(outs, axis=-1) * g_ref[...]
    r = r_ref[...].astype(F32)
    o_ref[...] = (y * (r * _sigmoid(r))).astype(o_ref.dtype)


def _ret(p1, norm_g, layer, batch, seq):
    t = batch * seq
    tb = SEQ_BLOCK
    nblk = seq // tb

    def at(width, col):
        return pl.BlockSpec((tb, width), lambda b, i: (b * nblk + i, col))

    return pl.pallas_call(
        _ret_kernel,
        grid=(batch, nblk),
        in_specs=[at(256, 6), at(256, 7), at(512, 4), at(512, 5), _layer(norm_g, layer)],
        out_specs=pl.BlockSpec((tb, BRANCH_W), lambda b, i: (b * nblk + i, 0)),
        out_shape=jax.ShapeDtypeStruct((t, BRANCH_W), BF16),
        scratch_shapes=[pltpu.VMEM((N_HEADS * RET_DV, N_HEADS * RET_DK), F32),
                        pltpu.VMEM((tb, N_HEADS * RET_DK), BF16), pltpu.VMEM((tb, N_HEADS * RET_DK), BF16),
                        pltpu.VMEM((tb, BRANCH_W), F32)],
        compiler_params=pltpu.CompilerParams(dimension_semantics=("parallel", "arbitrary"),
                                             vmem_limit_bytes=VMEM_LIMIT),
        name="retention",
    )(p1, p1, p1, p1, norm_g)


def _dil_kernel(q_ref, k_ref, v_ref, kp_ref, vp_ref, o_ref, l_ref, o_scr, l_scr, *, dilation):
    n = DIL_SPAN
    nj = q_ref.shape[1] // n
    row = lax.broadcasted_iota(jnp.int32, (n, 2 * n), 0)
    col = lax.broadcasted_iota(jnp.int32, (n, 2 * n), 1)
    dist = row + n - col
    band = (dist >= 0) & (dist <= n)
    band_first = band & ((col >= n) | (pl.program_id(1) > 0))
    lane_head = lax.broadcasted_iota(jnp.int32, (n, LANES), 1) // (LANES // N_HEADS)
    scale = DIL_HD ** -0.5
    scale_log2 = scale * math.log2(math.e)

    def unit(c, j, mask, keys, values):
        start = j * (n * dilation) + c
        dst = pl.ds(start, n, stride=dilation) if dilation > 1 else pl.ds(start, n)
        q = q_ref[c, pl.ds(j * n, n), :]
        lse_tile = jnp.zeros((n, LANES), F32)
        for h in range(N_HEADS):
            hs = slice(h * DIL_HD, (h + 1) * DIL_HD)
            s = jnp.where(mask, _nt_dot(q[:, hs], keys(hs)), MASK_VALUE)
            m = jnp.max(s, axis=-1, keepdims=True)
            p = jnp.exp2((s - m) * scale_log2)
            den = jnp.sum(p, axis=-1, keepdims=True)
            o = jnp.dot(p.astype(BF16), values(hs), preferred_element_type=F32) / den
            o_scr[h, dst, :] = o
            lse_tile = jnp.where(lane_head == h, m * scale + jnp.log(den), lse_tile)
        l_scr[dst, :] = lse_tile

    def first_unit(c, carry):
        unit(c, 0, band_first,
             lambda hs: jnp.concatenate([kp_ref[c, :, hs], k_ref[c, pl.ds(0, n), hs]], axis=0),
             lambda hs: jnp.concatenate([vp_ref[c, :, hs], v_ref[c, pl.ds(0, n), hs]], axis=0))
        return carry

    def later_unit(u, carry):
        c = u // (nj - 1)
        j = 1 + u % (nj - 1)
        r0 = pl.multiple_of((j - 1) * n, n)
        unit(c, j, band,
             lambda hs: k_ref[c, pl.ds(r0, 2 * n), hs],
             lambda hs: v_ref[c, pl.ds(r0, 2 * n), hs])
        return carry

    lax.fori_loop(0, dilation, first_unit, 0, unroll=min(dilation, DIL_UNROLL))
    if nj > 1:
        lax.fori_loop(0, dilation * (nj - 1), later_unit, 0, unroll=DIL_UNROLL)
    for h in range(N_HEADS):
        o_ref[:, h * DIL_HD:(h + 1) * DIL_HD] = o_scr[h].astype(o_ref.dtype)
    l_ref[...] = l_scr[...]


def _dil(pg, dilation, batch, seq):
    t = batch * seq
    blk = DIL_BLOCK
    nblk = seq // blk
    rows = blk // dilation
    prev_per_blk = rows // DIL_SPAN

    def cur(part):
        return pl.BlockSpec((None, dilation, rows, BRANCH_W), lambda b, i: (b, 0, i, part))

    def prev(part):
        return pl.BlockSpec((None, dilation, DIL_SPAN, BRANCH_W),
                            lambda b, i: (b, 0, jnp.maximum(i * prev_per_blk - 1, 0), part))

    return pl.pallas_call(
        functools.partial(_dil_kernel, dilation=dilation),
        grid=(batch, nblk),
        in_specs=[cur(0), cur(1), cur(2), prev(1), prev(2)],
        out_specs=[pl.BlockSpec((blk, BRANCH_W), lambda b, i: (b * nblk + i, 0)),
                   pl.BlockSpec((blk, LANES), lambda b, i: (b * nblk + i, 0))],
        out_shape=[jax.ShapeDtypeStruct((t, BRANCH_W), BF16), jax.ShapeDtypeStruct((t, LANES), F32)],
        scratch_shapes=[pltpu.VMEM((N_HEADS, blk, DIL_HD), F32), pltpu.VMEM((blk, LANES), F32)],
        compiler_params=pltpu.CompilerParams(dimension_semantics=("parallel", "arbitrary"),
                                             vmem_limit_bytes=VMEM_LIMIT),
        name=f"dilated_attn_{dilation}",
    )(pg, pg, pg, pg, pg)


def _layer_norm(v, g, b):
    mu = jnp.mean(v, axis=-1, keepdims=True)
    d = v - mu
    return d * lax.rsqrt(jnp.mean(d * d, axis=-1, keepdims=True) + 1e-5) * g + b


def _merge_kernel(x_ref, oa_ref, ob_ref, o1_ref, o2_ref, o3_ref, l1_ref, l2_ref, l3_ref,
                  wg_ref, bg_ref, wb_ref, wo_ref, lg_ref, lb_ref, y_ref, *, alpha):
    d = x_ref.shape[1]
    l1, l2, l3 = l1_ref[...], l2_ref[...], l3_ref[...]
    m = jnp.maximum(jnp.maximum(l1, l2), l3)
    e1, e2, e3 = jnp.exp(l1 - m), jnp.exp(l2 - m), jnp.exp(l3 - m)
    tot = e1 + e2 + e3
    w1, w2, w3 = e1 / tot, e2 / tot, e3 / tot
    lanes_per_head = LANES // N_HEADS
    parts = []
    for h in range(N_HEADS):
        hs = slice(h * DIL_HD, (h + 1) * DIL_HD)
        ls = slice(h * lanes_per_head, h * lanes_per_head + 1)
        parts.append(w1[:, ls] * o1_ref[:, hs].astype(F32) + w2[:, ls] * o2_ref[:, hs].astype(F32)
                     + w3[:, ls] * o3_ref[:, hs].astype(F32))
    oc = jnp.concatenate(parts, axis=-1).astype(BF16)
    x = x_ref[...]
    xb = x.astype(BF16)
    merged = None
    for j, o in enumerate((oa_ref[...], ob_ref[...], oc)):
        gate = _sigmoid(jnp.dot(xb, wg_ref[:, j * d:(j + 1) * d], preferred_element_type=F32)
                        + bg_ref[:, j * d:(j + 1) * d])
        term = gate * jnp.dot(o, wb_ref[j], preferred_element_type=F32)
        merged = term if merged is None else merged + term
    hmix = jnp.dot(merged.astype(BF16), wo_ref[...], preferred_element_type=F32)
    y_ref[...] = _layer_norm(alpha * x + hmix, lg_ref[...], lb_ref[...])


def _merge(x, oa, ob, o_dil, l_dil, w_gate, b_gate, w_branch, w_out, ln_g, ln_b, layer, alpha, tm):
    t, d = x.shape

    def rows(width):
        return pl.BlockSpec((tm, width), lambda i: (i, 0))

    params = (w_gate, b_gate, w_branch, w_out, ln_g, ln_b)
    return pl.pallas_call(
        functools.partial(_merge_kernel, alpha=alpha),
        grid=(t // tm,),
        in_specs=[rows(d)] + [rows(BRANCH_W)] * 5 + [rows(LANES)] * 3 + [_layer(p, layer) for p in params],
        out_specs=rows(d),
        out_shape=jax.ShapeDtypeStruct((t, d), F32),
        compiler_params=pltpu.CompilerParams(dimension_semantics=("parallel",), vmem_limit_bytes=VMEM_LIMIT),
        name="merge_out_ln",
    )(x, oa, ob, *o_dil, *l_dil, *params)


def _ffn_kernel(x_ref, wg_ref, wu_ref, wd_ref, lg_ref, lb_ref, y_ref, act_ref, *, alpha):
    x = x_ref[...]
    xb = x.astype(BF16)
    d_ff = wg_ref.shape[1]
    for j0 in range(0, d_ff, FF_CHUNK):
        cs = slice(j0, j0 + FF_CHUNK)
        g = jnp.dot(xb, wg_ref[:, cs], preferred_element_type=F32)
        u = jnp.dot(xb, wu_ref[:, cs], preferred_element_type=F32)
        act_ref[:, cs] = (g * _sigmoid(g) * u).astype(BF16)
    f = jnp.dot(act_ref[...], wd_ref[...], preferred_element_type=F32)
    y_ref[...] = _layer_norm(alpha * x + f, lg_ref[...], lb_ref[...])


def _ffn(x, w_gate, w_up, w_down, ln_g, ln_b, layer, alpha, tm):
    t, d = x.shape
    d_ff = w_gate.shape[2]
    rows = pl.BlockSpec((tm, d), lambda i: (i, 0))
    params = (w_gate, w_up, w_down, ln_g, ln_b)
    return pl.pallas_call(
        functools.partial(_ffn_kernel, alpha=alpha),
        grid=(t // tm,),
        in_specs=[rows] + [_layer(p, layer) for p in params],
        out_specs=rows,
        out_shape=jax.ShapeDtypeStruct((t, d), F32),
        scratch_shapes=[pltpu.VMEM((tm, d_ff), BF16)],
        compiler_params=pltpu.CompilerParams(dimension_semantics=("parallel",), vmem_limit_bytes=VMEM_LIMIT),
        name="swiglu_ln",
    )(x, *params)


def _rope_tables(seq):
    pos = jnp.arange(seq, dtype=F32)

    def tables(n_rot, base, width):
        half = n_rot // 2
        inv = base ** (-jnp.arange(half, dtype=F32) * 2.0 / n_rot)
        ang = pos[:, None] * inv[None, :]
        cos = jnp.concatenate([jnp.cos(ang), jnp.cos(ang), jnp.ones((seq, width - n_rot), F32)], axis=1)
        sin = jnp.concatenate([-jnp.sin(ang), jnp.sin(ang), jnp.zeros((seq, width - n_rot), F32)], axis=1)
        return cos, sin

    rc, rs = tables(RET_DK, RET_ROPE_BASE, RET_DK)
    dc, ds = tables(ROPE_DIMS, ROPE_THETA, DIL_HD)
    dil_tabs = []
    for _, dil in DIL_GROUPS:
        for tab in (dc, ds):
            dil_tabs.append(tab.reshape(seq // dil, dil, DIL_HD).transpose(1, 0, 2))
    return (jnp.tile(rc, (1, N_HEADS)), jnp.tile(rs, (1, N_HEADS))), dil_tabs


def _split_w_in(w_in):
    d = w_in.shape[1]
    widths = (256, 256, 512, 512, GLA_RANK, 256, 256, 512, 512, 1536, 1536, 1536, 3 * d)
    offs = [0]
    for w in widths:
        offs.append(offs[-1] + w)
    seg = [w_in[:, :, offs[i]:offs[i + 1]] for i in range(len(widths))]
    gq, gk, gv, gr, ga, rq, rk, rv, rg, dq, dk, dv, gate = seg
    ga = jnp.pad(ga, ((0, 0), (0, 0), (0, GA_PAD - GLA_RANK)))
    w1 = jnp.concatenate([gq, gk, gv, gr, rq, rk, rv, rg, ga], axis=2)
    dil = []
    for g in range(len(DIL_GROUPS)):
        gs = slice(g * BRANCH_W, (g + 1) * BRANCH_W)
        dil += [dq[:, :, gs], dk[:, :, gs], dv[:, :, gs]]
    w2 = jnp.concatenate(dil, axis=2)
    return w1.astype(BF16), w2.astype(BF16), gate.astype(BF16)


def kernel(x, w_in, w_gla_a2, b_gla_a, gla_norm_g, ret_norm_g, w_branch, b_gate, w_out, ln1_g, ln1_b,
           w_ffn_gate, w_ffn_up, w_ffn_down, ln2_g, ln2_b):
    batch, seq, d = x.shape
    depth = w_in.shape[0]
    alpha = (2 * depth) ** 0.25
    ret_tabs, dil_tabs = _rope_tables(seq)
    w1, w2, w_gate = _split_w_in(w_in)
    w_a2 = jnp.pad(w_gla_a2, ((0, 0), (0, GA_PAD - GLA_RANK), (0, 0))).astype(BF16)
    w_branch, w_out = w_branch.astype(BF16), w_out.astype(BF16)
    w_ffn_gate, w_ffn_up, w_ffn_down = w_ffn_gate.astype(BF16), w_ffn_up.astype(BF16), w_ffn_down.astype(BF16)
    b_gla_a, gla_norm_g, ret_norm_g, b_gate, ln1_g, ln1_b, ln2_g, ln2_b = (
        p[:, None, :] for p in (b_gla_a, gla_norm_g, ret_norm_g, b_gate, ln1_g, ln1_b, ln2_g, ln2_b))
    xf = x.reshape(batch * seq, d)
    for l in range(depth):
        p1, *p_dil = _proj(xf, w1, w2, l, ret_tabs, dil_tabs, batch, seq, ROW_TILE)
        oa = _gla(p1, w_a2, b_gla_a, gla_norm_g, l, batch, seq)
        ob = _ret(p1, ret_norm_g, l, batch, seq)
        o_dil, l_dil = [], []
        for pg, (_, dilation) in zip(p_dil, DIL_GROUPS):
            o, lse = _dil(pg, dilation, batch, seq)
            o_dil.append(o)
            l_dil.append(lse)
        xf = _merge(xf, oa, ob, o_dil, l_dil, w_gate, b_gate, w_branch, w_out, ln1_g, ln1_b, l, alpha, ROW_TILE)
        xf = _ffn(xf, w_ffn_gate, w_ffn_up, w_ffn_down, ln2_g, ln2_b, l, alpha, ROW_TILE)
    return xf.reshape(batch, seq, d)
```

```python
import functools
import math

import jax
import jax.numpy as jnp
from jax import lax
from jax.experimental import pallas as pl
from jax.experimental.pallas import tpu as pltpu

F32 = jnp.float32
BF16 = jnp.bfloat16

LANES = 128
N_HEADS = 4
GLA_DK = 64
GLA_DV = 128
GLA_RANK = 16
GLA_TAU = 16.0
GLA_CHUNK = 64
GLA_GROUP = 256
RET_DK = 64
RET_DV = 128
RET_CHUNK = 128
RET_GROUP = 256
RET_ROPE_BASE = 10000.0
DIL_HD = 128
DIL_GROUPS = ((128, 1), (512, 4), (2048, 16))
DIL_SPAN = 128
DIL_BLOCK = DIL_SPAN * max(d for _, d in DIL_GROUPS)
ROPE_THETA = 500000.0
ROPE_DIMS = 32
BRANCH_W = 512
GA_PAD = 256
GA_COL = 3072
RET_QK_COL = 1536
MASK_VALUE = -1e30

VMEM_LIMIT = 56 * 1024 * 1024
SEQ_BLOCK = 512
ROW_TILE = 512
FF_CHUNK = 256
FF_ROW_TILE = 1024
FF_ROW_SPLIT = 4
MERGE_ROW_TILE = 1024
MERGE_ROW_SPLIT = 4
DIL_UNROLL = 4


def _sigmoid(x):
    return 1.0 / (1.0 + jnp.exp(-x))


def _nt_dot(a, b):
    return lax.dot_general(a, b, (((1,), (1,)), ((), ())), preferred_element_type=F32)


def _tn_dot(a, b):
    return lax.dot_general(a, b, (((0,), (0,)), ((), ())), preferred_element_type=F32)


def _layer(stacked, layer):
    rest = stacked.shape[1:]
    return pl.BlockSpec((None,) + rest, lambda *_: (layer,) + (0,) * len(rest), pipeline_mode=pl.Buffered(1))


def _rotary(x, cos, sin, head_dim, half):
    width = x.shape[1]
    low = (lax.broadcasted_iota(jnp.int32, x.shape, 1) % head_dim) < half
    partner = jnp.where(low, pltpu.roll(x, width - half, 1), pltpu.roll(x, half, 1))
    return x * cos + partner * sin


def _proj_kernel(*refs):
    n_groups = len(DIL_GROUPS)
    x_ref, w1_ref, w2_ref, rc_ref, rs_ref = refs[:5]
    tabs = refs[5:5 + 2 * n_groups]
    p1_ref = refs[5 + 2 * n_groups]
    g_refs = refs[6 + 2 * n_groups:6 + 3 * n_groups]
    slab_ref, xp_ref = refs[-2:]
    tm, k = x_ref.shape
    tn = BRANCH_W
    x = x_ref[...].astype(BF16)
    n1 = p1_ref.shape[1]
    for j0 in range(0, n1, tn):
        j1 = min(j0 + tn, n1)
        res = jnp.dot(x, w1_ref[:, j0:j1], preferred_element_type=F32)
        if j0 == RET_QK_COL:
            cos = jnp.concatenate([rc_ref[...]] * 2, axis=1)
            sin = jnp.concatenate([rs_ref[...]] * 2, axis=1)
            res = _rotary(res, cos, sin, RET_DK, RET_DK // 2)
        p1_ref[:, j0:j1] = res.astype(BF16)
    for j in range(k // LANES):
        slab_ref[j] = x_ref[:, j * LANES:(j + 1) * LANES]
    for g, (_, dil) in enumerate(DIL_GROUPS):
        rows = tm // dil
        if dil == 1:
            lhs = x
        else:
            for c in range(dil):
                for j in range(k // LANES):
                    xp_ref[c * rows:(c + 1) * rows, j * LANES:(j + 1) * LANES] = (
                        slab_ref[j, pl.ds(c, rows, stride=dil), :].astype(BF16))
            lhs = xp_ref[...]
        cos = jnp.concatenate([tabs[2 * g][...].reshape(tm, LANES)] * N_HEADS, axis=1)
        sin = jnp.concatenate([tabs[2 * g + 1][...].reshape(tm, LANES)] * N_HEADS, axis=1)
        for part in range(3):
            col = (3 * g + part) * tn
            res = jnp.dot(lhs, w2_ref[:, col:col + tn], preferred_element_type=F32)
            if part < 2:
                res = _rotary(res, cos, sin, DIL_HD, ROPE_DIMS // 2)
            g_refs[g][:, :, part * tn:(part + 1) * tn] = res.astype(BF16).reshape(dil, rows, tn)


def _proj(x, w1, w2, layer, ret_tabs, dil_tabs, batch, seq, tm):
    t, k = x.shape
    n1 = w1.shape[2]
    nblk = seq // tm

    def row(i, b):
        return (b * nblk + i, 0)

    tab_specs, out_specs, out_shapes = [], [], []
    for _, dil in DIL_GROUPS:
        tab_specs += [pl.BlockSpec((dil, tm // dil, LANES), lambda i, b: (0, i, 0))] * 2
        out_specs.append(pl.BlockSpec((None, dil, tm // dil, 3 * BRANCH_W), lambda i, b: (b, 0, i, 0)))
        out_shapes.append(jax.ShapeDtypeStruct((batch, dil, seq // dil, 3 * BRANCH_W), BF16))
    ret_spec = pl.BlockSpec((tm, N_HEADS * RET_DK), lambda i, b: (i, 0))
    return pl.pallas_call(
        _proj_kernel,
        grid=(nblk, batch),
        in_specs=[pl.BlockSpec((tm, k), row), _layer(w1, layer), _layer(w2, layer), ret_spec, ret_spec] + tab_specs,
        out_specs=[pl.BlockSpec((tm, n1), row)] + out_specs,
        out_shape=[jax.ShapeDtypeStruct((t, n1), BF16)] + out_shapes,
        scratch_shapes=[pltpu.VMEM((k // LANES, tm, LANES), F32), pltpu.VMEM((tm, k), BF16)],
        compiler_params=pltpu.CompilerParams(dimension_semantics=("parallel", "parallel"),
                                             vmem_limit_bytes=VMEM_LIMIT),
        name="in_proj",
    )(x, w1, w2, *ret_tabs, *dil_tabs)


def _gla_kernel(q_ref, k_ref, v_ref, r_ref, a_ref, wa_ref, ba_ref, g_ref, o_ref, st_ref, qs_ref, kd_ref, acc_ref):
    c = GLA_CHUNK
    grp = GLA_GROUP
    tb = q_ref.shape[0]
    width = N_HEADS * GLA_DK

    @pl.when(pl.program_id(1) == 0)
    def _():
        st_ref[...] = jnp.zeros_like(st_ref)

    z = jnp.dot(a_ref[...], wa_ref[...], preferred_element_type=F32) + ba_ref[...]
    log_a = (jnp.minimum(z, 0.0) - jnp.log1p(jnp.exp(-jnp.abs(z)))) * (1.0 / GLA_TAU)
    hi = log_a.astype(BF16)
    lo = (log_a - hi.astype(F32)).astype(BF16)
    hl = jnp.concatenate([hi, lo], axis=1)

    row = lax.broadcasted_iota(jnp.int32, (grp, grp), 0)
    col = lax.broadcasted_iota(jnp.int32, (grp, grp), 1)
    same_chunk = (row // c) == (col // c)
    causal = same_chunk & (row >= col)
    cum_and_total = jnp.concatenate([jnp.where(causal, 1.0, 0.0), jnp.where(same_chunk, 1.0, 0.0)],
                                    axis=0).astype(BF16)
    lane_head = lax.broadcasted_iota(jnp.int32, (grp, width), 1) // GLA_DK
    st_row_head = lax.broadcasted_iota(jnp.int32, st_ref.shape, 0) // GLA_DV
    st_col_head = lax.broadcasted_iota(jnp.int32, st_ref.shape, 1) // GLA_DK
    st_diag = st_row_head == st_col_head

    decays = []
    for g0 in range(0, tb, grp):
        rows = slice(g0, g0 + grp)
        r2 = jnp.dot(cum_and_total, hl[rows], preferred_element_type=F32)
        b = r2[:grp, :width] + r2[:grp, width:]
        b_tot = r2[grp:, :width] + r2[grp:, width:]
        q = q_ref[rows, :].astype(F32)
        k = k_ref[rows, :].astype(F32)
        q_s = (q * jnp.exp(b)).astype(BF16)
        k_s = k * jnp.exp(-b)
        qs_ref[rows, :] = q_s
        kd_ref[rows, :] = (k * jnp.exp(b_tot - b)).astype(BF16)
        for n0 in range(0, grp, c):
            decays.append(jnp.exp(b_tot[n0:n0 + 1, :]))
        for h in range(N_HEADS):
            hs = slice(h * GLA_DV, (h + 1) * GLA_DV)
            k_h = jnp.where(lane_head == h, k_s, 0.0).astype(BF16)
            att = jnp.where(causal, _nt_dot(q_s, k_h), 0.0).astype(BF16)
            acc_ref[rows, hs] = jnp.dot(att, v_ref[rows, hs], preferred_element_type=F32)

    for n, decay in enumerate(decays):
        rows = slice(n * c, (n + 1) * c)
        st = st_ref[...]
        acc_ref[rows, :] += _nt_dot(qs_ref[rows, :], st.astype(BF16))
        kv_t = _tn_dot(v_ref[rows, :], kd_ref[rows, :])
        st_ref[...] = st * decay + jnp.where(st_diag, kv_t, 0.0)

    outs = []
    for h in range(N_HEADS):
        o = acc_ref[:, h * GLA_DV:(h + 1) * GLA_DV]
        outs.append(o * lax.rsqrt(jnp.mean(o * o, axis=-1, keepdims=True) + 1e-6))
    y = jnp.concatenate(outs, axis=-1) * g_ref[...]
    r = r_ref[...].astype(F32)
    o_ref[...] = (y * (r * _sigmoid(r))).astype(o_ref.dtype)


def _gla(p1, w_a2, b_a, norm_g, layer, batch, seq):
    t = batch * seq
    tb = SEQ_BLOCK
    nblk = seq // tb

    def at(width, col):
        return pl.BlockSpec((tb, width), lambda b, i: (b * nblk + i, col))

    return pl.pallas_call(
        _gla_kernel,
        grid=(batch, nblk),
        in_specs=[at(256, 0), at(256, 1), at(512, 1), at(512, 2), at(GA_PAD, GA_COL // GA_PAD),
                  _layer(w_a2, layer), _layer(b_a, layer), _layer(norm_g, layer)],
        out_specs=pl.BlockSpec((tb, BRANCH_W), lambda b, i: (b * nblk + i, 0)),
        out_shape=jax.ShapeDtypeStruct((t, BRANCH_W), BF16),
        scratch_shapes=[pltpu.VMEM((N_HEADS * GLA_DV, N_HEADS * GLA_DK), F32),
                        pltpu.VMEM((tb, N_HEADS * GLA_DK), BF16), pltpu.VMEM((tb, N_HEADS * GLA_DK), BF16),
                        pltpu.VMEM((tb, BRANCH_W), F32)],
        compiler_params=pltpu.CompilerParams(dimension_semantics=("parallel", "arbitrary"),
                                             vmem_limit_bytes=VMEM_LIMIT),
        name="gla",
    )(p1, p1, p1, p1, p1, w_a2, b_a, norm_g)


def _ret_kernel(q_ref, k_ref, v_ref, r_ref, g_ref, o_ref, st_ref, qd_ref, kd_ref, acc_ref):
    c = RET_CHUNK
    grp = RET_GROUP
    tb = q_ref.shape[0]
    width = N_HEADS * RET_DK

    @pl.when(pl.program_id(1) == 0)
    def _():
        st_ref[...] = jnp.zeros_like(st_ref)

    log_g = [math.log1p(-(2.0 ** (-5.0 - h))) for h in range(N_HEADS)]
    lane_head = lax.broadcasted_iota(jnp.int32, (grp, width), 1) // RET_DK
    lg_lane = jnp.zeros((grp, width), F32)
    for h in range(N_HEADS):
        lg_lane = jnp.where(lane_head == h, log_g[h], lg_lane)
    ridx = (lax.broadcasted_iota(jnp.int32, (grp, width), 0) % c).astype(F32)
    q_scale = jnp.exp((ridx + 1.0) * lg_lane)
    k_scale = jnp.exp((c - 1.0 - ridx) * lg_lane)
    chunk_decay = jnp.exp(c * lg_lane[0:1, :])
    row = lax.broadcasted_iota(jnp.int32, (grp, grp), 0)
    col = lax.broadcasted_iota(jnp.int32, (grp, grp), 1)
    causal = ((row // c) == (col // c)) & (row >= col)
    diff = (row - col).astype(F32)
    st_diag = (lax.broadcasted_iota(jnp.int32, st_ref.shape, 0) // RET_DV
               == lax.broadcasted_iota(jnp.int32, st_ref.shape, 1) // RET_DK)

    for g0 in range(0, tb, grp):
        rows = slice(g0, g0 + grp)
        q = q_ref[rows, :].astype(F32)
        k = k_ref[rows, :].astype(F32)
        qb = q_ref[rows, :]
        qd_ref[rows, :] = (q * q_scale).astype(BF16)
        kd_ref[rows, :] = (k * k_scale).astype(BF16)
        for h in range(N_HEADS):
            hs = slice(h * RET_DV, (h + 1) * RET_DV)
            dmat = jnp.where(causal, jnp.exp(jnp.maximum(diff, 0.0) * log_g[h]), 0.0)
            k_h = jnp.where(lane_head == h, k, 0.0).astype(BF16)
            att = (_nt_dot(qb, k_h) * dmat).astype(BF16)
            acc_ref[rows, hs] = jnp.dot(att, v_ref[rows, hs], preferred_element_type=F32)

    for n in range(tb // c):
        rows = slice(n * c, (n + 1) * c)
        st = st_ref[...]
        acc_ref[rows, :] += _nt_dot(qd_ref[rows, :], st.astype(BF16))
        kv_t = _tn_dot(v_ref[rows, :], kd_ref[rows, :])
        st_ref[...] = st * chunk_decay + jnp.where(st_diag, kv_t, 0.0)

    outs = []
    for h in range(N_HEADS):
        o = acc_ref[:, h * RET_DV:(h + 1) * RET_DV]
        mu = jnp.mean(o, axis=-1, keepdims=True)
        d = o - mu
        outs.append(d * lax.rsqrt(jnp.mean(d * d, axis=-1, keepdims=True) + 1e-5))
    y = jnp.concatenate(outs, axis=-1) * g_ref[...]
    r = r_ref[...].astype(F32)
    o_ref[...] = (y * (r * _sigmoid(r))).astype(o_ref.dtype)


def _ret(p1, norm_g, layer, batch, seq):
    t = batch * seq
    tb = SEQ_BLOCK
    nblk = seq // tb

    def at(width, col):
        return pl.BlockSpec((tb, width), lambda b, i: (b * nblk + i, col))

    return pl.pallas_call(
        _ret_kernel,
        grid=(batch, nblk),
        in_specs=[at(256, 6), at(256, 7), at(512, 4), at(512, 5), _layer(norm_g, layer)],
        out_specs=pl.BlockSpec((tb, BRANCH_W), lambda b, i: (b * nblk + i, 0)),
        out_shape=jax.ShapeDtypeStruct((t, BRANCH_W), BF16),
        scratch_shapes=[pltpu.VMEM((N_HEADS * RET_DV, N_HEADS * RET_DK), F32),
                        pltpu.VMEM((tb, N_HEADS * RET_DK), BF16), pltpu.VMEM((tb, N_HEADS * RET_DK), BF16),
                        pltpu.VMEM((tb, BRANCH_W), F32)],
        compiler_params=pltpu.CompilerParams(dimension_semantics=("parallel", "arbitrary"),
                                             vmem_limit_bytes=VMEM_LIMIT),
        name="retention",
    )(p1, p1, p1, p1, norm_g)


def _dil_kernel(q_ref, k_ref, v_ref, kp_ref, vp_ref, o_ref, l_ref, o_scr, l_scr, *, dilation):
    n = DIL_SPAN
    nj = q_ref.shape[1] // n
    qry = lax.broadcasted_iota(jnp.int32, (n, 2 * n), 0)
    key = lax.broadcasted_iota(jnp.int32, (n, 2 * n), 1)
    dist = qry + n - key
    bias_t = (dist >= 0) & (dist <= n)
    bias_t_first = bias_t & ((key >= n) | (pl.program_id(1) > 0))
    ones = jnp.ones((2 * n, n), BF16)
    lane_head = lax.broadcasted_iota(jnp.int32, (n, LANES), 1) // (LANES // N_HEADS)

    def unit(c, j, bias, keys, values):
        start = j * (n * dilation) + c
        dst = pl.ds(start, n, stride=dilation) if dilation > 1 else pl.ds(start, n)
        q = q_ref[c, pl.ds(j * n, n), :]
        lse_tile = jnp.zeros((n, LANES), F32)
        for h in range(N_HEADS):
            hs = slice(h * DIL_HD, (h + 1) * DIL_HD)
            s = jnp.where(bias, _nt_dot(q[:, hs], keys(hs)), MASK_VALUE)
            m = jnp.max(s, axis=-1, keepdims=True)
            p = jnp.exp2(s - m)
            o_den = jnp.dot(p.astype(BF16), jnp.concatenate([values(hs), ones], axis=1),
                            preferred_element_type=F32)
            den = o_den[:, DIL_HD:]
            o_scr[h, dst, :] = o_den[:, :DIL_HD] / den
            lse_tile = jnp.where(lane_head == h, m * math.log(2.0) + jnp.log(den), lse_tile)
        l_scr[dst, :] = lse_tile

    def first_unit(c, carry):
        unit(c, 0, bias_t_first,
             lambda hs: jnp.concatenate([kp_ref[c, :, hs], k_ref[c, pl.ds(0, n), hs]], axis=0),
             lambda hs: jnp.concatenate([vp_ref[c, :, hs], v_ref[c, pl.ds(0, n), hs]], axis=0))
        return carry

    def later_unit(u, carry):
        c = u // (nj - 1)
        j = 1 + u % (nj - 1)
        r0 = pl.multiple_of((j - 1) * n, n)
        unit(c, j, bias_t,
             lambda hs: k_ref[c, pl.ds(r0, 2 * n), hs],
             lambda hs: v_ref[c, pl.ds(r0, 2 * n), hs])
        return carry

    lax.fori_loop(0, dilation, first_unit, 0, unroll=min(dilation, DIL_UNROLL))
    if nj > 1:
        lax.fori_loop(0, dilation * (nj - 1), later_unit, 0, unroll=DIL_UNROLL)
    for h in range(N_HEADS):
        o_ref[:, h * DIL_HD:(h + 1) * DIL_HD] = o_scr[h].astype(o_ref.dtype)
    l_ref[...] = l_scr[...]


def _dil(pg, dilation, batch, seq):
    t = batch * seq
    blk = DIL_BLOCK
    nblk = seq // blk
    rows = blk // dilation
    prev_per_blk = rows // DIL_SPAN

    def cur(part):
        return pl.BlockSpec((None, dilation, rows, BRANCH_W), lambda b, i: (b, 0, i, part))

    def prev(part):
        return pl.BlockSpec((None, dilation, DIL_SPAN, BRANCH_W),
                            lambda b, i: (b, 0, jnp.maximum(i * prev_per_blk - 1, 0), part))

    return pl.pallas_call(
        functools.partial(_dil_kernel, dilation=dilation),
        grid=(batch, nblk),
        in_specs=[cur(0), cur(1), cur(2), prev(1), prev(2)],
        out_specs=[pl.BlockSpec((blk, BRANCH_W), lambda b, i: (b * nblk + i, 0)),
                   pl.BlockSpec((blk, LANES), lambda b, i: (b * nblk + i, 0))],
        out_shape=[jax.ShapeDtypeStruct((t, BRANCH_W), BF16), jax.ShapeDtypeStruct((t, LANES), F32)],
        scratch_shapes=[pltpu.VMEM((N_HEADS, blk, DIL_HD), F32), pltpu.VMEM((blk, LANES), F32)],
        compiler_params=pltpu.CompilerParams(dimension_semantics=("parallel", "arbitrary"),
                                             vmem_limit_bytes=VMEM_LIMIT),
        name=f"dilated_attn_{dilation}",
    )(pg, pg, pg, pg, pg)


def _layer_norm(v, g, b):
    mu = jnp.mean(v, axis=-1, keepdims=True)
    d = v - mu
    return d * lax.rsqrt(jnp.mean(d * d, axis=-1, keepdims=True) + 1e-5) * g + b


def _merge_kernel(x_ref, oa_ref, ob_ref, o1_ref, o2_ref, o3_ref, l1_ref, l2_ref, l3_ref,
                  wg_ref, bg_ref, wb_ref, wo_ref, lg_ref, lb_ref, y_ref, *, alpha):
    tm, d = x_ref.shape
    lanes_per_head = LANES // N_HEADS
    for r0 in range(0, tm, tm // MERGE_ROW_SPLIT):
        rows = slice(r0, r0 + tm // MERGE_ROW_SPLIT)
        l1, l2, l3 = l1_ref[rows, :], l2_ref[rows, :], l3_ref[rows, :]
        m = jnp.maximum(jnp.maximum(l1, l2), l3)
        e1, e2, e3 = jnp.exp(l1 - m), jnp.exp(l2 - m), jnp.exp(l3 - m)
        tot = e1 + e2 + e3
        w1, w2, w3 = e1 / tot, e2 / tot, e3 / tot
        parts = []
        for h in range(N_HEADS):
            hs = slice(h * DIL_HD, (h + 1) * DIL_HD)
            ls = slice(h * lanes_per_head, h * lanes_per_head + 1)
            parts.append(w1[:, ls] * o1_ref[rows, hs].astype(F32) + w2[:, ls] * o2_ref[rows, hs].astype(F32)
                         + w3[:, ls] * o3_ref[rows, hs].astype(F32))
        oc = jnp.concatenate(parts, axis=-1).astype(BF16)
        x = x_ref[rows, :]
        xb = x.astype(BF16)
        merged = None
        for j, o in enumerate((oa_ref[rows, :], ob_ref[rows, :], oc)):
            gate = _sigmoid(jnp.dot(xb, wg_ref[:, j * d:(j + 1) * d], preferred_element_type=F32)
                            + bg_ref[:, j * d:(j + 1) * d])
            term = gate * jnp.dot(o, wb_ref[j], preferred_element_type=F32)
            merged = term if merged is None else merged + term
        hmix = jnp.dot(merged.astype(BF16), wo_ref[...], preferred_element_type=F32)
        y_ref[rows, :] = _layer_norm(alpha * x + hmix, lg_ref[...], lb_ref[...])


def _merge(x, oa, ob, o_dil, l_dil, w_gate, b_gate, w_branch, w_out, ln_g, ln_b, layer, alpha, tm):
    t, d = x.shape

    def rows(width):
        return pl.BlockSpec((tm, width), lambda i: (i, 0))

    params = (w_gate, b_gate, w_branch, w_out, ln_g, ln_b)
    return pl.pallas_call(
        functools.partial(_merge_kernel, alpha=alpha),
        grid=(t // tm,),
        in_specs=[rows(d)] + [rows(BRANCH_W)] * 5 + [rows(LANES)] * 3 + [_layer(p, layer) for p in params],
        out_specs=rows(d),
        out_shape=jax.ShapeDtypeStruct((t, d), F32),
        compiler_params=pltpu.CompilerParams(dimension_semantics=("parallel",), vmem_limit_bytes=VMEM_LIMIT),
        name="merge_out_ln",
    )(x, oa, ob, *o_dil, *l_dil, *params)


def _ffn_kernel(x_ref, wg_ref, wu_ref, wd_ref, lg_ref, lb_ref, y_ref, act_ref, *, alpha):
    d_ff = wg_ref.shape[1]
    tm = x_ref.shape[0]
    for r0 in range(0, tm, tm // FF_ROW_SPLIT):
        rows = slice(r0, r0 + tm // FF_ROW_SPLIT)
        x = x_ref[rows, :]
        xb = x.astype(BF16)
        for j0 in range(0, d_ff, FF_CHUNK):
            cs = slice(j0, min(j0 + FF_CHUNK, d_ff))
            g = jnp.dot(xb, wg_ref[:, cs], preferred_element_type=F32)
            u = jnp.dot(xb, wu_ref[:, cs], preferred_element_type=F32)
            act_ref[rows, cs] = (g * _sigmoid(g) * u).astype(BF16)
        f = jnp.dot(act_ref[rows, :], wd_ref[...], preferred_element_type=F32)
        y_ref[rows, :] = _layer_norm(alpha * x + f, lg_ref[...], lb_ref[...])


def _ffn(x, w_gate, w_up, w_down, ln_g, ln_b, layer, alpha, tm):
    t, d = x.shape
    d_ff = w_gate.shape[2]
    rows = pl.BlockSpec((tm, d), lambda i: (i, 0))
    params = (w_gate, w_up, w_down, ln_g, ln_b)
    return pl.pallas_call(
        functools.partial(_ffn_kernel, alpha=alpha),
        grid=(t // tm,),
        in_specs=[rows] + [_layer(p, layer) for p in params],
        out_specs=rows,
        out_shape=jax.ShapeDtypeStruct((t, d), F32),
        scratch_shapes=[pltpu.VMEM((tm, d_ff), BF16)],
        compiler_params=pltpu.CompilerParams(dimension_semantics=("parallel",), vmem_limit_bytes=VMEM_LIMIT),
        name="swiglu_ln",
    )(x, *params)


def _rope_tables(seq):
    pos = jnp.arange(seq, dtype=F32)

    def tables(n_rot, base, width):
        half = n_rot // 2
        inv = base ** (-jnp.arange(half, dtype=F32) * 2.0 / n_rot)
        ang = pos[:, None] * inv[None, :]
        cos = jnp.concatenate([jnp.cos(ang), jnp.cos(ang), jnp.ones((seq, width - n_rot), F32)], axis=1)
        sin = jnp.concatenate([-jnp.sin(ang), jnp.sin(ang), jnp.zeros((seq, width - n_rot), F32)], axis=1)
        return cos, sin

    rc, rs = tables(RET_DK, RET_ROPE_BASE, RET_DK)
    dc, ds = tables(ROPE_DIMS, ROPE_THETA, DIL_HD)
    dil_tabs = []
    for _, dil in DIL_GROUPS:
        for tab in (dc, ds):
            dil_tabs.append(tab.reshape(seq // dil, dil, DIL_HD).transpose(1, 0, 2))
    return (jnp.tile(rc, (1, N_HEADS)), jnp.tile(rs, (1, N_HEADS))), dil_tabs


def _split_w_in(w_in):
    d = w_in.shape[1]
    widths = (256, 256, 512, 512, GLA_RANK, 256, 256, 512, 512, 1536, 1536, 1536, 3 * d)
    offs = [0]
    for w in widths:
        offs.append(offs[-1] + w)
    seg = [w_in[:, :, offs[i]:offs[i + 1]] for i in range(len(widths))]
    gq, gk, gv, gr, ga, rq, rk, rv, rg, dq, dk, dv, gate = seg
    gq = gq * (GLA_DK ** -0.5)
    rk = rk * (RET_DK ** -0.5)
    dq = dq * (DIL_HD ** -0.5 * math.log2(math.e))
    ga = jnp.pad(ga, ((0, 0), (0, 0), (0, GA_PAD - GLA_RANK)))
    w1 = jnp.concatenate([gq, gk, gv, gr, rq, rk, rv, rg, ga], axis=2)
    dil = []
    for g in range(len(DIL_GROUPS)):
        gs = slice(g * BRANCH_W, (g + 1) * BRANCH_W)
        dil += [dq[:, :, gs], dk[:, :, gs], dv[:, :, gs]]
    w2 = jnp.concatenate(dil, axis=2)
    return w1.astype(BF16), w2.astype(BF16), gate.astype(BF16)


def kernel(x, w_in, w_gla_a2, b_gla_a, gla_norm_g, ret_norm_g, w_branch, b_gate, w_out, ln1_g, ln1_b,
           w_ffn_gate, w_ffn_up, w_ffn_down, ln2_g, ln2_b):
    batch, seq, d = x.shape
    depth = w_in.shape[0]
    alpha = (2 * depth) ** 0.25
    ret_tabs, dil_tabs = _rope_tables(seq)
    w1, w2, w_gate = _split_w_in(w_in)
    w_a2 = jnp.pad(w_gla_a2, ((0, 0), (0, GA_PAD - GLA_RANK), (0, 0))).astype(BF16)
    w_branch, w_out = w_branch.astype(BF16), w_out.astype(BF16)
    w_ffn_gate, w_ffn_up, w_ffn_down = w_ffn_gate.astype(BF16), w_ffn_up.astype(BF16), w_ffn_down.astype(BF16)
    b_gla_a, gla_norm_g, ret_norm_g, b_gate, ln1_g, ln1_b, ln2_g, ln2_b = (
        p[:, None, :] for p in (b_gla_a, gla_norm_g, ret_norm_g, b_gate, ln1_g, ln1_b, ln2_g, ln2_b))
    xf = x.reshape(batch * seq, d)
    for l in range(depth):
        p1, *p_dil = _proj(xf, w1, w2, l, ret_tabs, dil_tabs, batch, seq, ROW_TILE)
        oa = _gla(p1, w_a2, b_gla_a, gla_norm_g, l, batch, seq)
        ob = _ret(p1, ret_norm_g, l, batch, seq)
        o_dil, l_dil = [], []
        for pg, (_, dilation) in zip(p_dil, DIL_GROUPS):
            o, lse = _dil(pg, dilation, batch, seq)
            o_dil.append(o)
            l_dil.append(lse)
        xf = _merge(xf, oa, ob, o_dil, l_dil, w_gate, b_gate, w_branch, w_out, ln1_g, ln1_b, l, alpha,
                    MERGE_ROW_TILE)
        xf = _ffn(xf, w_ffn_gate, w_ffn_up, w_ffn_down, ln2_g, ln2_b, l, alpha, FF_ROW_TILE)
    return xf.reshape(batch, seq, d)
```

```python
import functools
import math

import jax
import jax.numpy as jnp
from jax import lax
from jax.experimental import pallas as pl
from jax.experimental.pallas import tpu as pltpu

F32 = jnp.float32
BF16 = jnp.bfloat16

LANES = 128
N_HEADS = 4
GLA_DK = 64
GLA_DV = 128
GLA_RANK = 16
GLA_TAU = 16.0
GLA_CHUNK = 64
GLA_GROUP = 256
RET_DK = 64
RET_DV = 128
RET_CHUNK = 128
RET_GROUP = 256
RET_ROPE_BASE = 10000.0
DIL_HD = 128
DIL_GROUPS = ((128, 1), (512, 4), (2048, 16))
DIL_SPAN = 128
DIL_BLOCK = DIL_SPAN * max(d for _, d in DIL_GROUPS)
ROPE_THETA = 500000.0
ROPE_DIMS = 32
BRANCH_W = 512
GA_PAD = 256
GA_COL = 3072
RET_QK_COL = 1536
W_MAIN_COL = 0
W_GATE_COL = 3072
W_DIL_COL = 6144
W_GA_COL = 10752
MASK_VALUE = -1e30

VMEM_LIMIT = 56 * 1024 * 1024
SEQ_BLOCK = 512
ROW_TILE = 512
FF_CHUNK = 256
FF_ROW_TILE = 1024
FF_ROW_SPLIT = 4
MERGE_ROW_TILE = 1024
MERGE_ROW_SPLIT = 4
DIL_UNROLL = 4


def _sigmoid(x):
    return 1.0 / (1.0 + jnp.exp(-x))


def _nt_dot(a, b):
    return lax.dot_general(a, b, (((1,), (1,)), ((), ())), preferred_element_type=F32)


def _tn_dot(a, b):
    return lax.dot_general(a, b, (((0,), (0,)), ((), ())), preferred_element_type=F32)


def _layer(stacked, layer):
    rest = stacked.shape[1:]
    return pl.BlockSpec((None,) + rest, lambda *_: (layer,) + (0,) * len(rest), pipeline_mode=pl.Buffered(1))


def _rotary(x, cos, sin, head_dim, half):
    width = x.shape[1]
    low = (lax.broadcasted_iota(jnp.int32, x.shape, 1) % head_dim) < half
    partner = jnp.where(low, pltpu.roll(x, width - half, 1), pltpu.roll(x, half, 1))
    return x * cos + partner * sin


def _proj_kernel(*refs):
    n_groups = len(DIL_GROUPS)
    x_ref, wm_ref, wq_ref, wk_ref, wv_ref, wa_ref, rc_ref, rs_ref = refs[:8]
    tabs = refs[8:8 + 2 * n_groups]
    p1_ref = refs[8 + 2 * n_groups]
    g_refs = refs[9 + 2 * n_groups:9 + 3 * n_groups]
    slab_ref, slab4_ref, xp4_ref, xp16_ref = refs[-4:]
    tm, k = x_ref.shape
    n_slabs = k // LANES
    tn = BRANCH_W
    x = x_ref[...].astype(BF16)
    for j0 in range(0, GA_COL, tn):
        res = jnp.dot(x, wm_ref[:, j0:j0 + tn], preferred_element_type=F32)
        if j0 == RET_QK_COL:
            cos = jnp.concatenate([rc_ref[...]] * 2, axis=1)
            sin = jnp.concatenate([rs_ref[...]] * 2, axis=1)
            res = _rotary(res, cos, sin, RET_DK, RET_DK // 2)
        p1_ref[:, j0:j0 + tn] = res.astype(BF16)
    p1_ref[:, GA_COL:] = jnp.dot(x, wa_ref[...], preferred_element_type=F32).astype(BF16)
    q4 = tm // 4
    q16 = tm // 16
    for j in range(n_slabs):
        slab_ref[j] = x_ref[:, j * LANES:(j + 1) * LANES]
    for c in range(4):
        for j in range(n_slabs):
            part = slab_ref[j, pl.ds(c, q4, stride=4), :]
            slab4_ref[j, c * q4:(c + 1) * q4, :] = part
            xp4_ref[c * q4:(c + 1) * q4, j * LANES:(j + 1) * LANES] = part.astype(BF16)
    for c0 in range(4):
        for c1 in range(4):
            c = 4 * c1 + c0
            for j in range(n_slabs):
                xp16_ref[c * q16:(c + 1) * q16, j * LANES:(j + 1) * LANES] = (
                    slab4_ref[j, pl.ds(c0 * q4 + c1, q16, stride=4), :].astype(BF16))
    lhs_of = {1: lambda: x, 4: lambda: xp4_ref[...], 16: lambda: xp16_ref[...]}
    for g, (_, dil) in enumerate(DIL_GROUPS):
        rows = tm // dil
        lhs = lhs_of[dil]()
        cos = jnp.concatenate([tabs[2 * g][...].reshape(tm, LANES)] * N_HEADS, axis=1)
        sin = jnp.concatenate([tabs[2 * g + 1][...].reshape(tm, LANES)] * N_HEADS, axis=1)
        for part, w_ref in enumerate((wq_ref, wk_ref, wv_ref)):
            res = jnp.dot(lhs, w_ref[:, g * tn:(g + 1) * tn], preferred_element_type=F32)
            if part < 2:
                res = _rotary(res, cos, sin, DIL_HD, ROPE_DIMS // 2)
            g_refs[g][:, :, part * tn:(part + 1) * tn] = res.astype(BF16).reshape(dil, rows, tn)


def _cols(w, layer, width, offset):
    assert offset % width == 0
    return pl.BlockSpec((None, w.shape[1], width), lambda *_: (layer, 0, offset // width),
                        pipeline_mode=pl.Buffered(1))


def _proj(x, w, layer, ret_tabs, dil_tabs, batch, seq, tm):
    assert tuple(d for _, d in DIL_GROUPS) == (1, 4, 16)
    t, k = x.shape
    n1 = GA_COL + GA_PAD
    nblk = seq // tm
    w_specs = [_cols(w, layer, GA_COL, W_MAIN_COL)]
    w_specs += [_cols(w, layer, 3 * BRANCH_W, W_DIL_COL + part * 3 * BRANCH_W) for part in range(3)]
    w_specs += [_cols(w, layer, GA_PAD, W_GA_COL)]

    def row(i, b):
        return (b * nblk + i, 0)

    tab_specs, out_specs, out_shapes = [], [], []
    for _, dil in DIL_GROUPS:
        tab_specs += [pl.BlockSpec((dil, tm // dil, LANES), lambda i, b: (0, i, 0))] * 2
        out_specs.append(pl.BlockSpec((None, dil, tm // dil, 3 * BRANCH_W), lambda i, b: (b, 0, i, 0)))
        out_shapes.append(jax.ShapeDtypeStruct((batch, dil, seq // dil, 3 * BRANCH_W), BF16))
    ret_spec = pl.BlockSpec((tm, N_HEADS * RET_DK), lambda i, b: (i, 0))
    return pl.pallas_call(
        _proj_kernel,
        grid=(nblk, batch),
        in_specs=[pl.BlockSpec((tm, k), row)] + w_specs + [ret_spec, ret_spec] + tab_specs,
        out_specs=[pl.BlockSpec((tm, n1), row)] + out_specs,
        out_shape=[jax.ShapeDtypeStruct((t, n1), BF16)] + out_shapes,
        scratch_shapes=[pltpu.VMEM((k // LANES, tm, LANES), F32), pltpu.VMEM((k // LANES, tm, LANES), F32),
                        pltpu.VMEM((tm, k), BF16), pltpu.VMEM((tm, k), BF16)],
        compiler_params=pltpu.CompilerParams(dimension_semantics=("parallel", "parallel"),
                                             vmem_limit_bytes=VMEM_LIMIT),
        name="in_proj",
    )(x, *([w] * 5), *ret_tabs, *dil_tabs)


def _gla_kernel(q_ref, k_ref, v_ref, r_ref, a_ref, wa_ref, ba_ref, g_ref, o_ref, st_ref, qs_ref, kd_ref, acc_ref):
    c = GLA_CHUNK
    grp = GLA_GROUP
    tb = q_ref.shape[0]
    width = N_HEADS * GLA_DK

    @pl.when(pl.program_id(1) == 0)
    def _():
        st_ref[...] = jnp.zeros_like(st_ref)

    z = jnp.dot(a_ref[...], wa_ref[...], preferred_element_type=F32) + ba_ref[...]
    log_a = (jnp.minimum(z, 0.0) - jnp.log1p(jnp.exp(-jnp.abs(z)))) * (1.0 / GLA_TAU)
    hi = log_a.astype(BF16)
    lo = (log_a - hi.astype(F32)).astype(BF16)
    hl = jnp.concatenate([hi, lo], axis=1)

    row = lax.broadcasted_iota(jnp.int32, (grp, grp), 0)
    col = lax.broadcasted_iota(jnp.int32, (grp, grp), 1)
    same_chunk = (row // c) == (col // c)
    causal = same_chunk & (row >= col)
    cum_and_total = jnp.concatenate([jnp.where(causal, 1.0, 0.0), jnp.where(same_chunk, 1.0, 0.0)],
                                    axis=0).astype(BF16)
    lane_head = lax.broadcasted_iota(jnp.int32, (grp, width), 1) // GLA_DK
    st_row_head = lax.broadcasted_iota(jnp.int32, st_ref.shape, 0) // GLA_DV
    st_col_head = lax.broadcasted_iota(jnp.int32, st_ref.shape, 1) // GLA_DK
    st_diag = st_row_head == st_col_head

    decays = []
    for g0 in range(0, tb, grp):
        rows = slice(g0, g0 + grp)
        r2 = jnp.dot(cum_and_total, hl[rows], preferred_element_type=F32)
        b = r2[:grp, :width] + r2[:grp, width:]
        b_tot = r2[grp:, :width] + r2[grp:, width:]
        q = q_ref[rows, :].astype(F32)
        k = k_ref[rows, :].astype(F32)
        q_s = (q * jnp.exp(b)).astype(BF16)
        k_s = k * jnp.exp(-b)
        qs_ref[rows, :] = q_s
        kd_ref[rows, :] = (k * jnp.exp(b_tot - b)).astype(BF16)
        for n0 in range(0, grp, c):
            decays.append(jnp.exp(b_tot[n0:n0 + 1, :]))
        for h in range(N_HEADS):
            hs = slice(h * GLA_DV, (h + 1) * GLA_DV)
            k_h = jnp.where(lane_head == h, k_s, 0.0).astype(BF16)
            att = jnp.where(causal, _nt_dot(q_s, k_h), 0.0).astype(BF16)
            acc_ref[rows, hs] = jnp.dot(att, v_ref[rows, hs], preferred_element_type=F32)

    for n, decay in enumerate(decays):
        rows = slice(n * c, (n + 1) * c)
        st = st_ref[...]
        acc_ref[rows, :] += _nt_dot(qs_ref[rows, :], st.astype(BF16))
        kv_t = _tn_dot(v_ref[rows, :], kd_ref[rows, :])
        st_ref[...] = st * decay + jnp.where(st_diag, kv_t, 0.0)

    outs = []
    for h in range(N_HEADS):
        o = acc_ref[:, h * GLA_DV:(h + 1) * GLA_DV]
        outs.append(o * lax.rsqrt(jnp.mean(o * o, axis=-1, keepdims=True) + 1e-6))
    y = jnp.concatenate(outs, axis=-1) * g_ref[...]
    r = r_ref[...].astype(F32)
    o_ref[...] = (y * (r * _sigmoid(r))).astype(o_ref.dtype)


def _gla(p1, w_a2, b_a, norm_g, layer, batch, seq):
    t = batch * seq
    tb = SEQ_BLOCK
    nblk = seq // tb

    def at(width, col):
        return pl.BlockSpec((tb, width), lambda b, i: (b * nblk + i, col))

    return pl.pallas_call(
        _gla_kernel,
        grid=(batch, nblk),
        in_specs=[at(256, 0), at(256, 1), at(512, 1), at(512, 2), at(GA_PAD, GA_COL // GA_PAD),
                  _layer(w_a2, layer), _layer(b_a, layer), _layer(norm_g, layer)],
        out_specs=pl.BlockSpec((tb, BRANCH_W), lambda b, i: (b * nblk + i, 0)),
        out_shape=jax.ShapeDtypeStruct((t, BRANCH_W), BF16),
        scratch_shapes=[pltpu.VMEM((N_HEADS * GLA_DV, N_HEADS * GLA_DK), F32),
                        pltpu.VMEM((tb, N_HEADS * GLA_DK), BF16), pltpu.VMEM((tb, N_HEADS * GLA_DK), BF16),
                        pltpu.VMEM((tb, BRANCH_W), F32)],
        compiler_params=pltpu.CompilerParams(dimension_semantics=("parallel", "arbitrary"),
                                             vmem_limit_bytes=VMEM_LIMIT),
        name="gla",
    )(p1, p1, p1, p1, p1, w_a2, b_a, norm_g)


def _ret_kernel(q_ref, k_ref, v_ref, r_ref, g_ref, o_ref, st_ref, qd_ref, kd_ref, acc_ref):
    c = RET_CHUNK
    grp = RET_GROUP
    tb = q_ref.shape[0]
    width = N_HEADS * RET_DK

    @pl.when(pl.program_id(1) == 0)
    def _():
        st_ref[...] = jnp.zeros_like(st_ref)

    log_g = [math.log1p(-(2.0 ** (-5.0 - h))) for h in range(N_HEADS)]
    lane_head = lax.broadcasted_iota(jnp.int32, (grp, width), 1) // RET_DK
    lg_lane = jnp.zeros((grp, width), F32)
    for h in range(N_HEADS):
        lg_lane = jnp.where(lane_head == h, log_g[h], lg_lane)
    ridx = (lax.broadcasted_iota(jnp.int32, (grp, width), 0) % c).astype(F32)
    q_scale = jnp.exp((ridx + 1.0) * lg_lane)
    k_scale = jnp.exp((c - 1.0 - ridx) * lg_lane)
    chunk_decay = jnp.exp(c * lg_lane[0:1, :])
    row = lax.broadcasted_iota(jnp.int32, (grp, grp), 0)
    col = lax.broadcasted_iota(jnp.int32, (grp, grp), 1)
    causal = ((row // c) == (col // c)) & (row >= col)
    diff = (row - col).astype(F32)
    st_diag = (lax.broadcasted_iota(jnp.int32, st_ref.shape, 0) // RET_DV
               == lax.broadcasted_iota(jnp.int32, st_ref.shape, 1) // RET_DK)

    for g0 in range(0, tb, grp):
        rows = slice(g0, g0 + grp)
        q = q_ref[rows, :].astype(F32)
        k = k_ref[rows, :].astype(F32)
        qb = q_ref[rows, :]
        qd_ref[rows, :] = (q * q_scale).astype(BF16)
        kd_ref[rows, :] = (k * k_scale).astype(BF16)
        for h in range(N_HEADS):
            hs = slice(h * RET_DV, (h + 1) * RET_DV)
            dmat = jnp.where(causal, jnp.exp(jnp.maximum(diff, 0.0) * log_g[h]), 0.0)
            k_h = jnp.where(lane_head == h, k, 0.0).astype(BF16)
            att = (_nt_dot(qb, k_h) * dmat).astype(BF16)
            acc_ref[rows, hs] = jnp.dot(att, v_ref[rows, hs], preferred_element_type=F32)

    for n in range(tb // c):
        rows = slice(n * c, (n + 1) * c)
        st = st_ref[...]
        acc_ref[rows, :] += _nt_dot(qd_ref[rows, :], st.astype(BF16))
        kv_t = _tn_dot(v_ref[rows, :], kd_ref[rows, :])
        st_ref[...] = st * chunk_decay + jnp.where(st_diag, kv_t, 0.0)

    outs = []
    for h in range(N_HEADS):
        o = acc_ref[:, h * RET_DV:(h + 1) * RET_DV]
        mu = jnp.mean(o, axis=-1, keepdims=True)
        d = o - mu
        outs.append(d * lax.rsqrt(jnp.mean(d * d, axis=-1, keepdims=True) + 1e-5))
    y = jnp.concatenate(outs, axis=-1) * g_ref[...]
    r = r_ref[...].astype(F32)
    o_ref[...] = (y * (r * _sigmoid(r))).astype(o_ref.dtype)


def _ret(p1, norm_g, layer, batch, seq):
    t = batch * seq
    tb = SEQ_BLOCK
    nblk = seq // tb

    def at(width, col):
        return pl.BlockSpec((tb, width), lambda b, i: (b * nblk + i, col))

    return pl.pallas_call(
        _ret_kernel,
        grid=(batch, nblk),
        in_specs=[at(256, 6), at(256, 7), at(512, 4), at(512, 5), _layer(norm_g, layer)],
        out_specs=pl.BlockSpec((tb, BRANCH_W), lambda b, i: (b * nblk + i, 0)),
        out_shape=jax.ShapeDtypeStruct((t, BRANCH_W), BF16),
        scratch_shapes=[pltpu.VMEM((N_HEADS * RET_DV, N_HEADS * RET_DK), F32),
                        pltpu.VMEM((tb, N_HEADS * RET_DK), BF16), pltpu.VMEM((tb, N_HEADS * RET_DK), BF16),
                        pltpu.VMEM((tb, BRANCH_W), F32)],
        compiler_params=pltpu.CompilerParams(dimension_semantics=("parallel", "arbitrary"),
                                             vmem_limit_bytes=VMEM_LIMIT),
        name="retention",
    )(p1, p1, p1, p1, norm_g)


def _dil_kernel(q_ref, k_ref, v_ref, kp_ref, vp_ref, o_ref, l_ref, o_scr, l_scr, *, dilation):
    n = DIL_SPAN
    nj = q_ref.shape[1] // n
    qry = lax.broadcasted_iota(jnp.int32, (n, 2 * n), 0)
    key = lax.broadcasted_iota(jnp.int32, (n, 2 * n), 1)
    dist = qry + n - key
    bias_t = (dist >= 0) & (dist <= n)
    bias_t_first = bias_t & ((key >= n) | (pl.program_id(1) > 0))
    ones = jnp.ones((2 * n, n), BF16)
    lane_head = lax.broadcasted_iota(jnp.int32, (n, LANES), 1) // (LANES // N_HEADS)

    def unit(c, j, bias, keys, values):
        start = j * (n * dilation) + c
        dst = pl.ds(start, n, stride=dilation) if dilation > 1 else pl.ds(start, n)
        q = q_ref[c, pl.ds(j * n, n), :]
        lse_tile = jnp.zeros((n, LANES), F32)
        for h in range(N_HEADS):
            hs = slice(h * DIL_HD, (h + 1) * DIL_HD)
            s = jnp.where(bias, _nt_dot(q[:, hs], keys(hs)), MASK_VALUE)
            m = jnp.max(s, axis=-1, keepdims=True)
            p = jnp.exp2(s - m)
            o_den = jnp.dot(p.astype(BF16), jnp.concatenate([values(hs), ones], axis=1),
                            preferred_element_type=F32)
            den = o_den[:, DIL_HD:]
            o_scr[h, dst, :] = o_den[:, :DIL_HD] / den
            lse_tile = jnp.where(lane_head == h, m * math.log(2.0) + jnp.log(den), lse_tile)
        l_scr[dst, :] = lse_tile

    def first_unit(c, carry):
        unit(c, 0, bias_t_first,
             lambda hs: jnp.concatenate([kp_ref[c, :, hs], k_ref[c, pl.ds(0, n), hs]], axis=0),
             lambda hs: jnp.concatenate([vp_ref[c, :, hs], v_ref[c, pl.ds(0, n), hs]], axis=0))
        return carry

    def later_unit(u, carry):
        c = u // (nj - 1)
        j = 1 + u % (nj - 1)
        r0 = pl.multiple_of((j - 1) * n, n)
        unit(c, j, bias_t,
             lambda hs: k_ref[c, pl.ds(r0, 2 * n), hs],
             lambda hs: v_ref[c, pl.ds(r0, 2 * n), hs])
        return carry

    lax.fori_loop(0, dilation, first_unit, 0, unroll=min(dilation, DIL_UNROLL))
    if nj > 1:
        lax.fori_loop(0, dilation * (nj - 1), later_unit, 0, unroll=DIL_UNROLL)
    for h in range(N_HEADS):
        o_ref[:, h * DIL_HD:(h + 1) * DIL_HD] = o_scr[h].astype(o_ref.dtype)
    l_ref[...] = l_scr[...]


def _dil(pg, dilation, batch, seq):
    t = batch * seq
    blk = DIL_BLOCK
    nblk = seq // blk
    rows = blk // dilation
    prev_per_blk = rows // DIL_SPAN

    def cur(part):
        return pl.BlockSpec((None, dilation, rows, BRANCH_W), lambda b, i: (b, 0, i, part))

    def prev(part):
        return pl.BlockSpec((None, dilation, DIL_SPAN, BRANCH_W),
                            lambda b, i: (b, 0, jnp.maximum(i * prev_per_blk - 1, 0), part))

    return pl.pallas_call(
        functools.partial(_dil_kernel, dilation=dilation),
        grid=(batch, nblk),
        in_specs=[cur(0), cur(1), cur(2), prev(1), prev(2)],
        out_specs=[pl.BlockSpec((blk, BRANCH_W), lambda b, i: (b * nblk + i, 0)),
                   pl.BlockSpec((blk, LANES), lambda b, i: (b * nblk + i, 0))],
        out_shape=[jax.ShapeDtypeStruct((t, BRANCH_W), BF16), jax.ShapeDtypeStruct((t, LANES), F32)],
        scratch_shapes=[pltpu.VMEM((N_HEADS, blk, DIL_HD), F32), pltpu.VMEM((blk, LANES), F32)],
        compiler_params=pltpu.CompilerParams(dimension_semantics=("parallel", "arbitrary"),
                                             vmem_limit_bytes=VMEM_LIMIT),
        name=f"dilated_attn_{dilation}",
    )(pg, pg, pg, pg, pg)


def _layer_norm(v, g, b):
    mu = jnp.mean(v, axis=-1, keepdims=True)
    d = v - mu
    return d * lax.rsqrt(jnp.mean(d * d, axis=-1, keepdims=True) + 1e-5) * g + b


def _merge_kernel(x_ref, oa_ref, ob_ref, o1_ref, o2_ref, o3_ref, l1_ref, l2_ref, l3_ref,
                  wg_ref, bg_ref, wb_ref, wo_ref, lg_ref, lb_ref, y_ref, *, alpha):
    tm, d = x_ref.shape
    lanes_per_head = LANES // N_HEADS
    for r0 in range(0, tm, tm // MERGE_ROW_SPLIT):
        rows = slice(r0, r0 + tm // MERGE_ROW_SPLIT)
        l1, l2, l3 = l1_ref[rows, :], l2_ref[rows, :], l3_ref[rows, :]
        m = jnp.maximum(jnp.maximum(l1, l2), l3)
        e1, e2, e3 = jnp.exp(l1 - m), jnp.exp(l2 - m), jnp.exp(l3 - m)
        tot = e1 + e2 + e3
        w1, w2, w3 = e1 / tot, e2 / tot, e3 / tot
        parts = []
        for h in range(N_HEADS):
            hs = slice(h * DIL_HD, (h + 1) * DIL_HD)
            ls = slice(h * lanes_per_head, h * lanes_per_head + 1)
            parts.append(w1[:, ls] * o1_ref[rows, hs].astype(F32) + w2[:, ls] * o2_ref[rows, hs].astype(F32)
                         + w3[:, ls] * o3_ref[rows, hs].astype(F32))
        oc = jnp.concatenate(parts, axis=-1).astype(BF16)
        x = x_ref[rows, :]
        xb = x.astype(BF16)
        merged = None
        for j, o in enumerate((oa_ref[rows, :], ob_ref[rows, :], oc)):
            gate = _sigmoid(jnp.dot(xb, wg_ref[:, j * d:(j + 1) * d], preferred_element_type=F32)
                            + bg_ref[:, j * d:(j + 1) * d])
            term = gate * jnp.dot(o, wb_ref[j], preferred_element_type=F32)
            merged = term if merged is None else merged + term
        hmix = jnp.dot(merged.astype(BF16), wo_ref[...], preferred_element_type=F32)
        y_ref[rows, :] = _layer_norm(alpha * x + hmix, lg_ref[...], lb_ref[...])


def _merge(x, oa, ob, o_dil, l_dil, w_packed, b_gate, w_branch, w_out, ln_g, ln_b, layer, alpha, tm):
    t, d = x.shape

    def rows(width):
        return pl.BlockSpec((tm, width), lambda i: (i, 0))

    params = (b_gate, w_branch, w_out, ln_g, ln_b)
    return pl.pallas_call(
        functools.partial(_merge_kernel, alpha=alpha),
        grid=(t // tm,),
        in_specs=[rows(d)] + [rows(BRANCH_W)] * 5 + [rows(LANES)] * 3
                 + [_cols(w_packed, layer, 3 * d, W_GATE_COL)] + [_layer(p, layer) for p in params],
        out_specs=rows(d),
        out_shape=jax.ShapeDtypeStruct((t, d), F32),
        compiler_params=pltpu.CompilerParams(dimension_semantics=("parallel",), vmem_limit_bytes=VMEM_LIMIT),
        name="merge_out_ln",
    )(x, oa, ob, *o_dil, *l_dil, w_packed, *params)


def _ffn_kernel(x_ref, wg_ref, wu_ref, wd_ref, lg_ref, lb_ref, y_ref, act_ref, *, alpha):
    d_ff = wg_ref.shape[1]
    tm = x_ref.shape[0]
    for r0 in range(0, tm, tm // FF_ROW_SPLIT):
        rows = slice(r0, r0 + tm // FF_ROW_SPLIT)
        x = x_ref[rows, :]
        xb = x.astype(BF16)
        for j0 in range(0, d_ff, FF_CHUNK):
            cs = slice(j0, min(j0 + FF_CHUNK, d_ff))
            g = jnp.dot(xb, wg_ref[:, cs], preferred_element_type=F32)
            u = jnp.dot(xb, wu_ref[:, cs], preferred_element_type=F32)
            act_ref[rows, cs] = (g * _sigmoid(g) * u).astype(BF16)
        f = jnp.dot(act_ref[rows, :], wd_ref[...], preferred_element_type=F32)
        y_ref[rows, :] = _layer_norm(alpha * x + f, lg_ref[...], lb_ref[...])


def _ffn(x, w_gate, w_up, w_down, ln_g, ln_b, layer, alpha, tm):
    t, d = x.shape
    d_ff = w_gate.shape[2]
    rows = pl.BlockSpec((tm, d), lambda i: (i, 0))
    params = (w_gate, w_up, w_down, ln_g, ln_b)
    return pl.pallas_call(
        functools.partial(_ffn_kernel, alpha=alpha),
        grid=(t // tm,),
        in_specs=[rows] + [_layer(p, layer) for p in params],
        out_specs=rows,
        out_shape=jax.ShapeDtypeStruct((t, d), F32),
        scratch_shapes=[pltpu.VMEM((tm, d_ff), BF16)],
        compiler_params=pltpu.CompilerParams(dimension_semantics=("parallel",), vmem_limit_bytes=VMEM_LIMIT),
        name="swiglu_ln",
    )(x, *params)


def _rope_tables(seq):
    pos = jnp.arange(seq, dtype=F32)

    def tables(n_rot, base, width):
        half = n_rot // 2
        inv = base ** (-jnp.arange(half, dtype=F32) * 2.0 / n_rot)
        ang = pos[:, None] * inv[None, :]
        cos = jnp.concatenate([jnp.cos(ang), jnp.cos(ang), jnp.ones((seq, width - n_rot), F32)], axis=1)
        sin = jnp.concatenate([-jnp.sin(ang), jnp.sin(ang), jnp.zeros((seq, width - n_rot), F32)], axis=1)
        return cos, sin

    rc, rs = tables(RET_DK, RET_ROPE_BASE, RET_DK)
    dc, ds = tables(ROPE_DIMS, ROPE_THETA, DIL_HD)
    dil_tabs = []
    for _, dil in DIL_GROUPS:
        for tab in (dc, ds):
            dil_tabs.append(tab.reshape(seq // dil, dil, DIL_HD).transpose(1, 0, 2))
    return (jnp.tile(rc, (1, N_HEADS)), jnp.tile(rs, (1, N_HEADS))), dil_tabs


def _pack_w_in(w_in):
    d = w_in.shape[1]
    widths = (256, 256, 512, 512, GLA_RANK, 256, 256, 512, 512, 1536, 1536, 1536, 3 * d)
    scales = {0: GLA_DK ** -0.5, 6: RET_DK ** -0.5, 9: DIL_HD ** -0.5 * math.log2(math.e)}
    scale = jnp.concatenate([jnp.full((w,), scales.get(i, 1.0), F32) for i, w in enumerate(widths)])
    ga0 = sum(widths[:4])
    dil0 = sum(widths[:9])
    gate0 = sum(widths[:12])
    w = w_in * scale
    packed = jnp.concatenate(
        [w[:, :, :ga0], w[:, :, ga0 + GLA_RANK:dil0], w[:, :, gate0:], w[:, :, dil0:gate0],
         jnp.pad(w[:, :, ga0:ga0 + GLA_RANK], ((0, 0), (0, 0), (0, GA_PAD - GLA_RANK)))], axis=2)
    assert packed.shape[2] == W_GA_COL + GA_PAD
    return packed.astype(BF16)


def kernel(x, w_in, w_gla_a2, b_gla_a, gla_norm_g, ret_norm_g, w_branch, b_gate, w_out, ln1_g, ln1_b,
           w_ffn_gate, w_ffn_up, w_ffn_down, ln2_g, ln2_b):
    batch, seq, d = x.shape
    depth = w_in.shape[0]
    alpha = (2 * depth) ** 0.25
    ret_tabs, dil_tabs = _rope_tables(seq)
    w_packed = _pack_w_in(w_in)
    w_a2 = jnp.pad(w_gla_a2, ((0, 0), (0, GA_PAD - GLA_RANK), (0, 0))).astype(BF16)
    w_branch, w_out = w_branch.astype(BF16), w_out.astype(BF16)
    w_ffn_gate, w_ffn_up, w_ffn_down = w_ffn_gate.astype(BF16), w_ffn_up.astype(BF16), w_ffn_down.astype(BF16)
    b_gla_a, gla_norm_g, ret_norm_g, b_gate, ln1_g, ln1_b, ln2_g, ln2_b = (
        p[:, None, :] for p in (b_gla_a, gla_norm_g, ret_norm_g, b_gate, ln1_g, ln1_b, ln2_g, ln2_b))
    xf = x.reshape(batch * seq, d)
    for l in range(depth):
        p1, *p_dil = _proj(xf, w_packed, l, ret_tabs, dil_tabs, batch, seq, ROW_TILE)
        oa = _gla(p1, w_a2, b_gla_a, gla_norm_g, l, batch, seq)
        ob = _ret(p1, ret_norm_g, l, batch, seq)
        o_dil, l_dil = [], []
        for pg, (_, dilation) in zip(p_dil, DIL_GROUPS):
            o, lse = _dil(pg, dilation, batch, seq)
            o_dil.append(o)
            l_dil.append(lse)
        xf = _merge(xf, oa, ob, o_dil, l_dil, w_packed, b_gate, w_branch, w_out, ln1_g, ln1_b, l, alpha,
                    MERGE_ROW_TILE)
        xf = _ffn(xf, w_ffn_gate, w_ffn_up, w_ffn_down, ln2_g, ln2_b, l, alpha, FF_ROW_TILE)
    return xf.reshape(batch, seq, d)
```

```python
import functools
import math

import jax
import jax.numpy as jnp
from jax import lax
from jax.experimental import pallas as pl
from jax.experimental.pallas import tpu as pltpu

F32 = jnp.float32
BF16 = jnp.bfloat16

LANES = 128
N_HEADS = 4
GLA_DK = 64
GLA_DV = 128
GLA_RANK = 16
GLA_TAU = 16.0
GLA_CHUNK = 64
GLA_GROUP = 256
RET_DK = 64
RET_DV = 128
RET_CHUNK = 128
RET_GROUP = 256
RET_ROPE_BASE = 10000.0
DIL_HD = 128
DIL_GROUPS = ((128, 1), (512, 4), (2048, 16))
DIL_SPAN = 128
DIL_BLOCK = DIL_SPAN * max(d for _, d in DIL_GROUPS)
ROPE_THETA = 500000.0
ROPE_DIMS = 32
BRANCH_W = 512
GA_PAD = 256
GA_COL = 3072
RET_QK_COL = 1536
W_MAIN_COL = 0
W_GATE_COL = 3072
W_DIL_COL = 6144
W_GA_COL = 10752
MASK_VALUE = -1e30

VMEM_LIMIT = 56 * 1024 * 1024
SEQ_BLOCK = 512
ROW_TILE = 512
FF_CHUNK = 256
FF_ROW_TILE = 1024
FF_ROW_SPLIT = 4
MERGE_ROW_TILE = 1024
MERGE_ROW_SPLIT = 4
DIL_UNROLL = 4


def _sigmoid(x):
    return 1.0 / (1.0 + jnp.exp(-x))


def _nt_dot(a, b):
    return lax.dot_general(a, b, (((1,), (1,)), ((), ())), preferred_element_type=F32)


def _tn_dot(a, b):
    return lax.dot_general(a, b, (((0,), (0,)), ((), ())), preferred_element_type=F32)


def _layer(stacked, layer):
    rest = stacked.shape[1:]
    return pl.BlockSpec((None,) + rest, lambda *_: (layer,) + (0,) * len(rest), pipeline_mode=pl.Buffered(1))


def _rotary(x, cos, sin, head_dim, half):
    width = x.shape[1]
    low = (lax.broadcasted_iota(jnp.int32, x.shape, 1) % head_dim) < half
    partner = jnp.where(low, pltpu.roll(x, width - half, 1), pltpu.roll(x, half, 1))
    return x * cos + partner * sin


def _proj_kernel(*refs):
    n_groups = len(DIL_GROUPS)
    x_ref, wm_ref, wq_ref, wk_ref, wv_ref, wa_ref, rc_ref, rs_ref = refs[:8]
    tabs = refs[8:8 + 2 * n_groups]
    p1_ref = refs[8 + 2 * n_groups]
    g_refs = refs[9 + 2 * n_groups:9 + 3 * n_groups]
    slab_ref, slab4_ref, xp4_ref, xp16_ref = refs[-4:]
    tm, k = x_ref.shape
    n_slabs = k // LANES
    tn = BRANCH_W
    x = x_ref[...].astype(BF16)
    lane = lax.broadcasted_iota(jnp.int32, (1, tn), 1)
    for j0 in range(0, GA_COL, tn):
        res = jnp.dot(x, wm_ref[:, j0:j0 + tn], preferred_element_type=F32)
        if j0 == 0:
            res = res * jnp.where(lane < N_HEADS * GLA_DK, GLA_DK ** -0.5, 1.0)
        if j0 == RET_QK_COL:
            cos = jnp.concatenate([rc_ref[...]] * 2, axis=1)
            sin = jnp.concatenate([rs_ref[...]] * 2, axis=1)
            res = _rotary(res, cos, sin, RET_DK, RET_DK // 2)
            res = res * jnp.where(lane < N_HEADS * RET_DK, 1.0, RET_DK ** -0.5)
        p1_ref[:, j0:j0 + tn] = res.astype(BF16)
    p1_ref[:, GA_COL:] = jnp.dot(x, wa_ref[...], preferred_element_type=F32).astype(BF16)
    q4 = tm // 4
    q16 = tm // 16
    for j in range(n_slabs):
        slab_ref[j] = x_ref[:, j * LANES:(j + 1) * LANES]
    for c in range(4):
        for j in range(n_slabs):
            part = slab_ref[j, pl.ds(c, q4, stride=4), :]
            slab4_ref[j, c * q4:(c + 1) * q4, :] = part
            xp4_ref[c * q4:(c + 1) * q4, j * LANES:(j + 1) * LANES] = part.astype(BF16)
    for c0 in range(4):
        for c1 in range(4):
            c = 4 * c1 + c0
            for j in range(n_slabs):
                xp16_ref[c * q16:(c + 1) * q16, j * LANES:(j + 1) * LANES] = (
                    slab4_ref[j, pl.ds(c0 * q4 + c1, q16, stride=4), :].astype(BF16))
    lhs_of = {1: lambda: x, 4: lambda: xp4_ref[...], 16: lambda: xp16_ref[...]}
    for g, (_, dil) in enumerate(DIL_GROUPS):
        rows = tm // dil
        lhs = lhs_of[dil]()
        cos = jnp.concatenate([tabs[2 * g][...].reshape(tm, LANES)] * N_HEADS, axis=1)
        sin = jnp.concatenate([tabs[2 * g + 1][...].reshape(tm, LANES)] * N_HEADS, axis=1)
        for part, w_ref in enumerate((wq_ref, wk_ref, wv_ref)):
            res = jnp.dot(lhs, w_ref[:, g * tn:(g + 1) * tn], preferred_element_type=F32)
            if part < 2:
                res = _rotary(res, cos, sin, DIL_HD, ROPE_DIMS // 2)
            if part == 0:
                res = res * (DIL_HD ** -0.5 * math.log2(math.e))
            g_refs[g][:, :, part * tn:(part + 1) * tn] = res.astype(BF16).reshape(dil, rows, tn)


def _cols(w, layer, width, offset):
    assert offset % width == 0
    return pl.BlockSpec((None, w.shape[1], width), lambda *_: (layer, 0, offset // width),
                        pipeline_mode=pl.Buffered(1))


def _proj(x, w, layer, ret_tabs, dil_tabs, batch, seq, tm):
    assert tuple(d for _, d in DIL_GROUPS) == (1, 4, 16)
    t, k = x.shape
    n1 = GA_COL + GA_PAD
    nblk = seq // tm
    w_specs = [_cols(w, layer, GA_COL, W_MAIN_COL)]
    w_specs += [_cols(w, layer, 3 * BRANCH_W, W_DIL_COL + part * 3 * BRANCH_W) for part in range(3)]
    w_specs += [_cols(w, layer, GA_PAD, W_GA_COL)]

    def row(i, b):
        return (b * nblk + i, 0)

    tab_specs, out_specs, out_shapes = [], [], []
    for _, dil in DIL_GROUPS:
        tab_specs += [pl.BlockSpec((dil, tm // dil, LANES), lambda i, b: (0, i, 0))] * 2
        out_specs.append(pl.BlockSpec((None, dil, tm // dil, 3 * BRANCH_W), lambda i, b: (b, 0, i, 0)))
        out_shapes.append(jax.ShapeDtypeStruct((batch, dil, seq // dil, 3 * BRANCH_W), BF16))
    ret_spec = pl.BlockSpec((tm, N_HEADS * RET_DK), lambda i, b: (i, 0))
    return pl.pallas_call(
        _proj_kernel,
        grid=(nblk, batch),
        in_specs=[pl.BlockSpec((tm, k), row)] + w_specs + [ret_spec, ret_spec] + tab_specs,
        out_specs=[pl.BlockSpec((tm, n1), row)] + out_specs,
        out_shape=[jax.ShapeDtypeStruct((t, n1), BF16)] + out_shapes,
        scratch_shapes=[pltpu.VMEM((k // LANES, tm, LANES), F32), pltpu.VMEM((k // LANES, tm, LANES), F32),
                        pltpu.VMEM((tm, k), BF16), pltpu.VMEM((tm, k), BF16)],
        compiler_params=pltpu.CompilerParams(dimension_semantics=("parallel", "parallel"),
                                             vmem_limit_bytes=VMEM_LIMIT),
        name="in_proj",
    )(x, *([w] * 5), *ret_tabs, *dil_tabs)


def _gla_kernel(q_ref, k_ref, v_ref, r_ref, a_ref, wa_ref, ba_ref, g_ref, o_ref, st_ref, qs_ref, kd_ref, acc_ref):
    c = GLA_CHUNK
    grp = GLA_GROUP
    tb = q_ref.shape[0]
    width = N_HEADS * GLA_DK

    @pl.when(pl.program_id(1) == 0)
    def _():
        st_ref[...] = jnp.zeros_like(st_ref)

    z = jnp.dot(a_ref[...], wa_ref[...], preferred_element_type=F32) + ba_ref[...]
    log_a = (jnp.minimum(z, 0.0) - jnp.log1p(jnp.exp(-jnp.abs(z)))) * (1.0 / GLA_TAU)
    hi = log_a.astype(BF16)
    lo = (log_a - hi.astype(F32)).astype(BF16)
    hl = jnp.concatenate([hi, lo], axis=1)

    row = lax.broadcasted_iota(jnp.int32, (grp, grp), 0)
    col = lax.broadcasted_iota(jnp.int32, (grp, grp), 1)
    same_chunk = (row // c) == (col // c)
    causal = same_chunk & (row >= col)
    cum_and_total = jnp.concatenate([jnp.where(causal, 1.0, 0.0), jnp.where(same_chunk, 1.0, 0.0)],
                                    axis=0).astype(BF16)
    lane_head = lax.broadcasted_iota(jnp.int32, (grp, width), 1) // GLA_DK
    st_row_head = lax.broadcasted_iota(jnp.int32, st_ref.shape, 0) // GLA_DV
    st_col_head = lax.broadcasted_iota(jnp.int32, st_ref.shape, 1) // GLA_DK
    st_diag = st_row_head == st_col_head

    decays = []
    for g0 in range(0, tb, grp):
        rows = slice(g0, g0 + grp)
        r2 = jnp.dot(cum_and_total, hl[rows], preferred_element_type=F32)
        b = r2[:grp, :width] + r2[:grp, width:]
        b_tot = r2[grp:, :width] + r2[grp:, width:]
        q = q_ref[rows, :].astype(F32)
        k = k_ref[rows, :].astype(F32)
        q_s = (q * jnp.exp(b)).astype(BF16)
        k_s = k * jnp.exp(-b)
        qs_ref[rows, :] = q_s
        kd_ref[rows, :] = (k * jnp.exp(b_tot - b)).astype(BF16)
        for n0 in range(0, grp, c):
            decays.append(jnp.exp(b_tot[n0:n0 + 1, :]))
        for h in range(N_HEADS):
            hs = slice(h * GLA_DV, (h + 1) * GLA_DV)
            k_h = jnp.where(lane_head == h, k_s, 0.0).astype(BF16)
            att = jnp.where(causal, _nt_dot(q_s, k_h), 0.0).astype(BF16)
            acc_ref[rows, hs] = jnp.dot(att, v_ref[rows, hs], preferred_element_type=F32)

    for n, decay in enumerate(decays):
        rows = slice(n * c, (n + 1) * c)
        st = st_ref[...]
        acc_ref[rows, :] += _nt_dot(qs_ref[rows, :], st.astype(BF16))
        kv_t = _tn_dot(v_ref[rows, :], kd_ref[rows, :])
        st_ref[...] = st * decay + jnp.where(st_diag, kv_t, 0.0)

    outs = []
    for h in range(N_HEADS):
        o = acc_ref[:, h * GLA_DV:(h + 1) * GLA_DV]
        outs.append(o * lax.rsqrt(jnp.mean(o * o, axis=-1, keepdims=True) + 1e-6))
    y = jnp.concatenate(outs, axis=-1) * g_ref[...]
    r = r_ref[...].astype(F32)
    o_ref[...] = (y * (r * _sigmoid(r))).astype(o_ref.dtype)


def _gla(p1, w_a2, b_a, norm_g, layer, batch, seq):
    t = batch * seq
    tb = SEQ_BLOCK
    nblk = seq // tb

    def at(width, col):
        return pl.BlockSpec((tb, width), lambda b, i: (b * nblk + i, col))

    return pl.pallas_call(
        _gla_kernel,
        grid=(batch, nblk),
        in_specs=[at(256, 0), at(256, 1), at(512, 1), at(512, 2), at(GA_PAD, GA_COL // GA_PAD),
                  _layer(w_a2, layer), _layer(b_a, layer), _layer(norm_g, layer)],
        out_specs=pl.BlockSpec((tb, BRANCH_W), lambda b, i: (b * nblk + i, 0)),
        out_shape=jax.ShapeDtypeStruct((t, BRANCH_W), BF16),
        scratch_shapes=[pltpu.VMEM((N_HEADS * GLA_DV, N_HEADS * GLA_DK), F32),
                        pltpu.VMEM((tb, N_HEADS * GLA_DK), BF16), pltpu.VMEM((tb, N_HEADS * GLA_DK), BF16),
                        pltpu.VMEM((tb, BRANCH_W), F32)],
        compiler_params=pltpu.CompilerParams(dimension_semantics=("parallel", "arbitrary"),
                                             vmem_limit_bytes=VMEM_LIMIT),
        name="gla",
    )(p1, p1, p1, p1, p1, w_a2, b_a, norm_g)


def _ret_kernel(q_ref, k_ref, v_ref, r_ref, g_ref, o_ref, st_ref, qd_ref, kd_ref, acc_ref):
    c = RET_CHUNK
    grp = RET_GROUP
    tb = q_ref.shape[0]
    width = N_HEADS * RET_DK

    @pl.when(pl.program_id(1) == 0)
    def _():
        st_ref[...] = jnp.zeros_like(st_ref)

    log_g = [math.log1p(-(2.0 ** (-5.0 - h))) for h in range(N_HEADS)]
    lane_head = lax.broadcasted_iota(jnp.int32, (grp, width), 1) // RET_DK
    lg_lane = jnp.zeros((grp, width), F32)
    for h in range(N_HEADS):
        lg_lane = jnp.where(lane_head == h, log_g[h], lg_lane)
    ridx = (lax.broadcasted_iota(jnp.int32, (grp, width), 0) % c).astype(F32)
    q_scale = jnp.exp((ridx + 1.0) * lg_lane)
    k_scale = jnp.exp((c - 1.0 - ridx) * lg_lane)
    chunk_decay = jnp.exp(c * lg_lane[0:1, :])
    row = lax.broadcasted_iota(jnp.int32, (grp, grp), 0)
    col = lax.broadcasted_iota(jnp.int32, (grp, grp), 1)
    causal = ((row // c) == (col // c)) & (row >= col)
    diff = (row - col).astype(F32)
    st_diag = (lax.broadcasted_iota(jnp.int32, st_ref.shape, 0) // RET_DV
               == lax.broadcasted_iota(jnp.int32, st_ref.shape, 1) // RET_DK)

    for g0 in range(0, tb, grp):
        rows = slice(g0, g0 + grp)
        q = q_ref[rows, :].astype(F32)
        k = k_ref[rows, :].astype(F32)
        qb = q_ref[rows, :]
        qd_ref[rows, :] = (q * q_scale).astype(BF16)
        kd_ref[rows, :] = (k * k_scale).astype(BF16)
        for h in range(N_HEADS):
            hs = slice(h * RET_DV, (h + 1) * RET_DV)
            dmat = jnp.where(causal, jnp.exp(jnp.maximum(diff, 0.0) * log_g[h]), 0.0)
            k_h = jnp.where(lane_head == h, k, 0.0).astype(BF16)
            att = (_nt_dot(qb, k_h) * dmat).astype(BF16)
            acc_ref[rows, hs] = jnp.dot(att, v_ref[rows, hs], preferred_element_type=F32)

    for n in range(tb // c):
        rows = slice(n * c, (n + 1) * c)
        st = st_ref[...]
        acc_ref[rows, :] += _nt_dot(qd_ref[rows, :], st.astype(BF16))
        kv_t = _tn_dot(v_ref[rows, :], kd_ref[rows, :])
        st_ref[...] = st * chunk_decay + jnp.where(st_diag, kv_t, 0.0)

    outs = []
    for h in range(N_HEADS):
        o = acc_ref[:, h * RET_DV:(h + 1) * RET_DV]
        mu = jnp.mean(o, axis=-1, keepdims=True)
        d = o - mu
        outs.append(d * lax.rsqrt(jnp.mean(d * d, axis=-1, keepdims=True) + 1e-5))
    y = jnp.concatenate(outs, axis=-1) * g_ref[...]
    r = r_ref[...].astype(F32)
    o_ref[...] = (y * (r * _sigmoid(r))).astype(o_ref.dtype)


def _ret(p1, norm_g, layer, batch, seq):
    t = batch * seq
    tb = SEQ_BLOCK
    nblk = seq // tb

    def at(width, col):
        return pl.BlockSpec((tb, width), lambda b, i: (b * nblk + i, col))

    return pl.pallas_call(
        _ret_kernel,
        grid=(batch, nblk),
        in_specs=[at(256, 6), at(256, 7), at(512, 4), at(512, 5), _layer(norm_g, layer)],
        out_specs=pl.BlockSpec((tb, BRANCH_W), lambda b, i: (b * nblk + i, 0)),
        out_shape=jax.ShapeDtypeStruct((t, BRANCH_W), BF16),
        scratch_shapes=[pltpu.VMEM((N_HEADS * RET_DV, N_HEADS * RET_DK), F32),
                        pltpu.VMEM((tb, N_HEADS * RET_DK), BF16), pltpu.VMEM((tb, N_HEADS * RET_DK), BF16),
                        pltpu.VMEM((tb, BRANCH_W), F32)],
        compiler_params=pltpu.CompilerParams(dimension_semantics=("parallel", "arbitrary"),
                                             vmem_limit_bytes=VMEM_LIMIT),
        name="retention",
    )(p1, p1, p1, p1, norm_g)


def _dil_kernel(q_ref, k_ref, v_ref, kp_ref, vp_ref, o_ref, l_ref, o_scr, l_scr, *, dilation):
    n = DIL_SPAN
    nj = q_ref.shape[1] // n
    qry = lax.broadcasted_iota(jnp.int32, (n, 2 * n), 0)
    key = lax.broadcasted_iota(jnp.int32, (n, 2 * n), 1)
    dist = qry + n - key
    bias_t = (dist >= 0) & (dist <= n)
    bias_t_first = bias_t & ((key >= n) | (pl.program_id(1) > 0))
    ones = jnp.ones((2 * n, n), BF16)
    lane_head = lax.broadcasted_iota(jnp.int32, (n, LANES), 1) // (LANES // N_HEADS)

    def unit(c, j, bias, keys, values):
        start = j * (n * dilation) + c
        dst = pl.ds(start, n, stride=dilation) if dilation > 1 else pl.ds(start, n)
        q = q_ref[c, pl.ds(j * n, n), :]
        lse_tile = jnp.zeros((n, LANES), F32)
        for h in range(N_HEADS):
            hs = slice(h * DIL_HD, (h + 1) * DIL_HD)
            s = jnp.where(bias, _nt_dot(q[:, hs], keys(hs)), MASK_VALUE)
            m = jnp.max(s, axis=-1, keepdims=True)
            p = jnp.exp2(s - m)
            o_den = jnp.dot(p.astype(BF16), jnp.concatenate([values(hs), ones], axis=1),
                            preferred_element_type=F32)
            den = o_den[:, DIL_HD:]
            o_scr[h, dst, :] = o_den[:, :DIL_HD] / den
            lse_tile = jnp.where(lane_head == h, m * math.log(2.0) + jnp.log(den), lse_tile)
        l_scr[dst, :] = lse_tile

    def first_unit(c, carry):
        unit(c, 0, bias_t_first,
             lambda hs: jnp.concatenate([kp_ref[c, :, hs], k_ref[c, pl.ds(0, n), hs]], axis=0),
             lambda hs: jnp.concatenate([vp_ref[c, :, hs], v_ref[c, pl.ds(0, n), hs]], axis=0))
        return carry

    def later_unit(u, carry):
        c = u // (nj - 1)
        j = 1 + u % (nj - 1)
        r0 = pl.multiple_of((j - 1) * n, n)
        unit(c, j, bias_t,
             lambda hs: k_ref[c, pl.ds(r0, 2 * n), hs],
             lambda hs: v_ref[c, pl.ds(r0, 2 * n), hs])
        return carry

    lax.fori_loop(0, dilation, first_unit, 0, unroll=min(dilation, DIL_UNROLL))
    if nj > 1:
        lax.fori_loop(0, dilation * (nj - 1), later_unit, 0, unroll=DIL_UNROLL)
    for h in range(N_HEADS):
        o_ref[:, h * DIL_HD:(h + 1) * DIL_HD] = o_scr[h].astype(o_ref.dtype)
    l_ref[...] = l_scr[...]


def _dil(pg, dilation, batch, seq):
    t = batch * seq
    blk = DIL_BLOCK
    nblk = seq // blk
    rows = blk // dilation
    prev_per_blk = rows // DIL_SPAN

    def cur(part):
        return pl.BlockSpec((None, dilation, rows, BRANCH_W), lambda b, i: (b, 0, i, part))

    def prev(part):
        return pl.BlockSpec((None, dilation, DIL_SPAN, BRANCH_W),
                            lambda b, i: (b, 0, jnp.maximum(i * prev_per_blk - 1, 0), part))

    return pl.pallas_call(
        functools.partial(_dil_kernel, dilation=dilation),
        grid=(batch, nblk),
        in_specs=[cur(0), cur(1), cur(2), prev(1), prev(2)],
        out_specs=[pl.BlockSpec((blk, BRANCH_W), lambda b, i: (b * nblk + i, 0)),
                   pl.BlockSpec((blk, LANES), lambda b, i: (b * nblk + i, 0))],
        out_shape=[jax.ShapeDtypeStruct((t, BRANCH_W), BF16), jax.ShapeDtypeStruct((t, LANES), F32)],
        scratch_shapes=[pltpu.VMEM((N_HEADS, blk, DIL_HD), F32), pltpu.VMEM((blk, LANES), F32)],
        compiler_params=pltpu.CompilerParams(dimension_semantics=("parallel", "arbitrary"),
                                             vmem_limit_bytes=VMEM_LIMIT),
        name=f"dilated_attn_{dilation}",
    )(pg, pg, pg, pg, pg)


def _layer_norm(v, g, b):
    mu = jnp.mean(v, axis=-1, keepdims=True)
    d = v - mu
    return d * lax.rsqrt(jnp.mean(d * d, axis=-1, keepdims=True) + 1e-5) * g + b


def _merge_kernel(x_ref, oa_ref, ob_ref, o1_ref, o2_ref, o3_ref, l1_ref, l2_ref, l3_ref,
                  wg_ref, bg_ref, wb_ref, wo_ref, lg_ref, lb_ref, y_ref, *, alpha):
    tm, d = x_ref.shape
    lanes_per_head = LANES // N_HEADS
    for r0 in range(0, tm, tm // MERGE_ROW_SPLIT):
        rows = slice(r0, r0 + tm // MERGE_ROW_SPLIT)
        l1, l2, l3 = l1_ref[rows, :], l2_ref[rows, :], l3_ref[rows, :]
        m = jnp.maximum(jnp.maximum(l1, l2), l3)
        e1, e2, e3 = jnp.exp(l1 - m), jnp.exp(l2 - m), jnp.exp(l3 - m)
        tot = e1 + e2 + e3
        w1, w2, w3 = e1 / tot, e2 / tot, e3 / tot
        parts = []
        for h in range(N_HEADS):
            hs = slice(h * DIL_HD, (h + 1) * DIL_HD)
            ls = slice(h * lanes_per_head, h * lanes_per_head + 1)
            parts.append(w1[:, ls] * o1_ref[rows, hs].astype(F32) + w2[:, ls] * o2_ref[rows, hs].astype(F32)
                         + w3[:, ls] * o3_ref[rows, hs].astype(F32))
        oc = jnp.concatenate(parts, axis=-1).astype(BF16)
        x = x_ref[rows, :]
        xb = x.astype(BF16)
        merged = None
        for j, o in enumerate((oa_ref[rows, :], ob_ref[rows, :], oc)):
            gate = _sigmoid(jnp.dot(xb, wg_ref[:, j * d:(j + 1) * d], preferred_element_type=F32)
                            + bg_ref[:, j * d:(j + 1) * d])
            term = gate * jnp.dot(o, wb_ref[j], preferred_element_type=F32)
            merged = term if merged is None else merged + term
        hmix = jnp.dot(merged.astype(BF16), wo_ref[...], preferred_element_type=F32)
        y_ref[rows, :] = _layer_norm(alpha * x + hmix, lg_ref[...], lb_ref[...])


def _merge(x, oa, ob, o_dil, l_dil, w_packed, b_gate, w_branch, w_out, ln_g, ln_b, layer, alpha, tm):
    t, d = x.shape

    def rows(width):
        return pl.BlockSpec((tm, width), lambda i: (i, 0))

    params = (b_gate, w_branch, w_out, ln_g, ln_b)
    return pl.pallas_call(
        functools.partial(_merge_kernel, alpha=alpha),
        grid=(t // tm,),
        in_specs=[rows(d)] + [rows(BRANCH_W)] * 5 + [rows(LANES)] * 3
                 + [_cols(w_packed, layer, 3 * d, W_GATE_COL)] + [_layer(p, layer) for p in params],
        out_specs=rows(d),
        out_shape=jax.ShapeDtypeStruct((t, d), F32),
        compiler_params=pltpu.CompilerParams(dimension_semantics=("parallel",), vmem_limit_bytes=VMEM_LIMIT),
        name="merge_out_ln",
    )(x, oa, ob, *o_dil, *l_dil, w_packed, *params)


def _ffn_kernel(x_ref, wg_ref, wu_ref, wd_ref, lg_ref, lb_ref, y_ref, act_ref, *, alpha):
    d_ff = wg_ref.shape[1]
    tm = x_ref.shape[0]
    for r0 in range(0, tm, tm // FF_ROW_SPLIT):
        rows = slice(r0, r0 + tm // FF_ROW_SPLIT)
        x = x_ref[rows, :]
        xb = x.astype(BF16)
        for j0 in range(0, d_ff, FF_CHUNK):
            cs = slice(j0, min(j0 + FF_CHUNK, d_ff))
            g = jnp.dot(xb, wg_ref[:, cs], preferred_element_type=F32)
            u = jnp.dot(xb, wu_ref[:, cs], preferred_element_type=F32)
            act_ref[rows, cs] = (g * _sigmoid(g) * u).astype(BF16)
        f = jnp.dot(act_ref[rows, :], wd_ref[...], preferred_element_type=F32)
        y_ref[rows, :] = _layer_norm(alpha * x + f, lg_ref[...], lb_ref[...])


def _ffn(x, w_gate, w_up, w_down, ln_g, ln_b, layer, alpha, tm):
    t, d = x.shape
    d_ff = w_gate.shape[2]
    rows = pl.BlockSpec((tm, d), lambda i: (i, 0))
    params = (w_gate, w_up, w_down, ln_g, ln_b)
    return pl.pallas_call(
        functools.partial(_ffn_kernel, alpha=alpha),
        grid=(t // tm,),
        in_specs=[rows] + [_layer(p, layer) for p in params],
        out_specs=rows,
        out_shape=jax.ShapeDtypeStruct((t, d), F32),
        scratch_shapes=[pltpu.VMEM((tm, d_ff), BF16)],
        compiler_params=pltpu.CompilerParams(dimension_semantics=("parallel",), vmem_limit_bytes=VMEM_LIMIT),
        name="swiglu_ln",
    )(x, *params)


def _rope_tables(seq):
    pos = jnp.arange(seq, dtype=F32)

    def tables(n_rot, base, width):
        half = n_rot // 2
        inv = base ** (-jnp.arange(half, dtype=F32) * 2.0 / n_rot)
        ang = pos[:, None] * inv[None, :]
        cos = jnp.concatenate([jnp.cos(ang), jnp.cos(ang), jnp.ones((seq, width - n_rot), F32)], axis=1)
        sin = jnp.concatenate([-jnp.sin(ang), jnp.sin(ang), jnp.zeros((seq, width - n_rot), F32)], axis=1)
        return cos, sin

    rc, rs = tables(RET_DK, RET_ROPE_BASE, RET_DK)
    dc, ds = tables(ROPE_DIMS, ROPE_THETA, DIL_HD)
    dil_tabs = []
    for _, dil in DIL_GROUPS:
        for tab in (dc, ds):
            dil_tabs.append(tab.reshape(seq // dil, dil, DIL_HD).transpose(1, 0, 2))
    return (jnp.tile(rc, (1, N_HEADS)), jnp.tile(rs, (1, N_HEADS))), dil_tabs


def _pack_w_in(w_in):
    d = w_in.shape[1]
    widths = (256, 256, 512, 512, GLA_RANK, 256, 256, 512, 512, 1536, 1536, 1536, 3 * d)
    ga0 = sum(widths[:4])
    dil0 = sum(widths[:9])
    gate0 = sum(widths[:12])
    w = w_in.astype(BF16)
    packed = jnp.concatenate(
        [w[:, :, :ga0], w[:, :, ga0 + GLA_RANK:dil0], w[:, :, gate0:], w[:, :, dil0:gate0],
         jnp.pad(w[:, :, ga0:ga0 + GLA_RANK], ((0, 0), (0, 0), (0, GA_PAD - GLA_RANK)))], axis=2)
    assert packed.shape[2] == W_GA_COL + GA_PAD
    return packed


def kernel(x, w_in, w_gla_a2, b_gla_a, gla_norm_g, ret_norm_g, w_branch, b_gate, w_out, ln1_g, ln1_b,
           w_ffn_gate, w_ffn_up, w_ffn_down, ln2_g, ln2_b):
    batch, seq, d = x.shape
    depth = w_in.shape[0]
    alpha = (2 * depth) ** 0.25
    ret_tabs, dil_tabs = _rope_tables(seq)
    w_packed = _pack_w_in(w_in)
    w_a2 = jnp.pad(w_gla_a2, ((0, 0), (0, GA_PAD - GLA_RANK), (0, 0))).astype(BF16)
    w_branch, w_out = w_branch.astype(BF16), w_out.astype(BF16)
    w_ffn_gate, w_ffn_up, w_ffn_down = w_ffn_gate.astype(BF16), w_ffn_up.astype(BF16), w_ffn_down.astype(BF16)
    b_gla_a, gla_norm_g, ret_norm_g, b_gate, ln1_g, ln1_b, ln2_g, ln2_b = (
        p[:, None, :] for p in (b_gla_a, gla_norm_g, ret_norm_g, b_gate, ln1_g, ln1_b, ln2_g, ln2_b))
    xf = x.reshape(batch * seq, d)
    for l in range(depth):
        p1, *p_dil = _proj(xf, w_packed, l, ret_tabs, dil_tabs, batch, seq, ROW_TILE)
        oa = _gla(p1, w_a2, b_gla_a, gla_norm_g, l, batch, seq)
        ob = _ret(p1, ret_norm_g, l, batch, seq)
        o_dil, l_dil = [], []
        for pg, (_, dilation) in zip(p_dil, DIL_GROUPS):
            o, lse = _dil(pg, dilation, batch, seq)
            o_dil.append(o)
            l_dil.append(lse)
        xf = _merge(xf, oa, ob, o_dil, l_dil, w_packed, b_gate, w_branch, w_out, ln1_g, ln1_b, l, alpha,
                    MERGE_ROW_TILE)
        xf = _ffn(xf, w_ffn_gate, w_ffn_up, w_ffn_down, ln2_g, ln2_b, l, alpha, FF_ROW_TILE)
    return xf.reshape(batch, seq, d)
```

```python
import functools
import math

import jax
import jax.numpy as jnp
from jax import lax
from jax.experimental import pallas as pl
from jax.experimental.pallas import tpu as pltpu

F32 = jnp.float32
BF16 = jnp.bfloat16

LANES = 128
N_HEADS = 4
GLA_DK = 64
GLA_DV = 128
GLA_RANK = 16
GLA_TAU = 16.0
GLA_CHUNK = 64
GLA_GROUP = 256
RET_DK = 64
RET_DV = 128
RET_CHUNK = 128
RET_GROUP = 256
RET_ROPE_BASE = 10000.0
DIL_HD = 128
DIL_GROUPS = ((128, 1), (512, 4), (2048, 16))
DIL_SPAN = 128
DIL_BLOCK = DIL_SPAN * max(d for _, d in DIL_GROUPS)
ROPE_THETA = 500000.0
ROPE_DIMS = 32
BRANCH_W = 512
GA_PAD = 256
GA_COL = 3072
RET_QK_COL = 1536
P1_GLA_COLS = (0, 256, 512, 1024)
P1_RET_COLS = (1536, 1792, 2048, 2560)
W_MAIN_COL = 0
W_GATE_COL = 3072
W_DIL_COL = 6144
W_GA_COL = 10752
MASK_VALUE = -1e30

VMEM_LIMIT = 56 * 1024 * 1024
SEQ_BLOCK = 512
ROW_TILE = 512
FF_CHUNK = 256
FF_ROW_TILE = 1024
FF_ROW_SPLIT = 4
MERGE_ROW_TILE = 1024
MERGE_ROW_SPLIT = 4
DIL_UNROLL = 4


def _sigmoid(x):
    return 1.0 / (1.0 + jnp.exp(-x))


def _nt_dot(a, b):
    return lax.dot_general(a, b, (((1,), (1,)), ((), ())), preferred_element_type=F32)


def _tn_dot(a, b):
    return lax.dot_general(a, b, (((0,), (0,)), ((), ())), preferred_element_type=F32)


def _layer(stacked, layer):
    rest = stacked.shape[1:]
    return pl.BlockSpec((None,) + rest, lambda *_: (layer,) + (0,) * len(rest), pipeline_mode=pl.Buffered(1))


def _rotary(x, cos, sin, head_dim, half):
    width = x.shape[1]
    low = (lax.broadcasted_iota(jnp.int32, x.shape, 1) % head_dim) < half
    partner = jnp.where(low, pltpu.roll(x, width - half, 1), pltpu.roll(x, half, 1))
    return x * cos + partner * sin


def _proj_kernel(*refs):
    n_groups = len(DIL_GROUPS)
    x_ref, wm_ref, wq_ref, wk_ref, wv_ref, wa_ref, rc_ref, rs_ref = refs[:8]
    tabs = refs[8:8 + 2 * n_groups]
    wa2_ref, ba_ref, gg_ref, rg_ref = refs[8 + 2 * n_groups:12 + 2 * n_groups]
    oa_ref, ob_ref = refs[12 + 2 * n_groups:14 + 2 * n_groups]
    g_refs = refs[14 + 2 * n_groups:14 + 3 * n_groups]
    (slab_ref, slab4_ref, xp4_ref, xp16_ref, p1_ref,
     gst_ref, gqs_ref, gkd_ref, gacc_ref, rst_ref, rqd_ref, rkd_ref, racc_ref) = refs[14 + 3 * n_groups:]
    tm, k = x_ref.shape

    @pl.when(pl.program_id(1) == 0)
    def _():
        gst_ref[...] = jnp.zeros_like(gst_ref)
        rst_ref[...] = jnp.zeros_like(rst_ref)

    n_slabs = k // LANES
    tn = BRANCH_W
    x = x_ref[...].astype(BF16)
    lane = lax.broadcasted_iota(jnp.int32, (1, tn), 1)
    for j0 in range(0, GA_COL, tn):
        res = jnp.dot(x, wm_ref[:, j0:j0 + tn], preferred_element_type=F32)
        if j0 == 0:
            res = res * jnp.where(lane < N_HEADS * GLA_DK, GLA_DK ** -0.5, 1.0)
        if j0 == RET_QK_COL:
            cos = jnp.concatenate([rc_ref[...]] * 2, axis=1)
            sin = jnp.concatenate([rs_ref[...]] * 2, axis=1)
            res = _rotary(res, cos, sin, RET_DK, RET_DK // 2)
            res = res * jnp.where(lane < N_HEADS * RET_DK, 1.0, RET_DK ** -0.5)
        p1_ref[:, j0:j0 + tn] = res.astype(BF16)
    p1_ref[:, GA_COL:] = jnp.dot(x, wa_ref[...], preferred_element_type=F32).astype(BF16)
    _gla_body(p1_ref, wa2_ref, ba_ref, gg_ref, oa_ref, gst_ref, gqs_ref, gkd_ref, gacc_ref)
    _ret_body(p1_ref, rg_ref, ob_ref, rst_ref, rqd_ref, rkd_ref, racc_ref)
    q4 = tm // 4
    q16 = tm // 16
    for j in range(n_slabs):
        slab_ref[j] = x_ref[:, j * LANES:(j + 1) * LANES]
    for c in range(4):
        for j in range(n_slabs):
            part = slab_ref[j, pl.ds(c, q4, stride=4), :]
            slab4_ref[j, c * q4:(c + 1) * q4, :] = part
            xp4_ref[c * q4:(c + 1) * q4, j * LANES:(j + 1) * LANES] = part.astype(BF16)
    for c0 in range(4):
        for c1 in range(4):
            c = 4 * c1 + c0
            for j in range(n_slabs):
                xp16_ref[c * q16:(c + 1) * q16, j * LANES:(j + 1) * LANES] = (
                    slab4_ref[j, pl.ds(c0 * q4 + c1, q16, stride=4), :].astype(BF16))
    lhs_of = {1: lambda: x, 4: lambda: xp4_ref[...], 16: lambda: xp16_ref[...]}
    for g, (_, dil) in enumerate(DIL_GROUPS):
        rows = tm // dil
        lhs = lhs_of[dil]()
        cos = jnp.concatenate([tabs[2 * g][...].reshape(tm, LANES)] * N_HEADS, axis=1)
        sin = jnp.concatenate([tabs[2 * g + 1][...].reshape(tm, LANES)] * N_HEADS, axis=1)
        for part, w_ref in enumerate((wq_ref, wk_ref, wv_ref)):
            res = jnp.dot(lhs, w_ref[:, g * tn:(g + 1) * tn], preferred_element_type=F32)
            if part < 2:
                res = _rotary(res, cos, sin, DIL_HD, ROPE_DIMS // 2)
            if part == 0:
                res = res * (DIL_HD ** -0.5 * math.log2(math.e))
            g_refs[g][:, :, part * tn:(part + 1) * tn] = res.astype(BF16).reshape(dil, rows, tn)


def _cols(w, layer, width, offset):
    assert offset % width == 0
    return pl.BlockSpec((None, w.shape[1], width), lambda *_: (layer, 0, offset // width),
                        pipeline_mode=pl.Buffered(1))


def _proj(x, w, mixer_params, layer, ret_tabs, dil_tabs, batch, seq):
    assert tuple(d for _, d in DIL_GROUPS) == (1, 4, 16)
    t, k = x.shape
    tm = SEQ_BLOCK
    nblk = seq // tm
    w_specs = [_cols(w, layer, GA_COL, W_MAIN_COL)]
    w_specs += [_cols(w, layer, 3 * BRANCH_W, W_DIL_COL + part * 3 * BRANCH_W) for part in range(3)]
    w_specs += [_cols(w, layer, GA_PAD, W_GA_COL)]

    def row(b, i):
        return (b * nblk + i, 0)

    tab_specs, out_specs, out_shapes = [], [], []
    for _, dil in DIL_GROUPS:
        tab_specs += [pl.BlockSpec((dil, tm // dil, LANES), lambda b, i: (0, i, 0))] * 2
        out_specs.append(pl.BlockSpec((None, dil, tm // dil, 3 * BRANCH_W), lambda b, i: (b, 0, i, 0)))
        out_shapes.append(jax.ShapeDtypeStruct((batch, dil, seq // dil, 3 * BRANCH_W), BF16))
    ret_spec = pl.BlockSpec((tm, N_HEADS * RET_DK), lambda b, i: (i, 0))
    branch = pl.BlockSpec((tm, BRANCH_W), row)
    state = pltpu.VMEM((N_HEADS * GLA_DV, N_HEADS * GLA_DK), F32)
    decayed = pltpu.VMEM((tm, N_HEADS * GLA_DK), BF16)
    acc = pltpu.VMEM((tm, BRANCH_W), F32)
    return pl.pallas_call(
        _proj_kernel,
        grid=(batch, nblk),
        in_specs=[pl.BlockSpec((tm, k), row)] + w_specs + [ret_spec, ret_spec] + tab_specs
                 + [_layer(p, layer) for p in mixer_params],
        out_specs=[branch, branch] + out_specs,
        out_shape=[jax.ShapeDtypeStruct((t, BRANCH_W), BF16)] * 2 + out_shapes,
        scratch_shapes=[pltpu.VMEM((k // LANES, tm, LANES), F32), pltpu.VMEM((k // LANES, tm, LANES), F32),
                        pltpu.VMEM((tm, k), BF16), pltpu.VMEM((tm, k), BF16),
                        pltpu.VMEM((tm, GA_COL + GA_PAD), BF16),
                        state, decayed, decayed, acc, state, decayed, decayed, acc],
        compiler_params=pltpu.CompilerParams(dimension_semantics=("parallel", "arbitrary"),
                                             vmem_limit_bytes=VMEM_LIMIT),
        name="in_proj_mixers",
    )(x, *([w] * 5), *ret_tabs, *dil_tabs, *mixer_params)


def _gla_body(p1_ref, wa_ref, ba_ref, g_ref, o_ref, st_ref, qs_ref, kd_ref, acc_ref):
    c = GLA_CHUNK
    grp = GLA_GROUP
    tb = p1_ref.shape[0]
    width = N_HEADS * GLA_DK
    q_col, k_col, v_col, r_col = P1_GLA_COLS

    z = jnp.dot(p1_ref[:, GA_COL:GA_COL + GA_PAD], wa_ref[...], preferred_element_type=F32) + ba_ref[...]
    log_a = (jnp.minimum(z, 0.0) - jnp.log1p(jnp.exp(-jnp.abs(z)))) * (1.0 / GLA_TAU)
    hi = log_a.astype(BF16)
    lo = (log_a - hi.astype(F32)).astype(BF16)
    hl = jnp.concatenate([hi, lo], axis=1)

    row = lax.broadcasted_iota(jnp.int32, (grp, grp), 0)
    col = lax.broadcasted_iota(jnp.int32, (grp, grp), 1)
    same_chunk = (row // c) == (col // c)
    causal = same_chunk & (row >= col)
    cum_and_total = jnp.concatenate([jnp.where(causal, 1.0, 0.0), jnp.where(same_chunk, 1.0, 0.0)],
                                    axis=0).astype(BF16)
    lane_head = lax.broadcasted_iota(jnp.int32, (grp, width), 1) // GLA_DK
    st_row_head = lax.broadcasted_iota(jnp.int32, st_ref.shape, 0) // GLA_DV
    st_col_head = lax.broadcasted_iota(jnp.int32, st_ref.shape, 1) // GLA_DK
    st_diag = st_row_head == st_col_head

    decays = []
    for g0 in range(0, tb, grp):
        rows = slice(g0, g0 + grp)
        r2 = jnp.dot(cum_and_total, hl[rows], preferred_element_type=F32)
        b = r2[:grp, :width] + r2[:grp, width:]
        b_tot = r2[grp:, :width] + r2[grp:, width:]
        q = p1_ref[rows, q_col:q_col + width].astype(F32)
        k = p1_ref[rows, k_col:k_col + width].astype(F32)
        q_s = (q * jnp.exp(b)).astype(BF16)
        k_s = k * jnp.exp(-b)
        qs_ref[rows, :] = q_s
        kd_ref[rows, :] = (k * jnp.exp(b_tot - b)).astype(BF16)
        for n0 in range(0, grp, c):
            decays.append(jnp.exp(b_tot[n0:n0 + 1, :]))
        for h in range(N_HEADS):
            hs = slice(h * GLA_DV, (h + 1) * GLA_DV)
            k_h = jnp.where(lane_head == h, k_s, 0.0).astype(BF16)
            att = jnp.where(causal, _nt_dot(q_s, k_h), 0.0).astype(BF16)
            v_h = p1_ref[rows, v_col + h * GLA_DV:v_col + (h + 1) * GLA_DV]
            acc_ref[rows, hs] = jnp.dot(att, v_h, preferred_element_type=F32)

    for n, decay in enumerate(decays):
        rows = slice(n * c, (n + 1) * c)
        st = st_ref[...]
        acc_ref[rows, :] += _nt_dot(qs_ref[rows, :], st.astype(BF16))
        kv_t = _tn_dot(p1_ref[rows, v_col:v_col + BRANCH_W], kd_ref[rows, :])
        st_ref[...] = st * decay + jnp.where(st_diag, kv_t, 0.0)

    outs = []
    for h in range(N_HEADS):
        o = acc_ref[:, h * GLA_DV:(h + 1) * GLA_DV]
        outs.append(o * lax.rsqrt(jnp.mean(o * o, axis=-1, keepdims=True) + 1e-6))
    y = jnp.concatenate(outs, axis=-1) * g_ref[...]
    r = p1_ref[:, r_col:r_col + BRANCH_W].astype(F32)
    o_ref[...] = (y * (r * _sigmoid(r))).astype(o_ref.dtype)


def _ret_body(p1_ref, g_ref, o_ref, st_ref, qd_ref, kd_ref, acc_ref):
    c = RET_CHUNK
    grp = RET_GROUP
    tb = p1_ref.shape[0]
    width = N_HEADS * RET_DK
    q_col, k_col, v_col, r_col = P1_RET_COLS

    log_g = [math.log1p(-(2.0 ** (-5.0 - h))) for h in range(N_HEADS)]
    lane_head = lax.broadcasted_iota(jnp.int32, (grp, width), 1) // RET_DK
    lg_lane = jnp.zeros((grp, width), F32)
    for h in range(N_HEADS):
        lg_lane = jnp.where(lane_head == h, log_g[h], lg_lane)
    ridx = (lax.broadcasted_iota(jnp.int32, (grp, width), 0) % c).astype(F32)
    q_scale = jnp.exp((ridx + 1.0) * lg_lane)
    k_scale = jnp.exp((c - 1.0 - ridx) * lg_lane)
    chunk_decay = jnp.exp(c * lg_lane[0:1, :])
    row = lax.broadcasted_iota(jnp.int32, (grp, grp), 0)
    col = lax.broadcasted_iota(jnp.int32, (grp, grp), 1)
    causal = ((row // c) == (col // c)) & (row >= col)
    diff = (row - col).astype(F32)
    st_diag = (lax.broadcasted_iota(jnp.int32, st_ref.shape, 0) // RET_DV
               == lax.broadcasted_iota(jnp.int32, st_ref.shape, 1) // RET_DK)

    for g0 in range(0, tb, grp):
        rows = slice(g0, g0 + grp)
        qb = p1_ref[rows, q_col:q_col + width]
        q = qb.astype(F32)
        k = p1_ref[rows, k_col:k_col + width].astype(F32)
        qd_ref[rows, :] = (q * q_scale).astype(BF16)
        kd_ref[rows, :] = (k * k_scale).astype(BF16)
        for h in range(N_HEADS):
            hs = slice(h * RET_DV, (h + 1) * RET_DV)
            dmat = jnp.where(causal, jnp.exp(jnp.maximum(diff, 0.0) * log_g[h]), 0.0)
            k_h = jnp.where(lane_head == h, k, 0.0).astype(BF16)
            att = (_nt_dot(qb, k_h) * dmat).astype(BF16)
            v_h = p1_ref[rows, v_col + h * RET_DV:v_col + (h + 1) * RET_DV]
            acc_ref[rows, hs] = jnp.dot(att, v_h, preferred_element_type=F32)

    for n in range(tb // c):
        rows = slice(n * c, (n + 1) * c)
        st = st_ref[...]
        acc_ref[rows, :] += _nt_dot(qd_ref[rows, :], st.astype(BF16))
        kv_t = _tn_dot(p1_ref[rows, v_col:v_col + BRANCH_W], kd_ref[rows, :])
        st_ref[...] = st * chunk_decay + jnp.where(st_diag, kv_t, 0.0)

    outs = []
    for h in range(N_HEADS):
        o = acc_ref[:, h * RET_DV:(h + 1) * RET_DV]
        mu = jnp.mean(o, axis=-1, keepdims=True)
        d = o - mu
        outs.append(d * lax.rsqrt(jnp.mean(d * d, axis=-1, keepdims=True) + 1e-5))
    y = jnp.concatenate(outs, axis=-1) * g_ref[...]
    r = p1_ref[:, r_col:r_col + BRANCH_W].astype(F32)
    o_ref[...] = (y * (r * _sigmoid(r))).astype(o_ref.dtype)


def _dil_kernel(q_ref, k_ref, v_ref, kp_ref, vp_ref, o_ref, l_ref, o_scr, l_scr, *, dilation):
    n = DIL_SPAN
    nj = q_ref.shape[1] // n
    qry = lax.broadcasted_iota(jnp.int32, (n, 2 * n), 0)
    key = lax.broadcasted_iota(jnp.int32, (n, 2 * n), 1)
    dist = qry + n - key
    bias_t = (dist >= 0) & (dist <= n)
    bias_t_first = bias_t & ((key >= n) | (pl.program_id(1) > 0))
    ones = jnp.ones((2 * n, n), BF16)
    lane_head = lax.broadcasted_iota(jnp.int32, (n, LANES), 1) // (LANES // N_HEADS)

    def unit(c, j, bias, keys, values):
        start = j * (n * dilation) + c
        dst = pl.ds(start, n, stride=dilation) if dilation > 1 else pl.ds(start, n)
        q = q_ref[c, pl.ds(j * n, n), :]
        lse_tile = jnp.zeros((n, LANES), F32)
        for h in range(N_HEADS):
            hs = slice(h * DIL_HD, (h + 1) * DIL_HD)
            s = jnp.where(bias, _nt_dot(q[:, hs], keys(hs)), MASK_VALUE)
            m = jnp.max(s, axis=-1, keepdims=True)
            p = jnp.exp2(s - m)
            o_den = jnp.dot(p.astype(BF16), jnp.concatenate([values(hs), ones], axis=1),
                            preferred_element_type=F32)
            den = o_den[:, DIL_HD:]
            o_scr[h, dst, :] = o_den[:, :DIL_HD] / den
            lse_tile = jnp.where(lane_head == h, m * math.log(2.0) + jnp.log(den), lse_tile)
        l_scr[dst, :] = lse_tile

    def first_unit(c, carry):
        unit(c, 0, bias_t_first,
             lambda hs: jnp.concatenate([kp_ref[c, :, hs], k_ref[c, pl.ds(0, n), hs]], axis=0),
             lambda hs: jnp.concatenate([vp_ref[c, :, hs], v_ref[c, pl.ds(0, n), hs]], axis=0))
        return carry

    def later_unit(u, carry):
        c = u // (nj - 1)
        j = 1 + u % (nj - 1)
        r0 = pl.multiple_of((j - 1) * n, n)
        unit(c, j, bias_t,
             lambda hs: k_ref[c, pl.ds(r0, 2 * n), hs],
             lambda hs: v_ref[c, pl.ds(r0, 2 * n), hs])
        return carry

    lax.fori_loop(0, dilation, first_unit, 0, unroll=min(dilation, DIL_UNROLL))
    if nj > 1:
        lax.fori_loop(0, dilation * (nj - 1), later_unit, 0, unroll=DIL_UNROLL)
    for h in range(N_HEADS):
        o_ref[:, h * DIL_HD:(h + 1) * DIL_HD] = o_scr[h].astype(o_ref.dtype)
    l_ref[...] = l_scr[...]


def _dil(pg, dilation, batch, seq):
    t = batch * seq
    blk = DIL_BLOCK
    nblk = seq // blk
    rows = blk // dilation
    prev_per_blk = rows // DIL_SPAN

    def cur(part):
        return pl.BlockSpec((None, dilation, rows, BRANCH_W), lambda b, i: (b, 0, i, part))

    def prev(part):
        return pl.BlockSpec((None, dilation, DIL_SPAN, BRANCH_W),
                            lambda b, i: (b, 0, jnp.maximum(i * prev_per_blk - 1, 0), part))

    return pl.pallas_call(
        functools.partial(_dil_kernel, dilation=dilation),
        grid=(batch, nblk),
        in_specs=[cur(0), cur(1), cur(2), prev(1), prev(2)],
        out_specs=[pl.BlockSpec((blk, BRANCH_W), lambda b, i: (b * nblk + i, 0)),
                   pl.BlockSpec((blk, LANES), lambda b, i: (b * nblk + i, 0))],
        out_shape=[jax.ShapeDtypeStruct((t, BRANCH_W), BF16), jax.ShapeDtypeStruct((t, LANES), F32)],
        scratch_shapes=[pltpu.VMEM((N_HEADS, blk, DIL_HD), F32), pltpu.VMEM((blk, LANES), F32)],
        compiler_params=pltpu.CompilerParams(dimension_semantics=("parallel", "arbitrary"),
                                             vmem_limit_bytes=VMEM_LIMIT),
        name=f"dilated_attn_{dilation}",
    )(pg, pg, pg, pg, pg)


def _layer_norm(v, g, b):
    mu = jnp.mean(v, axis=-1, keepdims=True)
    d = v - mu
    return d * lax.rsqrt(jnp.mean(d * d, axis=-1, keepdims=True) + 1e-5) * g + b


def _merge_kernel(x_ref, oa_ref, ob_ref, o1_ref, o2_ref, o3_ref, l1_ref, l2_ref, l3_ref,
                  wg_ref, bg_ref, wb_ref, wo_ref, lg_ref, lb_ref, y_ref, *, alpha):
    tm, d = x_ref.shape
    lanes_per_head = LANES // N_HEADS
    for r0 in range(0, tm, tm // MERGE_ROW_SPLIT):
        rows = slice(r0, r0 + tm // MERGE_ROW_SPLIT)
        l1, l2, l3 = l1_ref[rows, :], l2_ref[rows, :], l3_ref[rows, :]
        m = jnp.maximum(jnp.maximum(l1, l2), l3)
        e1, e2, e3 = jnp.exp(l1 - m), jnp.exp(l2 - m), jnp.exp(l3 - m)
        tot = e1 + e2 + e3
        w1, w2, w3 = e1 / tot, e2 / tot, e3 / tot
        parts = []
        for h in range(N_HEADS):
            hs = slice(h * DIL_HD, (h + 1) * DIL_HD)
            ls = slice(h * lanes_per_head, h * lanes_per_head + 1)
            parts.append(w1[:, ls] * o1_ref[rows, hs].astype(F32) + w2[:, ls] * o2_ref[rows, hs].astype(F32)
                         + w3[:, ls] * o3_ref[rows, hs].astype(F32))
        oc = jnp.concatenate(parts, axis=-1).astype(BF16)
        x = x_ref[rows, :]
        xb = x.astype(BF16)
        merged = None
        for j, o in enumerate((oa_ref[rows, :], ob_ref[rows, :], oc)):
            gate = _sigmoid(jnp.dot(xb, wg_ref[:, j * d:(j + 1) * d], preferred_element_type=F32)
                            + bg_ref[:, j * d:(j + 1) * d])
            term = gate * jnp.dot(o, wb_ref[j], preferred_element_type=F32)
            merged = term if merged is None else merged + term
        hmix = jnp.dot(merged.astype(BF16), wo_ref[...], preferred_element_type=F32)
        y_ref[rows, :] = _layer_norm(alpha * x + hmix, lg_ref[...], lb_ref[...])


def _merge(x, oa, ob, o_dil, l_dil, w_packed, b_gate, w_branch, w_out, ln_g, ln_b, layer, alpha, tm):
    t, d = x.shape

    def rows(width):
        return pl.BlockSpec((tm, width), lambda i: (i, 0))

    params = (b_gate, w_branch, w_out, ln_g, ln_b)
    return pl.pallas_call(
        functools.partial(_merge_kernel, alpha=alpha),
        grid=(t // tm,),
        in_specs=[rows(d)] + [rows(BRANCH_W)] * 5 + [rows(LANES)] * 3
                 + [_cols(w_packed, layer, 3 * d, W_GATE_COL)] + [_layer(p, layer) for p in params],
        out_specs=rows(d),
        out_shape=jax.ShapeDtypeStruct((t, d), F32),
        compiler_params=pltpu.CompilerParams(dimension_semantics=("parallel",), vmem_limit_bytes=VMEM_LIMIT),
        name="merge_out_ln",
    )(x, oa, ob, *o_dil, *l_dil, w_packed, *params)


def _ffn_kernel(x_ref, wg_ref, wu_ref, wd_ref, lg_ref, lb_ref, y_ref, act_ref, *, alpha):
    d_ff = wg_ref.shape[1]
    tm = x_ref.shape[0]
    for r0 in range(0, tm, tm // FF_ROW_SPLIT):
        rows = slice(r0, r0 + tm // FF_ROW_SPLIT)
        x = x_ref[rows, :]
        xb = x.astype(BF16)
        for j0 in range(0, d_ff, FF_CHUNK):
            cs = slice(j0, min(j0 + FF_CHUNK, d_ff))
            g = jnp.dot(xb, wg_ref[:, cs], preferred_element_type=F32)
            u = jnp.dot(xb, wu_ref[:, cs], preferred_element_type=F32)
            act_ref[rows, cs] = (g * _sigmoid(g) * u).astype(BF16)
        f = jnp.dot(act_ref[rows, :], wd_ref[...], preferred_element_type=F32)
        y_ref[rows, :] = _layer_norm(alpha * x + f, lg_ref[...], lb_ref[...])


def _ffn(x, w_gate, w_up, w_down, ln_g, ln_b, layer, alpha, tm):
    t, d = x.shape
    d_ff = w_gate.shape[2]
    rows = pl.BlockSpec((tm, d), lambda i: (i, 0))
    params = (w_gate, w_up, w_down, ln_g, ln_b)
    return pl.pallas_call(
        functools.partial(_ffn_kernel, alpha=alpha),
        grid=(t // tm,),
        in_specs=[rows] + [_layer(p, layer) for p in params],
        out_specs=rows,
        out_shape=jax.ShapeDtypeStruct((t, d), F32),
        scratch_shapes=[pltpu.VMEM((tm, d_ff), BF16)],
        compiler_params=pltpu.CompilerParams(dimension_semantics=("parallel",), vmem_limit_bytes=VMEM_LIMIT),
        name="swiglu_ln",
    )(x, *params)


def _rope_tables(seq):
    pos = jnp.arange(seq, dtype=F32)

    def tables(n_rot, base, width):
        half = n_rot // 2
        inv = base ** (-jnp.arange(half, dtype=F32) * 2.0 / n_rot)
        ang = pos[:, None] * inv[None, :]
        cos = jnp.concatenate([jnp.cos(ang), jnp.cos(ang), jnp.ones((seq, width - n_rot), F32)], axis=1)
        sin = jnp.concatenate([-jnp.sin(ang), jnp.sin(ang), jnp.zeros((seq, width - n_rot), F32)], axis=1)
        return cos, sin

    rc, rs = tables(RET_DK, RET_ROPE_BASE, RET_DK)
    dc, ds = tables(ROPE_DIMS, ROPE_THETA, DIL_HD)
    dil_tabs = []
    for _, dil in DIL_GROUPS:
        for tab in (dc, ds):
            dil_tabs.append(tab.reshape(seq // dil, dil, DIL_HD).transpose(1, 0, 2))
    return (jnp.tile(rc, (1, N_HEADS)), jnp.tile(rs, (1, N_HEADS))), dil_tabs


def _pack_w_in(w_in):
    d = w_in.shape[1]
    widths = (256, 256, 512, 512, GLA_RANK, 256, 256, 512, 512, 1536, 1536, 1536, 3 * d)
    ga0 = sum(widths[:4])
    dil0 = sum(widths[:9])
    gate0 = sum(widths[:12])
    w = w_in.astype(BF16)
    packed = jnp.concatenate(
        [w[:, :, :ga0], w[:, :, ga0 + GLA_RANK:dil0], w[:, :, gate0:], w[:, :, dil0:gate0],
         jnp.pad(w[:, :, ga0:ga0 + GLA_RANK], ((0, 0), (0, 0), (0, GA_PAD - GLA_RANK)))], axis=2)
    assert packed.shape[2] == W_GA_COL + GA_PAD
    return packed


def kernel(x, w_in, w_gla_a2, b_gla_a, gla_norm_g, ret_norm_g, w_branch, b_gate, w_out, ln1_g, ln1_b,
           w_ffn_gate, w_ffn_up, w_ffn_down, ln2_g, ln2_b):
    batch, seq, d = x.shape
    depth = w_in.shape[0]
    alpha = (2 * depth) ** 0.25
    ret_tabs, dil_tabs = _rope_tables(seq)
    w_packed = _pack_w_in(w_in)
    w_a2 = jnp.pad(w_gla_a2, ((0, 0), (0, GA_PAD - GLA_RANK), (0, 0))).astype(BF16)
    w_branch, w_out = w_branch.astype(BF16), w_out.astype(BF16)
    w_ffn_gate, w_ffn_up, w_ffn_down = w_ffn_gate.astype(BF16), w_ffn_up.astype(BF16), w_ffn_down.astype(BF16)
    b_gla_a, gla_norm_g, ret_norm_g, b_gate, ln1_g, ln1_b, ln2_g, ln2_b = (
        p[:, None, :] for p in (b_gla_a, gla_norm_g, ret_norm_g, b_gate, ln1_g, ln1_b, ln2_g, ln2_b))
    xf = x.reshape(batch * seq, d)
    for l in range(depth):
        oa, ob, *p_dil = _proj(xf, w_packed, (w_a2, b_gla_a, gla_norm_g, ret_norm_g), l, ret_tabs, dil_tabs,
                               batch, seq)
        o_dil, l_dil = [], []
        for pg, (_, dilation) in zip(p_dil, DIL_GROUPS):
            o, lse = _dil(pg, dilation, batch, seq)
            o_dil.append(o)
            l_dil.append(lse)
        xf = _merge(xf, oa, ob, o_dil, l_dil, w_packed, b_gate, w_branch, w_out, ln1_g, ln1_b, l, alpha,
                    MERGE_ROW_TILE)
        xf = _ffn(xf, w_ffn_gate, w_ffn_up, w_ffn_down, ln2_g, ln2_b, l, alpha, FF_ROW_TILE)
    return xf.reshape(batch, seq, d)
```

```python
import functools
import math

import jax
import jax.numpy as jnp
from jax import lax
from jax.experimental import pallas as pl
from jax.experimental.pallas import tpu as pltpu

F32 = jnp.float32
BF16 = jnp.bfloat16

LANES = 128
N_HEADS = 4
GLA_DK = 64
GLA_DV = 128
GLA_RANK = 16
GLA_TAU = 16.0
GLA_CHUNK = 64
GLA_GROUP = 256
RET_DK = 64
RET_DV = 128
RET_CHUNK = 128
RET_GROUP = 256
RET_ROPE_BASE = 10000.0
DIL_HD = 128
DIL_GROUPS = ((128, 1), (512, 4), (2048, 16))
DIL_SPAN = 128
DIL_BLOCK = DIL_SPAN * max(d for _, d in DIL_GROUPS)
ROPE_THETA = 500000.0
ROPE_DIMS = 32
BRANCH_W = 512
GA_PAD = 256
GA_COL = 3072
RET_QK_COL = 1536
P1_GLA_COLS = (0, 256, 512, 1024)
P1_RET_COLS = (1536, 1792, 2048, 2560)
W_MAIN_COL = 0
W_GATE_COL = 3072
W_DIL_COL = 6144
W_GA_COL = 10752
MASK_VALUE = -1e30

VMEM_LIMIT = 56 * 1024 * 1024
SEQ_BLOCK = 512
ROW_TILE = 512
FF_CHUNK = 256
FF_ROW_TILE = 1024
FF_ROW_SPLIT = 4
MERGE_ROW_TILE = 1024
MERGE_ROW_SPLIT = 4
DIL_UNROLL = 4


def _sigmoid(x):
    return 1.0 / (1.0 + jnp.exp(-x))


def _nt_dot(a, b):
    return lax.dot_general(a, b, (((1,), (1,)), ((), ())), preferred_element_type=F32)


def _tn_dot(a, b):
    return lax.dot_general(a, b, (((0,), (0,)), ((), ())), preferred_element_type=F32)


def _layer(stacked, layer):
    rest = stacked.shape[1:]
    return pl.BlockSpec((None,) + rest, lambda *_: (layer,) + (0,) * len(rest), pipeline_mode=pl.Buffered(1))


def _rotary(x, cos, sin, head_dim, half):
    width = x.shape[1]
    low = (lax.broadcasted_iota(jnp.int32, x.shape, 1) % head_dim) < half
    partner = jnp.where(low, pltpu.roll(x, width - half, 1), pltpu.roll(x, half, 1))
    return x * cos + partner * sin


def _proj_kernel(*refs):
    n_groups = len(DIL_GROUPS)
    x_ref, wm_ref, wq_ref, wk_ref, wv_ref, wa_ref, rc_ref, rs_ref = refs[:8]
    tabs = refs[8:8 + 2 * n_groups]
    wa2_ref, ba_ref, gg_ref, rg_ref = refs[8 + 2 * n_groups:12 + 2 * n_groups]
    oa_ref, ob_ref = refs[12 + 2 * n_groups:14 + 2 * n_groups]
    g_refs = refs[14 + 2 * n_groups:14 + 3 * n_groups]
    (slab_ref, slab4_ref, xp4_ref, xp16_ref, p1_ref,
     gst_ref, gqs_ref, gkd_ref, gacc_ref, rst_ref, rqd_ref, rkd_ref, racc_ref) = refs[14 + 3 * n_groups:]
    tm, k = x_ref.shape

    @pl.when(pl.program_id(1) == 0)
    def _():
        gst_ref[...] = jnp.zeros_like(gst_ref)
        rst_ref[...] = jnp.zeros_like(rst_ref)

    n_slabs = k // LANES
    tn = BRANCH_W
    x = x_ref[...].astype(BF16)
    lane = lax.broadcasted_iota(jnp.int32, (1, tn), 1)
    for j0 in range(0, GA_COL, tn):
        res = jnp.dot(x, wm_ref[:, j0:j0 + tn], preferred_element_type=F32)
        if j0 == 0:
            res = res * jnp.where(lane < N_HEADS * GLA_DK, GLA_DK ** -0.5, 1.0)
        if j0 == RET_QK_COL:
            cos = jnp.concatenate([rc_ref[...]] * 2, axis=1)
            sin = jnp.concatenate([rs_ref[...]] * 2, axis=1)
            res = _rotary(res, cos, sin, RET_DK, RET_DK // 2)
            res = res * jnp.where(lane < N_HEADS * RET_DK, 1.0, RET_DK ** -0.5)
        p1_ref[:, j0:j0 + tn] = res.astype(BF16)
    p1_ref[:, GA_COL:] = jnp.dot(x, wa_ref[...], preferred_element_type=F32).astype(BF16)

    def dilated_projections():
        q4 = tm // 4
        q16 = tm // 16
        for j in range(n_slabs):
            slab_ref[j] = x_ref[:, j * LANES:(j + 1) * LANES]
        for c in range(4):
            for j in range(n_slabs):
                part = slab_ref[j, pl.ds(c, q4, stride=4), :]
                slab4_ref[j, c * q4:(c + 1) * q4, :] = part
                xp4_ref[c * q4:(c + 1) * q4, j * LANES:(j + 1) * LANES] = part.astype(BF16)
        yield
        for c0 in range(4):
            for c1 in range(4):
                c = 4 * c1 + c0
                for j in range(n_slabs):
                    xp16_ref[c * q16:(c + 1) * q16, j * LANES:(j + 1) * LANES] = (
                        slab4_ref[j, pl.ds(c0 * q4 + c1, q16, stride=4), :].astype(BF16))
        yield
        lhs_of = {1: lambda: x, 4: lambda: xp4_ref[...], 16: lambda: xp16_ref[...]}
        for g, (_, dil) in enumerate(DIL_GROUPS):
            rows = tm // dil
            lhs = lhs_of[dil]()
            cos = jnp.concatenate([tabs[2 * g][...].reshape(tm, LANES)] * N_HEADS, axis=1)
            sin = jnp.concatenate([tabs[2 * g + 1][...].reshape(tm, LANES)] * N_HEADS, axis=1)
            for part, w_ref in enumerate((wq_ref, wk_ref, wv_ref)):
                res = jnp.dot(lhs, w_ref[:, g * tn:(g + 1) * tn], preferred_element_type=F32)
                if part < 2:
                    res = _rotary(res, cos, sin, DIL_HD, ROPE_DIMS // 2)
                if part == 0:
                    res = res * (DIL_HD ** -0.5 * math.log2(math.e))
                g_refs[g][:, :, part * tn:(part + 1) * tn] = res.astype(BF16).reshape(dil, rows, tn)
                yield

    _interleave(dilated_projections(),
                _gla_stages(p1_ref, wa2_ref, ba_ref, gg_ref, oa_ref, gst_ref, gqs_ref, gkd_ref, gacc_ref),
                _ret_stages(p1_ref, rg_ref, ob_ref, rst_ref, rqd_ref, rkd_ref, racc_ref))


def _interleave(*streams):
    for s in streams:
        for _ in s:
            pass


def _cols(w, layer, width, offset):
    assert offset % width == 0
    return pl.BlockSpec((None, w.shape[1], width), lambda *_: (layer, 0, offset // width),
                        pipeline_mode=pl.Buffered(1))


def _proj(x, w, mixer_params, layer, ret_tabs, dil_tabs, batch, seq):
    assert tuple(d for _, d in DIL_GROUPS) == (1, 4, 16)
    t, k = x.shape
    tm = SEQ_BLOCK
    nblk = seq // tm
    w_specs = [_cols(w, layer, GA_COL, W_MAIN_COL)]
    w_specs += [_cols(w, layer, 3 * BRANCH_W, W_DIL_COL + part * 3 * BRANCH_W) for part in range(3)]
    w_specs += [_cols(w, layer, GA_PAD, W_GA_COL)]

    def row(b, i):
        return (b * nblk + i, 0)

    tab_specs, out_specs, out_shapes = [], [], []
    for _, dil in DIL_GROUPS:
        tab_specs += [pl.BlockSpec((dil, tm // dil, LANES), lambda b, i: (0, i, 0))] * 2
        out_specs.append(pl.BlockSpec((None, dil, tm // dil, 3 * BRANCH_W), lambda b, i: (b, 0, i, 0)))
        out_shapes.append(jax.ShapeDtypeStruct((batch, dil, seq // dil, 3 * BRANCH_W), BF16))
    ret_spec = pl.BlockSpec((tm, N_HEADS * RET_DK), lambda b, i: (i, 0))
    branch = pl.BlockSpec((tm, BRANCH_W), row)
    state = pltpu.VMEM((N_HEADS * GLA_DV, N_HEADS * GLA_DK), F32)
    decayed = pltpu.VMEM((tm, N_HEADS * GLA_DK), BF16)
    acc = pltpu.VMEM((tm, BRANCH_W), F32)
    return pl.pallas_call(
        _proj_kernel,
        grid=(batch, nblk),
        in_specs=[pl.BlockSpec((tm, k), row)] + w_specs + [ret_spec, ret_spec] + tab_specs
                 + [_layer(p, layer) for p in mixer_params],
        out_specs=[branch, branch] + out_specs,
        out_shape=[jax.ShapeDtypeStruct((t, BRANCH_W), BF16)] * 2 + out_shapes,
        scratch_shapes=[pltpu.VMEM((k // LANES, tm, LANES), F32), pltpu.VMEM((k // LANES, tm, LANES), F32),
                        pltpu.VMEM((tm, k), BF16), pltpu.VMEM((tm, k), BF16),
                        pltpu.VMEM((tm, GA_COL + GA_PAD), BF16),
                        state, decayed, decayed, acc, state, decayed, decayed, acc],
        compiler_params=pltpu.CompilerParams(dimension_semantics=("parallel", "arbitrary"),
                                             vmem_limit_bytes=VMEM_LIMIT),
        name="in_proj_mixers",
    )(x, *([w] * 5), *ret_tabs, *dil_tabs, *mixer_params)


def _gla_stages(p1_ref, wa_ref, ba_ref, g_ref, o_ref, st_ref, qs_ref, kd_ref, acc_ref):
    c = GLA_CHUNK
    grp = GLA_GROUP
    tb = p1_ref.shape[0]
    width = N_HEADS * GLA_DK
    q_col, k_col, v_col, r_col = P1_GLA_COLS

    z = jnp.dot(p1_ref[:, GA_COL:GA_COL + GA_PAD], wa_ref[...], preferred_element_type=F32) + ba_ref[...]
    log_a = (jnp.minimum(z, 0.0) - jnp.log1p(jnp.exp(-jnp.abs(z)))) * (1.0 / GLA_TAU)
    hi = log_a.astype(BF16)
    lo = (log_a - hi.astype(F32)).astype(BF16)
    hl = jnp.concatenate([hi, lo], axis=1)

    row = lax.broadcasted_iota(jnp.int32, (grp, grp), 0)
    col = lax.broadcasted_iota(jnp.int32, (grp, grp), 1)
    same_chunk = (row // c) == (col // c)
    causal = same_chunk & (row >= col)
    cum_and_total = jnp.concatenate([jnp.where(causal, 1.0, 0.0), jnp.where(same_chunk, 1.0, 0.0)],
                                    axis=0).astype(BF16)
    lane_head = lax.broadcasted_iota(jnp.int32, (grp, width), 1) // GLA_DK
    st_row_head = lax.broadcasted_iota(jnp.int32, st_ref.shape, 0) // GLA_DV
    st_col_head = lax.broadcasted_iota(jnp.int32, st_ref.shape, 1) // GLA_DK
    st_diag = st_row_head == st_col_head

    decays = []
    for g0 in range(0, tb, grp):
        rows = slice(g0, g0 + grp)
        r2 = jnp.dot(cum_and_total, hl[rows], preferred_element_type=F32)
        b = r2[:grp, :width] + r2[:grp, width:]
        b_tot = r2[grp:, :width] + r2[grp:, width:]
        q = p1_ref[rows, q_col:q_col + width].astype(F32)
        k = p1_ref[rows, k_col:k_col + width].astype(F32)
        q_s = (q * jnp.exp(b)).astype(BF16)
        k_s = k * jnp.exp(-b)
        qs_ref[rows, :] = q_s
        kd_ref[rows, :] = (k * jnp.exp(b_tot - b)).astype(BF16)
        for n0 in range(0, grp, c):
            decays.append(jnp.exp(b_tot[n0:n0 + 1, :]))
        for h in range(N_HEADS):
            hs = slice(h * GLA_DV, (h + 1) * GLA_DV)
            k_h = jnp.where(lane_head == h, k_s, 0.0).astype(BF16)
            att = jnp.where(causal, _nt_dot(q_s, k_h), 0.0).astype(BF16)
            v_h = p1_ref[rows, v_col + h * GLA_DV:v_col + (h + 1) * GLA_DV]
            acc_ref[rows, hs] = jnp.dot(att, v_h, preferred_element_type=F32)
        yield

    for n, decay in enumerate(decays):
        rows = slice(n * c, (n + 1) * c)
        st = st_ref[...]
        acc_ref[rows, :] += _nt_dot(qs_ref[rows, :], st.astype(BF16))
        kv_t = _tn_dot(p1_ref[rows, v_col:v_col + BRANCH_W], kd_ref[rows, :])
        st_ref[...] = st * decay + jnp.where(st_diag, kv_t, 0.0)
        if n % 2 == 1:
            yield

    outs = []
    for h in range(N_HEADS):
        o = acc_ref[:, h * GLA_DV:(h + 1) * GLA_DV]
        outs.append(o * lax.rsqrt(jnp.mean(o * o, axis=-1, keepdims=True) + 1e-6))
    y = jnp.concatenate(outs, axis=-1) * g_ref[...]
    r = p1_ref[:, r_col:r_col + BRANCH_W].astype(F32)
    o_ref[...] = (y * (r * _sigmoid(r))).astype(o_ref.dtype)


def _ret_stages(p1_ref, g_ref, o_ref, st_ref, qd_ref, kd_ref, acc_ref):
    c = RET_CHUNK
    grp = RET_GROUP
    tb = p1_ref.shape[0]
    width = N_HEADS * RET_DK
    q_col, k_col, v_col, r_col = P1_RET_COLS

    log_g = [math.log1p(-(2.0 ** (-5.0 - h))) for h in range(N_HEADS)]
    lane_head = lax.broadcasted_iota(jnp.int32, (grp, width), 1) // RET_DK
    lg_lane = jnp.zeros((grp, width), F32)
    for h in range(N_HEADS):
        lg_lane = jnp.where(lane_head == h, log_g[h], lg_lane)
    ridx = (lax.broadcasted_iota(jnp.int32, (grp, width), 0) % c).astype(F32)
    q_scale = jnp.exp((ridx + 1.0) * lg_lane)
    k_scale = jnp.exp((c - 1.0 - ridx) * lg_lane)
    chunk_decay = jnp.exp(c * lg_lane[0:1, :])
    row = lax.broadcasted_iota(jnp.int32, (grp, grp), 0)
    col = lax.broadcasted_iota(jnp.int32, (grp, grp), 1)
    causal = ((row // c) == (col // c)) & (row >= col)
    diff = (row - col).astype(F32)
    st_diag = (lax.broadcasted_iota(jnp.int32, st_ref.shape, 0) // RET_DV
               == lax.broadcasted_iota(jnp.int32, st_ref.shape, 1) // RET_DK)

    for g0 in range(0, tb, grp):
        rows = slice(g0, g0 + grp)
        qb = p1_ref[rows, q_col:q_col + width]
        q = qb.astype(F32)
        k = p1_ref[rows, k_col:k_col + width].astype(F32)
        qd_ref[rows, :] = (q * q_scale).astype(BF16)
        kd_ref[rows, :] = (k * k_scale).astype(BF16)
        for h in range(N_HEADS):
            hs = slice(h * RET_DV, (h + 1) * RET_DV)
            dmat = jnp.where(causal, jnp.exp(jnp.maximum(diff, 0.0) * log_g[h]), 0.0)
            k_h = jnp.where(lane_head == h, k, 0.0).astype(BF16)
            att = (_nt_dot(qb, k_h) * dmat).astype(BF16)
            v_h = p1_ref[rows, v_col + h * RET_DV:v_col + (h + 1) * RET_DV]
            acc_ref[rows, hs] = jnp.dot(att, v_h, preferred_element_type=F32)
        yield

    for n in range(tb // c):
        rows = slice(n * c, (n + 1) * c)
        st = st_ref[...]
        acc_ref[rows, :] += _nt_dot(qd_ref[rows, :], st.astype(BF16))
        kv_t = _tn_dot(p1_ref[rows, v_col:v_col + BRANCH_W], kd_ref[rows, :])
        st_ref[...] = st * chunk_decay + jnp.where(st_diag, kv_t, 0.0)
        if n % 2 == 1:
            yield

    outs = []
    for h in range(N_HEADS):
        o = acc_ref[:, h * RET_DV:(h + 1) * RET_DV]
        mu = jnp.mean(o, axis=-1, keepdims=True)
        d = o - mu
        outs.append(d * lax.rsqrt(jnp.mean(d * d, axis=-1, keepdims=True) + 1e-5))
    y = jnp.concatenate(outs, axis=-1) * g_ref[...]
    r = p1_ref[:, r_col:r_col + BRANCH_W].astype(F32)
    o_ref[...] = (y * (r * _sigmoid(r))).astype(o_ref.dtype)


def _dil_kernel(q_ref, k_ref, v_ref, kp_ref, vp_ref, o_ref, m_ref, d_ref, *scratch, dilation):
    n = DIL_SPAN
    nj = q_ref.shape[1] // n
    if dilation > 1:
        o_scr, m_scr, d_scr = scratch
    qry = lax.broadcasted_iota(jnp.int32, (n, 2 * n), 0)
    key = lax.broadcasted_iota(jnp.int32, (n, 2 * n), 1)
    dist = qry + n - key
    bias_t = (dist >= 0) & (dist <= n)
    bias_t_first = bias_t & ((key >= n) | (pl.program_id(1) > 0))
    ones = jnp.ones((2 * n, n), BF16)
    lane_head = lax.broadcasted_iota(jnp.int32, (n, LANES), 1) // (LANES // N_HEADS)

    def unit(c, j, bias, keys, values):
        q = q_ref[c, pl.ds(j * n, n), :]
        m_tile = jnp.zeros((n, LANES), F32)
        d_tile = jnp.zeros((n, LANES), F32)
        for h in range(N_HEADS):
            hs = slice(h * DIL_HD, (h + 1) * DIL_HD)
            s = jnp.where(bias, _nt_dot(q[:, hs], keys(hs)), MASK_VALUE)
            m = jnp.max(s, axis=-1, keepdims=True)
            p = jnp.exp2(s - m)
            o_den = jnp.dot(p.astype(BF16), jnp.concatenate([values(hs), ones], axis=1),
                            preferred_element_type=F32)
            if dilation == 1:
                o_ref[pl.ds(pl.multiple_of(j * n, n), n), hs] = o_den[:, :DIL_HD].astype(o_ref.dtype)
            else:
                o_scr[h, pl.ds(j * (n * dilation) + c, n, stride=dilation), :] = o_den[:, :DIL_HD]
            m_tile = jnp.where(lane_head == h, m, m_tile)
            d_tile = jnp.where(lane_head == h, o_den[:, DIL_HD:DIL_HD + LANES], d_tile)
        if dilation == 1:
            rows = pl.ds(pl.multiple_of(j * n, n), n)
            m_ref[rows, :] = m_tile
            d_ref[rows, :] = d_tile
        else:
            rows = pl.ds(j * (n * dilation) + c, n, stride=dilation)
            m_scr[rows, :] = m_tile
            d_scr[rows, :] = d_tile

    def first_unit(c, carry):
        unit(c, 0, bias_t_first,
             lambda hs: jnp.concatenate([kp_ref[c, :, hs], k_ref[c, pl.ds(0, n), hs]], axis=0),
             lambda hs: jnp.concatenate([vp_ref[c, :, hs], v_ref[c, pl.ds(0, n), hs]], axis=0))
        return carry

    def later_unit(u, carry):
        c = u // (nj - 1)
        j = 1 + u % (nj - 1)
        r0 = pl.multiple_of((j - 1) * n, n)
        unit(c, j, bias_t,
             lambda hs: k_ref[c, pl.ds(r0, 2 * n), hs],
             lambda hs: v_ref[c, pl.ds(r0, 2 * n), hs])
        return carry

    lax.fori_loop(0, dilation, first_unit, 0, unroll=min(dilation, DIL_UNROLL))
    if nj > 1:
        lax.fori_loop(0, dilation * (nj - 1), later_unit, 0, unroll=DIL_UNROLL)
    if dilation > 1:
        for h in range(N_HEADS):
            o_ref[:, h * DIL_HD:(h + 1) * DIL_HD] = o_scr[h].astype(o_ref.dtype)
        m_ref[...] = m_scr[...]
        d_ref[...] = d_scr[...]


def _dil(pg, dilation, batch, seq):
    t = batch * seq
    blk = DIL_BLOCK
    nblk = seq // blk
    rows = blk // dilation
    prev_per_blk = rows // DIL_SPAN

    def cur(part):
        return pl.BlockSpec((None, dilation, rows, BRANCH_W), lambda b, i: (b, 0, i, part))

    def prev(part):
        return pl.BlockSpec((None, dilation, DIL_SPAN, BRANCH_W),
                            lambda b, i: (b, 0, jnp.maximum(i * prev_per_blk - 1, 0), part))

    return pl.pallas_call(
        functools.partial(_dil_kernel, dilation=dilation),
        grid=(batch, nblk),
        in_specs=[cur(0), cur(1), cur(2), prev(1), prev(2)],
        out_specs=[pl.BlockSpec((blk, BRANCH_W), lambda b, i: (b * nblk + i, 0)),
                   pl.BlockSpec((blk, LANES), lambda b, i: (b * nblk + i, 0)),
                   pl.BlockSpec((blk, LANES), lambda b, i: (b * nblk + i, 0))],
        out_shape=[jax.ShapeDtypeStruct((t, BRANCH_W), BF16), jax.ShapeDtypeStruct((t, LANES), F32),
                   jax.ShapeDtypeStruct((t, LANES), F32)],
        scratch_shapes=([] if dilation == 1 else
                        [pltpu.VMEM((N_HEADS, blk, DIL_HD), F32), pltpu.VMEM((blk, LANES), F32),
                         pltpu.VMEM((blk, LANES), F32)]),
        compiler_params=pltpu.CompilerParams(dimension_semantics=("parallel", "arbitrary"),
                                             vmem_limit_bytes=VMEM_LIMIT),
        name=f"dilated_attn_{dilation}",
    )(pg, pg, pg, pg, pg)


def _layer_norm(v, g, b):
    mu = jnp.mean(v, axis=-1, keepdims=True)
    d = v - mu
    return d * lax.rsqrt(jnp.mean(d * d, axis=-1, keepdims=True) + 1e-5) * g + b


def _merge_kernel(x_ref, oa_ref, ob_ref, o1_ref, o2_ref, o3_ref, m1_ref, m2_ref, m3_ref, d1_ref, d2_ref, d3_ref,
                  wg_ref, bg_ref, wb_ref, wo_ref, lg_ref, lb_ref, y_ref, *, alpha):
    tm, d = x_ref.shape
    lanes_per_head = LANES // N_HEADS
    for r0 in range(0, tm, tm // MERGE_ROW_SPLIT):
        rows = slice(r0, r0 + tm // MERGE_ROW_SPLIT)
        m1, m2, m3 = m1_ref[rows, :], m2_ref[rows, :], m3_ref[rows, :]
        m = jnp.maximum(jnp.maximum(m1, m2), m3)
        e1, e2, e3 = jnp.exp2(m1 - m), jnp.exp2(m2 - m), jnp.exp2(m3 - m)
        tot = e1 * d1_ref[rows, :] + e2 * d2_ref[rows, :] + e3 * d3_ref[rows, :]
        w1, w2, w3 = e1 / tot, e2 / tot, e3 / tot
        parts = []
        for h in range(N_HEADS):
            hs = slice(h * DIL_HD, (h + 1) * DIL_HD)
            ls = slice(h * lanes_per_head, h * lanes_per_head + 1)
            parts.append(w1[:, ls] * o1_ref[rows, hs].astype(F32) + w2[:, ls] * o2_ref[rows, hs].astype(F32)
                         + w3[:, ls] * o3_ref[rows, hs].astype(F32))
        oc = jnp.concatenate(parts, axis=-1).astype(BF16)
        x = x_ref[rows, :]
        xb = x.astype(BF16)
        merged = None
        for j, o in enumerate((oa_ref[rows, :], ob_ref[rows, :], oc)):
            gate = _sigmoid(jnp.dot(xb, wg_ref[:, j * d:(j + 1) * d], preferred_element_type=F32)
                            + bg_ref[:, j * d:(j + 1) * d])
            term = gate * jnp.dot(o, wb_ref[j], preferred_element_type=F32)
            merged = term if merged is None else merged + term
        hmix = jnp.dot(merged.astype(BF16), wo_ref[...], preferred_element_type=F32)
        y_ref[rows, :] = _layer_norm(alpha * x + hmix, lg_ref[...], lb_ref[...])


def _merge(x, oa, ob, o_dil, m_dil, d_dil, w_packed, b_gate, w_branch, w_out, ln_g, ln_b, layer, alpha, tm):
    t, d = x.shape

    def rows(width):
        return pl.BlockSpec((tm, width), lambda i: (i, 0))

    params = (b_gate, w_branch, w_out, ln_g, ln_b)
    return pl.pallas_call(
        functools.partial(_merge_kernel, alpha=alpha),
        grid=(t // tm,),
        in_specs=[rows(d)] + [rows(BRANCH_W)] * 5 + [rows(LANES)] * 6
                 + [_cols(w_packed, layer, 3 * d, W_GATE_COL)] + [_layer(p, layer) for p in params],
        out_specs=rows(d),
        out_shape=jax.ShapeDtypeStruct((t, d), F32),
        compiler_params=pltpu.CompilerParams(dimension_semantics=("parallel",), vmem_limit_bytes=VMEM_LIMIT),
        name="merge_out_ln",
    )(x, oa, ob, *o_dil, *m_dil, *d_dil, w_packed, *params)


def _ffn_kernel(x_ref, wg_ref, wu_ref, wd_ref, lg_ref, lb_ref, y_ref, act_ref, *, alpha):
    d_ff = wg_ref.shape[1]
    tm = x_ref.shape[0]
    for r0 in range(0, tm, tm // FF_ROW_SPLIT):
        rows = slice(r0, r0 + tm // FF_ROW_SPLIT)
        x = x_ref[rows, :]
        xb = x.astype(BF16)
        for j0 in range(0, d_ff, FF_CHUNK):
            cs = slice(j0, min(j0 + FF_CHUNK, d_ff))
            g = jnp.dot(xb, wg_ref[:, cs], preferred_element_type=F32)
            u = jnp.dot(xb, wu_ref[:, cs], preferred_element_type=F32)
            act_ref[rows, cs] = (g * _sigmoid(g) * u).astype(BF16)
        f = jnp.dot(act_ref[rows, :], wd_ref[...], preferred_element_type=F32)
        y_ref[rows, :] = _layer_norm(alpha * x + f, lg_ref[...], lb_ref[...])


def _ffn(x, w_gate, w_up, w_down, ln_g, ln_b, layer, alpha, tm):
    t, d = x.shape
    d_ff = w_gate.shape[2]
    rows = pl.BlockSpec((tm, d), lambda i: (i, 0))
    params = (w_gate, w_up, w_down, ln_g, ln_b)
    return pl.pallas_call(
        functools.partial(_ffn_kernel, alpha=alpha),
        grid=(t // tm,),
        in_specs=[rows] + [_layer(p, layer) for p in params],
        out_specs=rows,
        out_shape=jax.ShapeDtypeStruct((t, d), F32),
        scratch_shapes=[pltpu.VMEM((tm, d_ff), BF16)],
        compiler_params=pltpu.CompilerParams(dimension_semantics=("parallel",), vmem_limit_bytes=VMEM_LIMIT),
        name="swiglu_ln",
    )(x, *params)


def _rope_tables(seq):
    pos = jnp.arange(seq, dtype=F32)

    def tables(n_rot, base, width):
        half = n_rot // 2
        inv = base ** (-jnp.arange(half, dtype=F32) * 2.0 / n_rot)
        ang = pos[:, None] * inv[None, :]
        cos = jnp.concatenate([jnp.cos(ang), jnp.cos(ang), jnp.ones((seq, width - n_rot), F32)], axis=1)
        sin = jnp.concatenate([-jnp.sin(ang), jnp.sin(ang), jnp.zeros((seq, width - n_rot), F32)], axis=1)
        return cos, sin

    rc, rs = tables(RET_DK, RET_ROPE_BASE, RET_DK)
    dc, ds = tables(ROPE_DIMS, ROPE_THETA, DIL_HD)
    dil_tabs = []
    for _, dil in DIL_GROUPS:
        for tab in (dc, ds):
            dil_tabs.append(tab.reshape(seq // dil, dil, DIL_HD).transpose(1, 0, 2))
    return (jnp.tile(rc, (1, N_HEADS)), jnp.tile(rs, (1, N_HEADS))), dil_tabs


def _pack_w_in(w_in):
    d = w_in.shape[1]
    widths = (256, 256, 512, 512, GLA_RANK, 256, 256, 512, 512, 1536, 1536, 1536, 3 * d)
    ga0 = sum(widths[:4])
    dil0 = sum(widths[:9])
    gate0 = sum(widths[:12])
    w = w_in.astype(BF16)
    packed = jnp.concatenate(
        [w[:, :, :ga0], w[:, :, ga0 + GLA_RANK:dil0], w[:, :, gate0:], w[:, :, dil0:gate0],
         jnp.pad(w[:, :, ga0:ga0 + GLA_RANK], ((0, 0), (0, 0), (0, GA_PAD - GLA_RANK)))], axis=2)
    assert packed.shape[2] == W_GA_COL + GA_PAD
    return packed


def kernel(x, w_in, w_gla_a2, b_gla_a, gla_norm_g, ret_norm_g, w_branch, b_gate, w_out, ln1_g, ln1_b,
           w_ffn_gate, w_ffn_up, w_ffn_down, ln2_g, ln2_b):
    batch, seq, d = x.shape
    depth = w_in.shape[0]
    alpha = (2 * depth) ** 0.25
    ret_tabs, dil_tabs = _rope_tables(seq)
    w_packed = _pack_w_in(w_in)
    w_a2 = jnp.pad(w_gla_a2, ((0, 0), (0, GA_PAD - GLA_RANK), (0, 0))).astype(BF16)
    w_branch, w_out = w_branch.astype(BF16), w_out.astype(BF16)
    w_ffn_gate, w_ffn_up, w_ffn_down = w_ffn_gate.astype(BF16), w_ffn_up.astype(BF16), w_ffn_down.astype(BF16)
    b_gla_a, gla_norm_g, ret_norm_g, b_gate, ln1_g, ln1_b, ln2_g, ln2_b = (
        p[:, None, :] for p in (b_gla_a, gla_norm_g, ret_norm_g, b_gate, ln1_g, ln1_b, ln2_g, ln2_b))
    xf = x.reshape(batch * seq, d)
    for l in range(depth):
        oa, ob, *p_dil = _proj(xf, w_packed, (w_a2, b_gla_a, gla_norm_g, ret_norm_g), l, ret_tabs, dil_tabs,
                               batch, seq)
        o_dil, m_dil, d_dil = zip(*(_dil(pg, dilation, batch, seq) for pg, (_, dilation) in zip(p_dil, DIL_GROUPS)))
        xf = _merge(xf, oa, ob, o_dil, m_dil, d_dil, w_packed, b_gate, w_branch, w_out, ln1_g, ln1_b, l, alpha,
                    MERGE_ROW_TILE)
        xf = _ffn(xf, w_ffn_gate, w_ffn_up, w_ffn_down, ln2_g, ln2_b, l, alpha, FF_ROW_TILE)
    return xf.reshape(batch, seq, d)
```

```python
import functools
import math

import jax
import jax.numpy as jnp
import numpy as np
from jax import lax
from jax.experimental import pallas as pl
from jax.experimental.pallas import tpu as pltpu

F32 = jnp.float32
BF16 = jnp.bfloat16

LANES = 128
N_HEADS = 4
GLA_DK = 64
GLA_DV = 128
GLA_RANK = 16
GLA_TAU = 16.0
GLA_CHUNK = 64
GLA_GROUP = 256
RET_DK = 64
RET_DV = 128
RET_CHUNK = 128
RET_GROUP = 256
RET_ROPE_BASE = 10000.0
DIL_HD = 128
DIL_GROUPS = ((128, 1), (512, 4), (2048, 16))
DIL_SPAN = 128
DIL_BLOCK = DIL_SPAN * max(d for _, d in DIL_GROUPS)
ROPE_THETA = 500000.0
ROPE_DIMS = 32
BRANCH_W = 512
GA_PAD = 256
GA_COL = 3072
RET_QK_COL = 1536
P1_GLA_COLS = (0, 256, 512, 1024)
P1_RET_COLS = (1536, 1792, 2048, 2560)
W_MAIN_COL = 0
W_GATE_COL = 3072
W_DIL_COL = 6144
W_GA_COL = 10752
MASK_VALUE = -1e30

VMEM_LIMIT = 56 * 1024 * 1024
SEQ_BLOCK = 512
FF_CHUNK = 256
FF_ROW_TILE = 1024
FF_ROW_SPLIT = 4
MERGE_ROW_TILE = 1024
MERGE_ROW_SPLIT = 4
DIL_UNROLL = 4


def _sigmoid(x):
    return 1.0 / (1.0 + jnp.exp(-x))


def _nt_dot(a, b):
    return lax.dot_general(a, b, (((1,), (1,)), ((), ())), preferred_element_type=F32)


def _tn_dot(a, b):
    return lax.dot_general(a, b, (((0,), (0,)), ((), ())), preferred_element_type=F32)


def _layer(stacked, layer):
    rest = stacked.shape[1:]
    return pl.BlockSpec((None,) + rest, lambda *_: (layer,) + (0,) * len(rest), pipeline_mode=pl.Buffered(1))


def _rotary(x, cos, sin, head_dim, half):
    width = x.shape[1]
    low = (lax.broadcasted_iota(jnp.int32, x.shape, 1) % head_dim) < half
    partner = jnp.where(low, pltpu.roll(x, width - half, 1), pltpu.roll(x, half, 1))
    return x * cos + partner * sin


def _proj_kernel(*refs):
    n_groups = len(DIL_GROUPS)
    x_ref, wm_ref, wq_ref, wk_ref, wv_ref, wa_ref, rc_ref, rs_ref = refs[:8]
    tabs = refs[8:8 + 2 * n_groups]
    wa2_ref, ba_ref, gg_ref, rg_ref = refs[8 + 2 * n_groups:12 + 2 * n_groups]
    oa_ref, ob_ref = refs[12 + 2 * n_groups:14 + 2 * n_groups]
    g_refs = refs[14 + 2 * n_groups:14 + 3 * n_groups]
    (slab_ref, slab4_ref, xp4_ref, xp16_ref, p1_ref,
     gst_ref, gqs_ref, gkd_ref, gacc_ref, rst_ref, rqd_ref, rkd_ref, racc_ref) = refs[14 + 3 * n_groups:]
    tm, k = x_ref.shape

    @pl.when(pl.program_id(1) == 0)
    def _():
        gst_ref[...] = jnp.zeros_like(gst_ref)
        rst_ref[...] = jnp.zeros_like(rst_ref)

    n_slabs = k // LANES
    tn = BRANCH_W
    x = x_ref[...].astype(BF16)
    lane = lax.broadcasted_iota(jnp.int32, (1, tn), 1)
    for j0 in range(0, GA_COL, tn):
        res = jnp.dot(x, wm_ref[:, j0:j0 + tn], preferred_element_type=F32)
        if j0 == 0:
            res = res * jnp.where(lane < N_HEADS * GLA_DK, GLA_DK ** -0.5, 1.0)
        if j0 == RET_QK_COL:
            cos = jnp.concatenate([rc_ref[...]] * (tn // LANES), axis=1)
            sin = jnp.concatenate([rs_ref[...]] * (tn // LANES), axis=1)
            res = _rotary(res, cos, sin, RET_DK, RET_DK // 2)
            res = res * jnp.where(lane < N_HEADS * RET_DK, 1.0, RET_DK ** -0.5)
        p1_ref[:, j0:j0 + tn] = res.astype(BF16)
    p1_ref[:, GA_COL:] = jnp.dot(x, wa_ref[...], preferred_element_type=F32).astype(BF16)
    _gla_body(p1_ref, wa2_ref, ba_ref, gg_ref, oa_ref, gst_ref, gqs_ref, gkd_ref, gacc_ref)
    _ret_body(p1_ref, rg_ref, ob_ref, rst_ref, rqd_ref, rkd_ref, racc_ref)
    q4 = tm // 4
    q16 = tm // 16
    for j in range(n_slabs):
        slab_ref[j] = x_ref[:, j * LANES:(j + 1) * LANES]
    for c in range(4):
        for j in range(n_slabs):
            part = slab_ref[j, pl.ds(c, q4, stride=4), :]
            slab4_ref[j, c * q4:(c + 1) * q4, :] = part
            xp4_ref[c * q4:(c + 1) * q4, j * LANES:(j + 1) * LANES] = part.astype(BF16)
    for c0 in range(4):
        for c1 in range(4):
            c = 4 * c1 + c0
            for j in range(n_slabs):
                xp16_ref[c * q16:(c + 1) * q16, j * LANES:(j + 1) * LANES] = (
                    slab4_ref[j, pl.ds(c0 * q4 + c1, q16, stride=4), :].astype(BF16))
    lhs_of = {1: lambda: x, 4: lambda: xp4_ref[...], 16: lambda: xp16_ref[...]}
    for g, (_, dil) in enumerate(DIL_GROUPS):
        rows = tm // dil
        lhs = lhs_of[dil]()
        cos = jnp.concatenate([tabs[2 * g][...].reshape(tm, LANES)] * N_HEADS, axis=1)
        sin = jnp.concatenate([tabs[2 * g + 1][...].reshape(tm, LANES)] * N_HEADS, axis=1)
        for part, w_ref in enumerate((wq_ref, wk_ref, wv_ref)):
            res = jnp.dot(lhs, w_ref[:, g * tn:(g + 1) * tn], preferred_element_type=F32)
            if part < 2:
                res = _rotary(res, cos, sin, DIL_HD, ROPE_DIMS // 2)
            if part == 0:
                res = res * (DIL_HD ** -0.5 * math.log2(math.e))
            g_refs[g][:, :, part * tn:(part + 1) * tn] = res.astype(BF16).reshape(dil, rows, tn)


def _cols(w, layer, width, offset):
    assert offset % width == 0
    return pl.BlockSpec((None, w.shape[1], width), lambda *_: (layer, 0, offset // width),
                        pipeline_mode=pl.Buffered(1))


def _proj(x, w, mixer_params, layer, ret_tabs, dil_tabs, batch, seq):
    assert tuple(d for _, d in DIL_GROUPS) == (1, 4, 16)
    t, k = x.shape
    tm = SEQ_BLOCK
    nblk = seq // tm
    w_specs = [_cols(w, layer, GA_COL, W_MAIN_COL)]
    w_specs += [_cols(w, layer, 3 * BRANCH_W, W_DIL_COL + part * 3 * BRANCH_W) for part in range(3)]
    w_specs += [_cols(w, layer, GA_PAD, W_GA_COL)]

    def row(b, i):
        return (b * nblk + i, 0)

    tab_specs, out_specs, out_shapes = [], [], []
    for _, dil in DIL_GROUPS:
        tab_specs += [pl.BlockSpec((dil, tm // dil, LANES), lambda b, i: (0, i, 0))] * 2
        out_specs.append(pl.BlockSpec((None, dil, tm // dil, 3 * BRANCH_W), lambda b, i: (b, 0, i, 0)))
        out_shapes.append(jax.ShapeDtypeStruct((batch, dil, seq // dil, 3 * BRANCH_W), BF16))
    ret_spec = pl.BlockSpec((tm, LANES), lambda b, i: (i, 0))
    branch = pl.BlockSpec((tm, BRANCH_W), row)
    state = pltpu.VMEM((N_HEADS * GLA_DV, N_HEADS * GLA_DK), F32)
    decayed = pltpu.VMEM((tm, N_HEADS * GLA_DK), BF16)
    acc = pltpu.VMEM((tm, BRANCH_W), F32)
    return pl.pallas_call(
        _proj_kernel,
        grid=(batch, nblk),
        in_specs=[pl.BlockSpec((tm, k), row)] + w_specs + [ret_spec, ret_spec] + tab_specs
                 + [_layer(p, layer) for p in mixer_params],
        out_specs=[branch, branch] + out_specs,
        out_shape=[jax.ShapeDtypeStruct((t, BRANCH_W), BF16)] * 2 + out_shapes,
        scratch_shapes=[pltpu.VMEM((k // LANES, tm, LANES), F32), pltpu.VMEM((k // LANES, tm, LANES), F32),
                        pltpu.VMEM((tm, k), BF16), pltpu.VMEM((tm, k), BF16),
                        pltpu.VMEM((tm, GA_COL + GA_PAD), BF16),
                        state, decayed, decayed, acc, state, decayed, decayed, acc],
        compiler_params=pltpu.CompilerParams(dimension_semantics=("parallel", "arbitrary"),
                                             vmem_limit_bytes=VMEM_LIMIT),
        name="in_proj_mixers",
    )(x, *([w] * 5), *ret_tabs, *dil_tabs, *mixer_params)


def _state_step(p1_ref, rows, v_col, st_ref, q_ref, kd_ref, acc_ref, decay, st_diag):
    st = st_ref[...]
    acc_ref[rows, :] += _nt_dot(q_ref[rows, :], st.astype(BF16))
    kv_t = _tn_dot(p1_ref[rows, v_col:v_col + BRANCH_W], kd_ref[rows, :])
    st_ref[...] = st * decay + jnp.where(st_diag, kv_t, 0.0)


def _gla_body(p1_ref, wa_ref, ba_ref, g_ref, o_ref, st_ref, qs_ref, kd_ref, acc_ref):
    c = GLA_CHUNK
    grp = GLA_GROUP
    tb = p1_ref.shape[0]
    width = N_HEADS * GLA_DK
    q_col, k_col, v_col, r_col = P1_GLA_COLS

    z = jnp.dot(p1_ref[:, GA_COL:GA_COL + GA_PAD], wa_ref[...], preferred_element_type=F32) + ba_ref[...]
    log_a = (jnp.minimum(z, 0.0) - jnp.log1p(jnp.exp(-jnp.abs(z)))) * (1.0 / GLA_TAU)
    hi = log_a.astype(BF16)
    lo = (log_a - hi.astype(F32)).astype(BF16)
    hl = jnp.concatenate([hi, lo], axis=1)

    row = lax.broadcasted_iota(jnp.int32, (grp, grp), 0)
    col = lax.broadcasted_iota(jnp.int32, (grp, grp), 1)
    same_chunk = (row // c) == (col // c)
    causal = same_chunk & (row >= col)
    cum_and_total = jnp.concatenate([jnp.where(causal, 1.0, 0.0), jnp.where(same_chunk, 1.0, 0.0)],
                                    axis=0).astype(BF16)
    lane_head = lax.broadcasted_iota(jnp.int32, (grp, width), 1) // GLA_DK
    st_diag = (lax.broadcasted_iota(jnp.int32, st_ref.shape, 0) // GLA_DV
               == lax.broadcasted_iota(jnp.int32, st_ref.shape, 1) // GLA_DK)

    decays = []
    for g0 in range(0, tb, grp):
        rows = slice(g0, g0 + grp)
        r2 = jnp.dot(cum_and_total, hl[rows], preferred_element_type=F32)
        b = r2[:grp, :width] + r2[:grp, width:]
        b_tot = r2[grp:, :width] + r2[grp:, width:]
        q = p1_ref[rows, q_col:q_col + width].astype(F32)
        k = p1_ref[rows, k_col:k_col + width].astype(F32)
        q_s = (q * jnp.exp(b)).astype(BF16)
        k_s = k * jnp.exp(-b)
        qs_ref[rows, :] = q_s
        kd_ref[rows, :] = (k * jnp.exp(b_tot - b)).astype(BF16)
        for n0 in range(0, grp, c):
            decays.append(jnp.exp(b_tot[n0:n0 + 1, :]))
        for h in range(N_HEADS):
            hs = slice(h * GLA_DV, (h + 1) * GLA_DV)
            k_h = jnp.where(lane_head == h, k_s, 0.0).astype(BF16)
            att = jnp.where(causal, _nt_dot(q_s, k_h), 0.0).astype(BF16)
            v_h = p1_ref[rows, v_col + h * GLA_DV:v_col + (h + 1) * GLA_DV]
            acc_ref[rows, hs] = jnp.dot(att, v_h, preferred_element_type=F32)

    for n, decay in enumerate(decays):
        rows = slice(n * c, (n + 1) * c)
        _state_step(p1_ref, rows, v_col, st_ref, qs_ref, kd_ref, acc_ref, decay, st_diag)

    outs = []
    for h in range(N_HEADS):
        o = acc_ref[:, h * GLA_DV:(h + 1) * GLA_DV]
        outs.append(o * lax.rsqrt(jnp.mean(o * o, axis=-1, keepdims=True) + 1e-6))
    y = jnp.concatenate(outs, axis=-1) * g_ref[...]
    r = p1_ref[:, r_col:r_col + BRANCH_W].astype(F32)
    o_ref[...] = (y * (r * _sigmoid(r))).astype(o_ref.dtype)


def _ret_body(p1_ref, g_ref, o_ref, st_ref, qd_ref, kd_ref, acc_ref):
    c = RET_CHUNK
    grp = RET_GROUP
    tb = p1_ref.shape[0]
    width = N_HEADS * RET_DK
    q_col, k_col, v_col, r_col = P1_RET_COLS

    log_g = [math.log1p(-(2.0 ** (-5.0 - h))) for h in range(N_HEADS)]
    lane_head = lax.broadcasted_iota(jnp.int32, (grp, width), 1) // RET_DK
    lg_lane = jnp.zeros((grp, width), F32)
    for h in range(N_HEADS):
        lg_lane = jnp.where(lane_head == h, log_g[h], lg_lane)
    ridx = (lax.broadcasted_iota(jnp.int32, (grp, width), 0) % c).astype(F32)
    q_scale = jnp.exp((ridx + 1.0) * lg_lane)
    k_scale = jnp.exp((c - 1.0 - ridx) * lg_lane)
    chunk_decay = jnp.exp(c * lg_lane[0:1, :])
    row = lax.broadcasted_iota(jnp.int32, (grp, grp), 0)
    col = lax.broadcasted_iota(jnp.int32, (grp, grp), 1)
    causal = ((row // c) == (col // c)) & (row >= col)
    diff = (row - col).astype(F32)
    st_diag = (lax.broadcasted_iota(jnp.int32, st_ref.shape, 0) // RET_DV
               == lax.broadcasted_iota(jnp.int32, st_ref.shape, 1) // RET_DK)

    for g0 in range(0, tb, grp):
        rows = slice(g0, g0 + grp)
        qb = p1_ref[rows, q_col:q_col + width]
        q = qb.astype(F32)
        k = p1_ref[rows, k_col:k_col + width].astype(F32)
        qd_ref[rows, :] = (q * q_scale).astype(BF16)
        kd_ref[rows, :] = (k * k_scale).astype(BF16)
        for h in range(N_HEADS):
            hs = slice(h * RET_DV, (h + 1) * RET_DV)
            dmat = jnp.where(causal, jnp.exp(jnp.maximum(diff, 0.0) * log_g[h]), 0.0)
            k_h = jnp.where(lane_head == h, k, 0.0).astype(BF16)
            att = (_nt_dot(qb, k_h) * dmat).astype(BF16)
            v_h = p1_ref[rows, v_col + h * RET_DV:v_col + (h + 1) * RET_DV]
            acc_ref[rows, hs] = jnp.dot(att, v_h, preferred_element_type=F32)

    for n in range(tb // c):
        rows = slice(n * c, (n + 1) * c)
        _state_step(p1_ref, rows, v_col, st_ref, qd_ref, kd_ref, acc_ref, chunk_decay, st_diag)

    outs = []
    for h in range(N_HEADS):
        o = acc_ref[:, h * RET_DV:(h + 1) * RET_DV]
        mu = jnp.mean(o, axis=-1, keepdims=True)
        d = o - mu
        outs.append(d * lax.rsqrt(jnp.mean(d * d, axis=-1, keepdims=True) + 1e-5))
    y = jnp.concatenate(outs, axis=-1) * g_ref[...]
    r = p1_ref[:, r_col:r_col + BRANCH_W].astype(F32)
    o_ref[...] = (y * (r * _sigmoid(r))).astype(o_ref.dtype)


def _dil_kernel(q_ref, k_ref, v_ref, kp_ref, vp_ref, o_ref, l_ref, o_scr, l_scr, *, dilation):
    n = DIL_SPAN
    nj = q_ref.shape[1] // n
    qry = lax.broadcasted_iota(jnp.int32, (n, 2 * n), 0)
    key = lax.broadcasted_iota(jnp.int32, (n, 2 * n), 1)
    dist = qry + n - key
    bias_t = (dist >= 0) & (dist <= n)
    bias_t_first = bias_t & ((key >= n) | (pl.program_id(1) > 0))
    ones = jnp.ones((2 * n, n), BF16)
    lane_head = lax.broadcasted_iota(jnp.int32, (n, LANES), 1) // (LANES // N_HEADS)

    def unit(c, j, bias, keys, values):
        start = j * (n * dilation) + c
        dst = pl.ds(start, n, stride=dilation) if dilation > 1 else pl.ds(start, n)
        q = q_ref[c, pl.ds(j * n, n), :]
        lse_tile = jnp.zeros((n, LANES), F32)
        for h in range(N_HEADS):
            hs = slice(h * DIL_HD, (h + 1) * DIL_HD)
            s = jnp.where(bias, _nt_dot(q[:, hs], keys(hs)), MASK_VALUE)
            m = jnp.max(s, axis=-1, keepdims=True)
            p = jnp.exp2(s - m)
            o_den = jnp.dot(p.astype(BF16), jnp.concatenate([values(hs), ones], axis=1),
                            preferred_element_type=F32)
            den = o_den[:, DIL_HD:]
            o_scr[h, dst, :] = o_den[:, :DIL_HD] / den
            lse_tile = jnp.where(lane_head == h, m * math.log(2.0) + jnp.log(den), lse_tile)
        l_scr[dst, :] = lse_tile

    def first_unit(c, carry):
        unit(c, 0, bias_t_first,
             lambda hs: jnp.concatenate([kp_ref[c, :, hs], k_ref[c, pl.ds(0, n), hs]], axis=0),
             lambda hs: jnp.concatenate([vp_ref[c, :, hs], v_ref[c, pl.ds(0, n), hs]], axis=0))
        return carry

    def later_unit(u, carry):
        c = u // (nj - 1)
        j = 1 + u % (nj - 1)
        r0 = pl.multiple_of((j - 1) * n, n)
        unit(c, j, bias_t,
             lambda hs: k_ref[c, pl.ds(r0, 2 * n), hs],
             lambda hs: v_ref[c, pl.ds(r0, 2 * n), hs])
        return carry

    lax.fori_loop(0, dilation, first_unit, 0, unroll=min(dilation, DIL_UNROLL))
    if nj > 1:
        lax.fori_loop(0, dilation * (nj - 1), later_unit, 0, unroll=DIL_UNROLL)
    for h in range(N_HEADS):
        o_ref[:, h * DIL_HD:(h + 1) * DIL_HD] = o_scr[h].astype(o_ref.dtype)
    l_ref[...] = l_scr[...]


def _dil(pg, dilation, batch, seq):
    t = batch * seq
    blk = DIL_BLOCK
    nblk = seq // blk
    rows = blk // dilation
    prev_per_blk = rows // DIL_SPAN

    def cur(part):
        return pl.BlockSpec((None, dilation, rows, BRANCH_W), lambda b, i: (b, 0, i, part))

    def prev(part):
        return pl.BlockSpec((None, dilation, DIL_SPAN, BRANCH_W),
                            lambda b, i: (b, 0, jnp.maximum(i * prev_per_blk - 1, 0), part))

    return pl.pallas_call(
        functools.partial(_dil_kernel, dilation=dilation),
        grid=(batch, nblk),
        in_specs=[cur(0), cur(1), cur(2), prev(1), prev(2)],
        out_specs=[pl.BlockSpec((blk, BRANCH_W), lambda b, i: (b * nblk + i, 0)),
                   pl.BlockSpec((blk, LANES), lambda b, i: (b * nblk + i, 0))],
        out_shape=[jax.ShapeDtypeStruct((t, BRANCH_W), BF16), jax.ShapeDtypeStruct((t, LANES), F32)],
        scratch_shapes=[pltpu.VMEM((N_HEADS, blk, DIL_HD), F32), pltpu.VMEM((blk, LANES), F32)],
        compiler_params=pltpu.CompilerParams(dimension_semantics=("parallel", "arbitrary"),
                                             vmem_limit_bytes=VMEM_LIMIT),
        name=f"dilated_attn_{dilation}",
    )(pg, pg, pg, pg, pg)


def _layer_norm(v, g, b):
    mu = jnp.mean(v, axis=-1, keepdims=True)
    d = v - mu
    return d * lax.rsqrt(jnp.mean(d * d, axis=-1, keepdims=True) + 1e-5) * g + b


def _merge_kernel(x_ref, oa_ref, ob_ref, o1_ref, o2_ref, o3_ref, l1_ref, l2_ref, l3_ref,
                  wg_ref, bg_ref, wb_ref, wo_ref, lg_ref, lb_ref, y_ref, *, alpha):
    tm, d = x_ref.shape
    lanes_per_head = LANES // N_HEADS
    for r0 in range(0, tm, tm // MERGE_ROW_SPLIT):
        rows = slice(r0, r0 + tm // MERGE_ROW_SPLIT)
        l1, l2, l3 = l1_ref[rows, :], l2_ref[rows, :], l3_ref[rows, :]
        m = jnp.maximum(jnp.maximum(l1, l2), l3)
        e1, e2, e3 = jnp.exp(l1 - m), jnp.exp(l2 - m), jnp.exp(l3 - m)
        tot = e1 + e2 + e3
        w1, w2, w3 = e1 / tot, e2 / tot, e3 / tot
        parts = []
        for h in range(N_HEADS):
            hs = slice(h * DIL_HD, (h + 1) * DIL_HD)
            ls = slice(h * lanes_per_head, h * lanes_per_head + 1)
            parts.append(w1[:, ls] * o1_ref[rows, hs].astype(F32) + w2[:, ls] * o2_ref[rows, hs].astype(F32)
                         + w3[:, ls] * o3_ref[rows, hs].astype(F32))
        oc = jnp.concatenate(parts, axis=-1).astype(BF16)
        x = x_ref[rows, :]
        xb = x.astype(BF16)
        merged = None
        for j, o in enumerate((oa_ref[rows, :], ob_ref[rows, :], oc)):
            gate = _sigmoid(jnp.dot(xb, wg_ref[:, j * d:(j + 1) * d], preferred_element_type=F32)
                            + bg_ref[:, j * d:(j + 1) * d])
            term = gate * jnp.dot(o, wb_ref[j], preferred_element_type=F32)
            merged = term if merged is None else merged + term
        hmix = jnp.dot(merged.astype(BF16), wo_ref[...], preferred_element_type=F32)
        y_ref[rows, :] = _layer_norm(alpha * x + hmix, lg_ref[...], lb_ref[...])


def _merge(x, oa, ob, o_dil, l_dil, w_packed, b_gate, w_branch, w_out, ln_g, ln_b, layer, alpha, tm):
    t, d = x.shape

    def rows(width):
        return pl.BlockSpec((tm, width), lambda i: (i, 0))

    params = (b_gate, w_branch, w_out, ln_g, ln_b)
    return pl.pallas_call(
        functools.partial(_merge_kernel, alpha=alpha),
        grid=(t // tm,),
        in_specs=[rows(d)] + [rows(BRANCH_W)] * 5 + [rows(LANES)] * 3
                 + [_cols(w_packed, layer, 3 * d, W_GATE_COL)] + [_layer(p, layer) for p in params],
        out_specs=rows(d),
        out_shape=jax.ShapeDtypeStruct((t, d), F32),
        compiler_params=pltpu.CompilerParams(dimension_semantics=("parallel",), vmem_limit_bytes=VMEM_LIMIT),
        name="merge_out_ln",
    )(x, oa, ob, *o_dil, *l_dil, w_packed, *params)


def _ffn_kernel(x_ref, wg_ref, wu_ref, wd_ref, lg_ref, lb_ref, y_ref, act_ref, *, alpha):
    d_ff = wg_ref.shape[1]
    tm = x_ref.shape[0]
    for r0 in range(0, tm, tm // FF_ROW_SPLIT):
        rows = slice(r0, r0 + tm // FF_ROW_SPLIT)
        x = x_ref[rows, :]
        xb = x.astype(BF16)
        for j0 in range(0, d_ff, FF_CHUNK):
            cs = slice(j0, min(j0 + FF_CHUNK, d_ff))
            g = jnp.dot(xb, wg_ref[:, cs], preferred_element_type=F32)
            u = jnp.dot(xb, wu_ref[:, cs], preferred_element_type=F32)
            act_ref[rows, cs] = (g * _sigmoid(g) * u).astype(BF16)
        f = jnp.dot(act_ref[rows, :], wd_ref[...], preferred_element_type=F32)
        y_ref[rows, :] = _layer_norm(alpha * x + f, lg_ref[...], lb_ref[...])


def _ffn(x, w_gate, w_up, w_down, ln_g, ln_b, layer, alpha, tm):
    t, d = x.shape
    d_ff = w_gate.shape[2]
    rows = pl.BlockSpec((tm, d), lambda i: (i, 0))
    params = (w_gate, w_up, w_down, ln_g, ln_b)
    return pl.pallas_call(
        functools.partial(_ffn_kernel, alpha=alpha),
        grid=(t // tm,),
        in_specs=[rows] + [_layer(p, layer) for p in params],
        out_specs=rows,
        out_shape=jax.ShapeDtypeStruct((t, d), F32),
        scratch_shapes=[pltpu.VMEM((tm, d_ff), BF16)],
        compiler_params=pltpu.CompilerParams(dimension_semantics=("parallel",), vmem_limit_bytes=VMEM_LIMIT),
        name="swiglu_ln",
    )(x, *params)


def _rope_tables(seq):
    pos = np.arange(seq, dtype=np.float32)

    def tables(n_rot, base, width):
        half = n_rot // 2
        inv = np.float32(base) ** (-np.arange(half, dtype=np.float32) * np.float32(2.0) / np.float32(n_rot))
        ang = (pos[:, None] * inv[None, :]).astype(np.float32)
        cos = np.concatenate([np.cos(ang), np.cos(ang), np.ones((seq, width - n_rot), np.float32)], axis=1)
        sin = np.concatenate([-np.sin(ang), np.sin(ang), np.zeros((seq, width - n_rot), np.float32)], axis=1)
        return cos.astype(np.float32), sin.astype(np.float32)

    rc, rs = tables(RET_DK, RET_ROPE_BASE, RET_DK)
    dc, ds = tables(ROPE_DIMS, ROPE_THETA, DIL_HD)
    dil_tabs = []
    for _, dil in DIL_GROUPS:
        for tab in (dc, ds):
            dil_tabs.append(np.ascontiguousarray(tab.reshape(seq // dil, dil, DIL_HD).transpose(1, 0, 2)))
    return (np.tile(rc, (1, LANES // RET_DK)), np.tile(rs, (1, LANES // RET_DK))), dil_tabs


def _pack_w_in(w_in):
    d = w_in.shape[1]
    widths = (256, 256, 512, 512, GLA_RANK, 256, 256, 512, 512, 1536, 1536, 1536, 3 * d)
    ga0 = sum(widths[:4])
    dil0 = sum(widths[:9])
    gate0 = sum(widths[:12])
    w = w_in.astype(BF16)
    packed = jnp.concatenate(
        [w[:, :, :ga0], w[:, :, ga0 + GLA_RANK:dil0], w[:, :, gate0:], w[:, :, dil0:gate0],
         jnp.pad(w[:, :, ga0:ga0 + GLA_RANK], ((0, 0), (0, 0), (0, GA_PAD - GLA_RANK)))], axis=2)
    assert packed.shape[2] == W_GA_COL + GA_PAD
    return packed


def kernel(x, w_in, w_gla_a2, b_gla_a, gla_norm_g, ret_norm_g, w_branch, b_gate, w_out, ln1_g, ln1_b,
           w_ffn_gate, w_ffn_up, w_ffn_down, ln2_g, ln2_b):
    batch, seq, d = x.shape
    depth = w_in.shape[0]
    alpha = (2 * depth) ** 0.25
    ret_tabs, dil_tabs = _rope_tables(seq)
    w_packed = _pack_w_in(w_in)
    w_a2 = jnp.pad(w_gla_a2, ((0, 0), (0, GA_PAD - GLA_RANK), (0, 0))).astype(BF16)
    w_branch, w_out = w_branch.astype(BF16), w_out.astype(BF16)
    w_ffn_gate, w_ffn_up, w_ffn_down = w_ffn_gate.astype(BF16), w_ffn_up.astype(BF16), w_ffn_down.astype(BF16)
    b_gla_a, gla_norm_g, ret_norm_g, b_gate, ln1_g, ln1_b, ln2_g, ln2_b = (
        p[:, None, :] for p in (b_gla_a, gla_norm_g, ret_norm_g, b_gate, ln1_g, ln1_b, ln2_g, ln2_b))
    xf = x.reshape(batch * seq, d)
    for l in range(depth):
        oa, ob, *p_dil = _proj(xf, w_packed, (w_a2, b_gla_a, gla_norm_g, ret_norm_g), l, ret_tabs, dil_tabs,
                               batch, seq)
        o_dil, l_dil = [], []
        for pg, (_, dilation) in zip(p_dil, DIL_GROUPS):
            o, lse = _dil(pg, dilation, batch, seq)
            o_dil.append(o)
            l_dil.append(lse)
        xf = _merge(xf, oa, ob, o_dil, l_dil, w_packed, b_gate, w_branch, w_out, ln1_g, ln1_b, l, alpha,
                    MERGE_ROW_TILE)
        xf = _ffn(xf, w_ffn_gate, w_ffn_up, w_ffn_down, ln2_g, ln2_b, l, alpha, FF_ROW_TILE)
    return xf.reshape(batch, seq, d)
```

```python
import functools
import math

import jax
import jax.numpy as jnp
import numpy as np
from jax import lax
from jax.experimental import pallas as pl
from jax.experimental.pallas import tpu as pltpu

F32 = jnp.float32
BF16 = jnp.bfloat16

LANES = 128
N_HEADS = 4
GLA_DK = 64
GLA_DV = 128
GLA_RANK = 16
GLA_TAU = 16.0
GLA_CHUNK = 64
GLA_GROUP = 256
RET_DK = 64
RET_DV = 128
RET_CHUNK = 128
RET_GROUP = 256
RET_ROPE_BASE = 10000.0
DIL_HD = 128
DIL_GROUPS = ((128, 1), (512, 4), (2048, 16))
DIL_SPAN = 128
DIL_BLOCK = DIL_SPAN * max(d for _, d in DIL_GROUPS)
ROPE_THETA = 500000.0
ROPE_DIMS = 32
BRANCH_W = 512
GA_PAD = 256
GA_COL = 3072
RET_QK_COL = 1536
P1_GLA_COLS = (0, 256, 512, 1024)
P1_RET_COLS = (1536, 1792, 2048, 2560)
W_MAIN_COL = 0
W_GATE_COL = 3072
W_DIL_COL = 6144
W_GA_COL = 10752
MASK_VALUE = -1e30

VMEM_LIMIT = 56 * 1024 * 1024
SEQ_BLOCK = 512
FF_CHUNK = 256
FF_ROW_TILE = 1024
FF_ROW_SPLIT = 4
MERGE_ROW_TILE = 1024
MERGE_ROW_SPLIT = 4
PACK_ROWS = 256
DIL_UNROLL = 4


def _sigmoid(x):
    return 1.0 / (1.0 + jnp.exp(-x))


def _nt_dot(a, b):
    return lax.dot_general(a, b, (((1,), (1,)), ((), ())), preferred_element_type=F32)


def _tn_dot(a, b):
    return lax.dot_general(a, b, (((0,), (0,)), ((), ())), preferred_element_type=F32)


def _layer(stacked, layer):
    rest = stacked.shape[1:]
    return pl.BlockSpec((None,) + rest, lambda *_: (layer,) + (0,) * len(rest), pipeline_mode=pl.Buffered(1))


def _rotary(x, cos, sin, head_dim, half):
    width = x.shape[1]
    low = (lax.broadcasted_iota(jnp.int32, x.shape, 1) % head_dim) < half
    partner = jnp.where(low, pltpu.roll(x, width - half, 1), pltpu.roll(x, half, 1))
    return x * cos + partner * sin


def _proj_kernel(*refs):
    n_groups = len(DIL_GROUPS)
    x_ref, wm_ref, wq_ref, wk_ref, wv_ref, wa_ref, rc_ref, rs_ref = refs[:8]
    tabs = refs[8:8 + 2 * n_groups]
    wa2_ref, ba_ref, gg_ref, rg_ref = refs[8 + 2 * n_groups:12 + 2 * n_groups]
    oa_ref, ob_ref = refs[12 + 2 * n_groups:14 + 2 * n_groups]
    g_refs = refs[14 + 2 * n_groups:14 + 3 * n_groups]
    (slab_ref, slab4_ref, xp4_ref, xp16_ref, p1_ref,
     gst_ref, gqs_ref, gkd_ref, gacc_ref, rst_ref, rqd_ref, rkd_ref, racc_ref) = refs[14 + 3 * n_groups:]
    tm, k = x_ref.shape

    @pl.when(pl.program_id(1) == 0)
    def _():
        gst_ref[...] = jnp.zeros_like(gst_ref)
        rst_ref[...] = jnp.zeros_like(rst_ref)

    n_slabs = k // LANES
    tn = BRANCH_W
    x = x_ref[...].astype(BF16)
    lane = lax.broadcasted_iota(jnp.int32, (1, tn), 1)
    for j0 in range(0, GA_COL, tn):
        res = jnp.dot(x, wm_ref[:, j0:j0 + tn], preferred_element_type=F32)
        if j0 == 0:
            res = res * jnp.where(lane < N_HEADS * GLA_DK, GLA_DK ** -0.5, 1.0)
        if j0 == RET_QK_COL:
            cos = jnp.concatenate([rc_ref[...]] * (tn // LANES), axis=1)
            sin = jnp.concatenate([rs_ref[...]] * (tn // LANES), axis=1)
            res = _rotary(res, cos, sin, RET_DK, RET_DK // 2)
            res = res * jnp.where(lane < N_HEADS * RET_DK, 1.0, RET_DK ** -0.5)
        p1_ref[:, j0:j0 + tn] = res.astype(BF16)
    p1_ref[:, GA_COL:] = jnp.dot(x, wa_ref[...], preferred_element_type=F32).astype(BF16)
    _gla_body(p1_ref, wa2_ref, ba_ref, gg_ref, oa_ref, gst_ref, gqs_ref, gkd_ref, gacc_ref)
    _ret_body(p1_ref, rg_ref, ob_ref, rst_ref, rqd_ref, rkd_ref, racc_ref)
    q4 = tm // 4
    q16 = tm // 16
    for j in range(n_slabs):
        slab_ref[j] = x_ref[:, j * LANES:(j + 1) * LANES]
    for c in range(4):
        for j in range(n_slabs):
            part = slab_ref[j, pl.ds(c, q4, stride=4), :]
            slab4_ref[j, c * q4:(c + 1) * q4, :] = part
            xp4_ref[c * q4:(c + 1) * q4, j * LANES:(j + 1) * LANES] = part.astype(BF16)
    for c0 in range(4):
        for c1 in range(4):
            c = 4 * c1 + c0
            for j in range(n_slabs):
                xp16_ref[c * q16:(c + 1) * q16, j * LANES:(j + 1) * LANES] = (
                    slab4_ref[j, pl.ds(c0 * q4 + c1, q16, stride=4), :].astype(BF16))
    lhs_of = {1: lambda: x, 4: lambda: xp4_ref[...], 16: lambda: xp16_ref[...]}
    for g, (_, dil) in enumerate(DIL_GROUPS):
        rows = tm // dil
        lhs = lhs_of[dil]()
        cos = jnp.concatenate([tabs[2 * g][...].reshape(tm, LANES)] * N_HEADS, axis=1)
        sin = jnp.concatenate([tabs[2 * g + 1][...].reshape(tm, LANES)] * N_HEADS, axis=1)
        for part, w_ref in enumerate((wq_ref, wk_ref, wv_ref)):
            res = jnp.dot(lhs, w_ref[:, g * tn:(g + 1) * tn], preferred_element_type=F32)
            if part < 2:
                res = _rotary(res, cos, sin, DIL_HD, ROPE_DIMS // 2)
            if part == 0:
                res = res * (DIL_HD ** -0.5 * math.log2(math.e))
            g_refs[g][:, :, part * tn:(part + 1) * tn] = res.astype(BF16).reshape(dil, rows, tn)


def _cols(w, layer, width, offset):
    assert offset % width == 0
    return pl.BlockSpec((None, w.shape[1], width), lambda *_: (layer, 0, offset // width),
                        pipeline_mode=pl.Buffered(1))


def _proj(x, w, mixer_params, layer, ret_tabs, dil_tabs, batch, seq):
    assert tuple(d for _, d in DIL_GROUPS) == (1, 4, 16)
    t, k = x.shape
    tm = SEQ_BLOCK
    nblk = seq // tm
    w_specs = [_cols(w, layer, GA_COL, W_MAIN_COL)]
    w_specs += [_cols(w, layer, 3 * BRANCH_W, W_DIL_COL + part * 3 * BRANCH_W) for part in range(3)]
    w_specs += [_cols(w, layer, GA_PAD, W_GA_COL)]

    def row(b, i):
        return (b * nblk + i, 0)

    tab_specs, out_specs, out_shapes = [], [], []
    for _, dil in DIL_GROUPS:
        tab_specs += [pl.BlockSpec((dil, tm // dil, LANES), lambda b, i: (0, i, 0))] * 2
        out_specs.append(pl.BlockSpec((None, dil, tm // dil, 3 * BRANCH_W), lambda b, i: (b, 0, i, 0)))
        out_shapes.append(jax.ShapeDtypeStruct((batch, dil, seq // dil, 3 * BRANCH_W), BF16))
    ret_spec = pl.BlockSpec((tm, LANES), lambda b, i: (i, 0))
    branch = pl.BlockSpec((tm, BRANCH_W), row)
    state = pltpu.VMEM((N_HEADS * GLA_DV, N_HEADS * GLA_DK), F32)
    decayed = pltpu.VMEM((tm, N_HEADS * GLA_DK), BF16)
    acc = pltpu.VMEM((tm, BRANCH_W), F32)
    return pl.pallas_call(
        _proj_kernel,
        grid=(batch, nblk),
        in_specs=[pl.BlockSpec((tm, k), row)] + w_specs + [ret_spec, ret_spec] + tab_specs
                 + [_layer(p, layer) for p in mixer_params],
        out_specs=[branch, branch] + out_specs,
        out_shape=[jax.ShapeDtypeStruct((t, BRANCH_W), BF16)] * 2 + out_shapes,
        scratch_shapes=[pltpu.VMEM((k // LANES, tm, LANES), F32), pltpu.VMEM((k // LANES, tm, LANES), F32),
                        pltpu.VMEM((tm, k), BF16), pltpu.VMEM((tm, k), BF16),
                        pltpu.VMEM((tm, GA_COL + GA_PAD), BF16),
                        state, decayed, decayed, acc, state, decayed, decayed, acc],
        compiler_params=pltpu.CompilerParams(dimension_semantics=("parallel", "arbitrary"),
                                             vmem_limit_bytes=VMEM_LIMIT),
        name="in_proj_mixers",
    )(x, *([w] * 5), *ret_tabs, *dil_tabs, *mixer_params)


def _state_step(p1_ref, rows, v_col, st_ref, q_ref, kd_ref, acc_ref, decay, st_diag):
    st = st_ref[...]
    acc_ref[rows, :] += _nt_dot(q_ref[rows, :], st.astype(BF16))
    kv_t = _tn_dot(p1_ref[rows, v_col:v_col + BRANCH_W], kd_ref[rows, :])
    st_ref[...] = st * decay + jnp.where(st_diag, kv_t, 0.0)


def _gla_body(p1_ref, wa_ref, ba_ref, g_ref, o_ref, st_ref, qs_ref, kd_ref, acc_ref):
    c = GLA_CHUNK
    grp = GLA_GROUP
    tb = p1_ref.shape[0]
    width = N_HEADS * GLA_DK
    q_col, k_col, v_col, r_col = P1_GLA_COLS

    z = jnp.dot(p1_ref[:, GA_COL:GA_COL + GA_PAD], wa_ref[...], preferred_element_type=F32) + ba_ref[...]
    log_a = (jnp.minimum(z, 0.0) - jnp.log1p(jnp.exp(-jnp.abs(z)))) * (1.0 / GLA_TAU)
    hi = log_a.astype(BF16)
    lo = (log_a - hi.astype(F32)).astype(BF16)
    hl = jnp.concatenate([hi, lo], axis=1)

    row = lax.broadcasted_iota(jnp.int32, (grp, grp), 0)
    col = lax.broadcasted_iota(jnp.int32, (grp, grp), 1)
    same_chunk = (row // c) == (col // c)
    causal = same_chunk & (row >= col)
    cum_and_total = jnp.concatenate([jnp.where(causal, 1.0, 0.0), jnp.where(same_chunk, 1.0, 0.0)],
                                    axis=0).astype(BF16)
    lane_head = lax.broadcasted_iota(jnp.int32, (grp, width), 1) // GLA_DK
    st_diag = (lax.broadcasted_iota(jnp.int32, st_ref.shape, 0) // GLA_DV
               == lax.broadcasted_iota(jnp.int32, st_ref.shape, 1) // GLA_DK)

    decays = []
    for g0 in range(0, tb, grp):
        rows = slice(g0, g0 + grp)
        r2 = jnp.dot(cum_and_total, hl[rows], preferred_element_type=F32)
        b = r2[:grp, :width] + r2[:grp, width:]
        b_tot = r2[grp:, :width] + r2[grp:, width:]
        q = p1_ref[rows, q_col:q_col + width].astype(F32)
        k = p1_ref[rows, k_col:k_col + width].astype(F32)
        q_s = (q * jnp.exp(b)).astype(BF16)
        k_s = k * jnp.exp(-b)
        qs_ref[rows, :] = q_s
        kd_ref[rows, :] = (k * jnp.exp(b_tot - b)).astype(BF16)
        for n0 in range(0, grp, c):
            decays.append(jnp.exp(b_tot[n0:n0 + 1, :]))
        for h in range(N_HEADS):
            hs = slice(h * GLA_DV, (h + 1) * GLA_DV)
            k_h = jnp.where(lane_head == h, k_s, 0.0).astype(BF16)
            att = jnp.where(causal, _nt_dot(q_s, k_h), 0.0).astype(BF16)
            v_h = p1_ref[rows, v_col + h * GLA_DV:v_col + (h + 1) * GLA_DV]
            acc_ref[rows, hs] = jnp.dot(att, v_h, preferred_element_type=F32)

    for n, decay in enumerate(decays):
        rows = slice(n * c, (n + 1) * c)
        _state_step(p1_ref, rows, v_col, st_ref, qs_ref, kd_ref, acc_ref, decay, st_diag)

    outs = []
    for h in range(N_HEADS):
        o = acc_ref[:, h * GLA_DV:(h + 1) * GLA_DV]
        outs.append(o * lax.rsqrt(jnp.mean(o * o, axis=-1, keepdims=True) + 1e-6))
    y = jnp.concatenate(outs, axis=-1) * g_ref[...]
    r = p1_ref[:, r_col:r_col + BRANCH_W].astype(F32)
    o_ref[...] = (y * (r * _sigmoid(r))).astype(o_ref.dtype)


def _ret_body(p1_ref, g_ref, o_ref, st_ref, qd_ref, kd_ref, acc_ref):
    c = RET_CHUNK
    grp = RET_GROUP
    tb = p1_ref.shape[0]
    width = N_HEADS * RET_DK
    q_col, k_col, v_col, r_col = P1_RET_COLS

    log_g = [math.log1p(-(2.0 ** (-5.0 - h))) for h in range(N_HEADS)]
    lane_head = lax.broadcasted_iota(jnp.int32, (grp, width), 1) // RET_DK
    lg_lane = jnp.zeros((grp, width), F32)
    for h in range(N_HEADS):
        lg_lane = jnp.where(lane_head == h, log_g[h], lg_lane)
    ridx = (lax.broadcasted_iota(jnp.int32, (grp, width), 0) % c).astype(F32)
    q_scale = jnp.exp((ridx + 1.0) * lg_lane)
    k_scale = jnp.exp((c - 1.0 - ridx) * lg_lane)
    chunk_decay = jnp.exp(c * lg_lane[0:1, :])
    row = lax.broadcasted_iota(jnp.int32, (grp, grp), 0)
    col = lax.broadcasted_iota(jnp.int32, (grp, grp), 1)
    causal = ((row // c) == (col // c)) & (row >= col)
    diff = (row - col).astype(F32)
    st_diag = (lax.broadcasted_iota(jnp.int32, st_ref.shape, 0) // RET_DV
               == lax.broadcasted_iota(jnp.int32, st_ref.shape, 1) // RET_DK)

    for g0 in range(0, tb, grp):
        rows = slice(g0, g0 + grp)
        qb = p1_ref[rows, q_col:q_col + width]
        q = qb.astype(F32)
        k = p1_ref[rows, k_col:k_col + width].astype(F32)
        qd_ref[rows, :] = (q * q_scale).astype(BF16)
        kd_ref[rows, :] = (k * k_scale).astype(BF16)
        for h in range(N_HEADS):
            hs = slice(h * RET_DV, (h + 1) * RET_DV)
            dmat = jnp.where(causal, jnp.exp(jnp.maximum(diff, 0.0) * log_g[h]), 0.0)
            k_h = jnp.where(lane_head == h, k, 0.0).astype(BF16)
            att = (_nt_dot(qb, k_h) * dmat).astype(BF16)
            v_h = p1_ref[rows, v_col + h * RET_DV:v_col + (h + 1) * RET_DV]
            acc_ref[rows, hs] = jnp.dot(att, v_h, preferred_element_type=F32)

    for n in range(tb // c):
        rows = slice(n * c, (n + 1) * c)
        _state_step(p1_ref, rows, v_col, st_ref, qd_ref, kd_ref, acc_ref, chunk_decay, st_diag)

    outs = []
    for h in range(N_HEADS):
        o = acc_ref[:, h * RET_DV:(h + 1) * RET_DV]
        mu = jnp.mean(o, axis=-1, keepdims=True)
        d = o - mu
        outs.append(d * lax.rsqrt(jnp.mean(d * d, axis=-1, keepdims=True) + 1e-5))
    y = jnp.concatenate(outs, axis=-1) * g_ref[...]
    r = p1_ref[:, r_col:r_col + BRANCH_W].astype(F32)
    o_ref[...] = (y * (r * _sigmoid(r))).astype(o_ref.dtype)


def _dil_kernel(q_ref, k_ref, v_ref, kp_ref, vp_ref, o_ref, l_ref, o_scr, l_scr, *, dilation):
    n = DIL_SPAN
    nj = q_ref.shape[1] // n
    qry = lax.broadcasted_iota(jnp.int32, (n, 2 * n), 0)
    key = lax.broadcasted_iota(jnp.int32, (n, 2 * n), 1)
    dist = qry + n - key
    bias_t = (dist >= 0) & (dist <= n)
    bias_t_first = bias_t & ((key >= n) | (pl.program_id(1) > 0))
    ones = jnp.ones((2 * n, n), BF16)
    lane_head = lax.broadcasted_iota(jnp.int32, (n, LANES), 1) // (LANES // N_HEADS)

    def unit(c, j, bias, keys, values):
        start = j * (n * dilation) + c
        dst = pl.ds(start, n, stride=dilation) if dilation > 1 else pl.ds(start, n)
        q = q_ref[c, pl.ds(j * n, n), :]
        lse_tile = jnp.zeros((n, LANES), F32)
        for h in range(N_HEADS):
            hs = slice(h * DIL_HD, (h + 1) * DIL_HD)
            s = jnp.where(bias, _nt_dot(q[:, hs], keys(hs)), MASK_VALUE)
            m = jnp.max(s, axis=-1, keepdims=True)
            p = jnp.exp2(s - m)
            o_den = jnp.dot(p.astype(BF16), jnp.concatenate([values(hs), ones], axis=1),
                            preferred_element_type=F32)
            den = o_den[:, DIL_HD:]
            o_scr[h, dst, :] = o_den[:, :DIL_HD] / den
            lse_tile = jnp.where(lane_head == h, m * math.log(2.0) + jnp.log(den), lse_tile)
        l_scr[dst, :] = lse_tile

    def first_unit(c, carry):
        unit(c, 0, bias_t_first,
             lambda hs: jnp.concatenate([kp_ref[c, :, hs], k_ref[c, pl.ds(0, n), hs]], axis=0),
             lambda hs: jnp.concatenate([vp_ref[c, :, hs], v_ref[c, pl.ds(0, n), hs]], axis=0))
        return carry

    def later_unit(u, carry):
        c = u // (nj - 1)
        j = 1 + u % (nj - 1)
        r0 = pl.multiple_of((j - 1) * n, n)
        unit(c, j, bias_t,
             lambda hs: k_ref[c, pl.ds(r0, 2 * n), hs],
             lambda hs: v_ref[c, pl.ds(r0, 2 * n), hs])
        return carry

    lax.fori_loop(0, dilation, first_unit, 0, unroll=min(dilation, DIL_UNROLL))
    if nj > 1:
        lax.fori_loop(0, dilation * (nj - 1), later_unit, 0, unroll=DIL_UNROLL)
    for h in range(N_HEADS):
        o_ref[:, h * DIL_HD:(h + 1) * DIL_HD] = o_scr[h].astype(o_ref.dtype)
    l_ref[...] = l_scr[...]


def _dil(pg, dilation, batch, seq):
    t = batch * seq
    blk = DIL_BLOCK
    nblk = seq // blk
    rows = blk // dilation
    prev_per_blk = rows // DIL_SPAN

    def cur(part):
        return pl.BlockSpec((None, dilation, rows, BRANCH_W), lambda b, i: (b, 0, i, part))

    def prev(part):
        return pl.BlockSpec((None, dilation, DIL_SPAN, BRANCH_W),
                            lambda b, i: (b, 0, jnp.maximum(i * prev_per_blk - 1, 0), part))

    return pl.pallas_call(
        functools.partial(_dil_kernel, dilation=dilation),
        grid=(batch, nblk),
        in_specs=[cur(0), cur(1), cur(2), prev(1), prev(2)],
        out_specs=[pl.BlockSpec((blk, BRANCH_W), lambda b, i: (b * nblk + i, 0)),
                   pl.BlockSpec((blk, LANES), lambda b, i: (b * nblk + i, 0))],
        out_shape=[jax.ShapeDtypeStruct((t, BRANCH_W), BF16), jax.ShapeDtypeStruct((t, LANES), F32)],
        scratch_shapes=[pltpu.VMEM((N_HEADS, blk, DIL_HD), F32), pltpu.VMEM((blk, LANES), F32)],
        compiler_params=pltpu.CompilerParams(dimension_semantics=("parallel", "arbitrary"),
                                             vmem_limit_bytes=VMEM_LIMIT),
        name=f"dilated_attn_{dilation}",
    )(pg, pg, pg, pg, pg)


def _layer_norm(v, g, b):
    mu = jnp.mean(v, axis=-1, keepdims=True)
    d = v - mu
    return d * lax.rsqrt(jnp.mean(d * d, axis=-1, keepdims=True) + 1e-5) * g + b


def _merge_kernel(x_ref, oa_ref, ob_ref, o1_ref, o2_ref, o3_ref, l1_ref, l2_ref, l3_ref,
                  wg_ref, bg_ref, wb_ref, wo_ref, lg_ref, lb_ref, y_ref, *, alpha):
    tm, d = x_ref.shape
    lanes_per_head = LANES // N_HEADS
    for r0 in range(0, tm, tm // MERGE_ROW_SPLIT):
        rows = slice(r0, r0 + tm // MERGE_ROW_SPLIT)
        l1, l2, l3 = l1_ref[rows, :], l2_ref[rows, :], l3_ref[rows, :]
        m = jnp.maximum(jnp.maximum(l1, l2), l3)
        e1, e2, e3 = jnp.exp(l1 - m), jnp.exp(l2 - m), jnp.exp(l3 - m)
        tot = e1 + e2 + e3
        w1, w2, w3 = e1 / tot, e2 / tot, e3 / tot
        parts = []
        for h in range(N_HEADS):
            hs = slice(h * DIL_HD, (h + 1) * DIL_HD)
            ls = slice(h * lanes_per_head, h * lanes_per_head + 1)
            parts.append(w1[:, ls] * o1_ref[rows, hs].astype(F32) + w2[:, ls] * o2_ref[rows, hs].astype(F32)
                         + w3[:, ls] * o3_ref[rows, hs].astype(F32))
        oc = jnp.concatenate(parts, axis=-1).astype(BF16)
        x = x_ref[rows, :]
        xb = x.astype(BF16)
        merged = None
        for j, o in enumerate((oa_ref[rows, :], ob_ref[rows, :], oc)):
            gate = _sigmoid(jnp.dot(xb, wg_ref[:, j * d:(j + 1) * d], preferred_element_type=F32)
                            + bg_ref[:, j * d:(j + 1) * d])
            term = gate * jnp.dot(o, wb_ref[j], preferred_element_type=F32)
            merged = term if merged is None else merged + term
        hmix = jnp.dot(merged.astype(BF16), wo_ref[...], preferred_element_type=F32)
        y_ref[rows, :] = _layer_norm(alpha * x + hmix, lg_ref[...], lb_ref[...])


def _merge(x, oa, ob, o_dil, l_dil, w_packed, b_gate, w_branch, w_out, ln_g, ln_b, layer, alpha, tm):
    t, d = x.shape

    def rows(width):
        return pl.BlockSpec((tm, width), lambda i: (i, 0))

    params = (b_gate, w_branch, w_out, ln_g, ln_b)
    return pl.pallas_call(
        functools.partial(_merge_kernel, alpha=alpha),
        grid=(t // tm,),
        in_specs=[rows(d)] + [rows(BRANCH_W)] * 5 + [rows(LANES)] * 3
                 + [_cols(w_packed, layer, 3 * d, W_GATE_COL)] + [_layer(p, layer) for p in params],
        out_specs=rows(d),
        out_shape=jax.ShapeDtypeStruct((t, d), F32),
        compiler_params=pltpu.CompilerParams(dimension_semantics=("parallel",), vmem_limit_bytes=VMEM_LIMIT),
        name="merge_out_ln",
    )(x, oa, ob, *o_dil, *l_dil, w_packed, *params)


def _ffn_kernel(x_ref, wg_ref, wu_ref, wd_ref, lg_ref, lb_ref, y_ref, act_ref, *, alpha):
    d_ff = wg_ref.shape[1]
    tm = x_ref.shape[0]
    for r0 in range(0, tm, tm // FF_ROW_SPLIT):
        rows = slice(r0, r0 + tm // FF_ROW_SPLIT)
        x = x_ref[rows, :]
        xb = x.astype(BF16)
        for j0 in range(0, d_ff, FF_CHUNK):
            cs = slice(j0, min(j0 + FF_CHUNK, d_ff))
            g = jnp.dot(xb, wg_ref[:, cs], preferred_element_type=F32)
            u = jnp.dot(xb, wu_ref[:, cs], preferred_element_type=F32)
            act_ref[rows, cs] = (g * _sigmoid(g) * u).astype(BF16)
        f = jnp.dot(act_ref[rows, :], wd_ref[...], preferred_element_type=F32)
        y_ref[rows, :] = _layer_norm(alpha * x + f, lg_ref[...], lb_ref[...])


def _ffn(x, w_gate, w_up, w_down, ln_g, ln_b, layer, alpha, tm):
    t, d = x.shape
    d_ff = w_gate.shape[2]
    rows = pl.BlockSpec((tm, d), lambda i: (i, 0))
    params = (w_gate, w_up, w_down, ln_g, ln_b)
    return pl.pallas_call(
        functools.partial(_ffn_kernel, alpha=alpha),
        grid=(t // tm,),
        in_specs=[rows] + [_layer(p, layer) for p in params],
        out_specs=rows,
        out_shape=jax.ShapeDtypeStruct((t, d), F32),
        scratch_shapes=[pltpu.VMEM((tm, d_ff), BF16)],
        compiler_params=pltpu.CompilerParams(dimension_semantics=("parallel",), vmem_limit_bytes=VMEM_LIMIT),
        name="swiglu_ln",
    )(x, *params)


def _rope_tables(seq):
    pos = np.arange(seq, dtype=np.float32)

    def tables(n_rot, base, width):
        half = n_rot // 2
        inv = np.float32(base) ** (-np.arange(half, dtype=np.float32) * np.float32(2.0) / np.float32(n_rot))
        ang = (pos[:, None] * inv[None, :]).astype(np.float32)
        cos = np.concatenate([np.cos(ang), np.cos(ang), np.ones((seq, width - n_rot), np.float32)], axis=1)
        sin = np.concatenate([-np.sin(ang), np.sin(ang), np.zeros((seq, width - n_rot), np.float32)], axis=1)
        return cos.astype(np.float32), sin.astype(np.float32)

    rc, rs = tables(RET_DK, RET_ROPE_BASE, RET_DK)
    dc, ds = tables(ROPE_DIMS, ROPE_THETA, DIL_HD)
    dil_tabs = []
    for _, dil in DIL_GROUPS:
        for tab in (dc, ds):
            dil_tabs.append(np.ascontiguousarray(tab.reshape(seq // dil, dil, DIL_HD).transpose(1, 0, 2)))
    return (np.tile(rc, (1, LANES // RET_DK)), np.tile(rs, (1, LANES // RET_DK))), dil_tabs


def _pack_kernel(w_ref, o_ref, *, ga0, dil0, gate0):
    d_in = w_ref.shape[1]
    o_ref[:, W_MAIN_COL:ga0] = w_ref[:, :ga0].astype(BF16)
    o_ref[:, ga0:W_GATE_COL] = w_ref[:, ga0 + GLA_RANK:dil0].astype(BF16)
    o_ref[:, W_GATE_COL:W_DIL_COL] = w_ref[:, gate0:d_in].astype(BF16)
    o_ref[:, W_DIL_COL:W_GA_COL] = w_ref[:, dil0:gate0].astype(BF16)
    o_ref[:, W_GA_COL:W_GA_COL + GLA_RANK] = w_ref[:, ga0:ga0 + GLA_RANK].astype(BF16)
    o_ref[:, W_GA_COL + GLA_RANK:] = jnp.zeros((w_ref.shape[0], GA_PAD - GLA_RANK), BF16)


def _pack_w_in(w_in):
    depth, d, d_in = w_in.shape
    widths = (256, 256, 512, 512, GLA_RANK, 256, 256, 512, 512, 1536, 1536, 1536, 3 * d)
    assert sum(widths) == d_in
    ga0 = sum(widths[:4])
    dil0 = sum(widths[:9])
    gate0 = sum(widths[:12])
    assert (ga0 + (dil0 - ga0 - GLA_RANK), gate0 - dil0, d_in - gate0) == (
        W_GATE_COL, W_GA_COL - W_DIL_COL, W_DIL_COL - W_GATE_COL)
    n_out = W_GA_COL + GA_PAD
    return pl.pallas_call(
        functools.partial(_pack_kernel, ga0=ga0, dil0=dil0, gate0=gate0),
        grid=(depth, d // PACK_ROWS),
        in_specs=[pl.BlockSpec((None, PACK_ROWS, d_in), lambda l, r: (l, r, 0))],
        out_specs=pl.BlockSpec((None, PACK_ROWS, n_out), lambda l, r: (l, r, 0)),
        out_shape=jax.ShapeDtypeStruct((depth, d, n_out), BF16),
        compiler_params=pltpu.CompilerParams(dimension_semantics=("parallel", "parallel"),
                                             vmem_limit_bytes=VMEM_LIMIT),
        name="pack_w_in",
    )(w_in)


def kernel(x, w_in, w_gla_a2, b_gla_a, gla_norm_g, ret_norm_g, w_branch, b_gate, w_out, ln1_g, ln1_b,
           w_ffn_gate, w_ffn_up, w_ffn_down, ln2_g, ln2_b):
    batch, seq, d = x.shape
    depth = w_in.shape[0]
    alpha = (2 * depth) ** 0.25
    ret_tabs, dil_tabs = _rope_tables(seq)
    w_packed = _pack_w_in(w_in)
    w_a2 = jnp.pad(w_gla_a2, ((0, 0), (0, GA_PAD - GLA_RANK), (0, 0))).astype(BF16)
    w_branch, w_out = w_branch.astype(BF16), w_out.astype(BF16)
    w_ffn_gate, w_ffn_up, w_ffn_down = w_ffn_gate.astype(BF16), w_ffn_up.astype(BF16), w_ffn_down.astype(BF16)
    b_gla_a, gla_norm_g, ret_norm_g, b_gate, ln1_g, ln1_b, ln2_g, ln2_b = (
        p[:, None, :] for p in (b_gla_a, gla_norm_g, ret_norm_g, b_gate, ln1_g, ln1_b, ln2_g, ln2_b))
    xf = x.reshape(batch * seq, d)
    for l in range(depth):
        oa, ob, *p_dil = _proj(xf, w_packed, (w_a2, b_gla_a, gla_norm_g, ret_norm_g), l, ret_tabs, dil_tabs,
                               batch, seq)
        o_dil, l_dil = [], []
        for pg, (_, dilation) in zip(p_dil, DIL_GROUPS):
            o, lse = _dil(pg, dilation, batch, seq)
            o_dil.append(o)
            l_dil.append(lse)
        xf = _merge(xf, oa, ob, o_dil, l_dil, w_packed, b_gate, w_branch, w_out, ln1_g, ln1_b, l, alpha,
                    MERGE_ROW_TILE)
        xf = _ffn(xf, w_ffn_gate, w_ffn_up, w_ffn_down, ln2_g, ln2_b, l, alpha, FF_ROW_TILE)
    return xf.reshape(batch, seq, d)
```

```python
import functools
import math

import jax
import jax.numpy as jnp
import numpy as np
from jax import lax
from jax.experimental import pallas as pl
from jax.experimental.pallas import tpu as pltpu

F32 = jnp.float32
BF16 = jnp.bfloat16

LANES = 128
N_HEADS = 4
GLA_DK = 64
GLA_DV = 128
GLA_RANK = 16
GLA_TAU = 16.0
GLA_CHUNK = 64
GLA_GROUP = 256
RET_DK = 64
RET_DV = 128
RET_CHUNK = 128
RET_GROUP = 256
RET_ROPE_BASE = 10000.0
DIL_HD = 128
DIL_GROUPS = ((128, 1), (512, 4), (2048, 16))
DIL_SPAN = 128
DIL_BLOCK = DIL_SPAN * max(d for _, d in DIL_GROUPS)
ROPE_THETA = 500000.0
ROPE_DIMS = 32
BRANCH_W = 512
GA_PAD = 256
GA_COL = 3072
RET_QK_COL = 1536
P1_GLA_COLS = (0, 256, 512, 1024)
P1_RET_COLS = (1536, 1792, 2048, 2560)
W_MAIN_COL = 0
W_GATE_COL = 3072
W_DIL_COL = 6144
W_GA_COL = 10752
MASK_VALUE = -1e30

VMEM_LIMIT = 56 * 1024 * 1024
SEQ_BLOCK = 512
FF_CHUNK = 256
FF_ROW_TILE = 1024
FF_ROW_SPLIT = 4
MERGE_ROW_TILE = 1024
MERGE_ROW_SPLIT = 4
DIL_UNROLL = 4


def _sigmoid(x):
    return 0.5 * (jnp.tanh(0.5 * x) + 1.0)


def _nt_dot(a, b):
    return lax.dot_general(a, b, (((1,), (1,)), ((), ())), preferred_element_type=F32)


def _tn_dot(a, b):
    return lax.dot_general(a, b, (((0,), (0,)), ((), ())), preferred_element_type=F32)


def _layer(stacked, layer):
    rest = stacked.shape[1:]
    return pl.BlockSpec((None,) + rest, lambda *_: (layer,) + (0,) * len(rest), pipeline_mode=pl.Buffered(1))


def _rotary(x, cos, sin, head_dim, half):
    width = x.shape[1]
    low = (lax.broadcasted_iota(jnp.int32, x.shape, 1) % head_dim) < half
    partner = jnp.where(low, pltpu.roll(x, width - half, 1), pltpu.roll(x, half, 1))
    return x * cos + partner * sin


def _proj_kernel(*refs):
    n_groups = len(DIL_GROUPS)
    x_ref, wm_ref, wq_ref, wk_ref, wv_ref, wa_ref, rc_ref, rs_ref = refs[:8]
    tabs = refs[8:8 + 2 * n_groups]
    wa2_ref, ba_ref, gg_ref, rg_ref = refs[8 + 2 * n_groups:12 + 2 * n_groups]
    oa_ref, ob_ref = refs[12 + 2 * n_groups:14 + 2 * n_groups]
    g_refs = refs[14 + 2 * n_groups:14 + 3 * n_groups]
    (slab_ref, slab4_ref, xp4_ref, xp16_ref, p1_ref,
     gst_ref, gqs_ref, gkd_ref, gacc_ref, rst_ref, rqd_ref, rkd_ref, racc_ref) = refs[14 + 3 * n_groups:]
    tm, k = x_ref.shape

    @pl.when(pl.program_id(1) == 0)
    def _():
        gst_ref[...] = jnp.zeros_like(gst_ref)
        rst_ref[...] = jnp.zeros_like(rst_ref)

    n_slabs = k // LANES
    tn = BRANCH_W
    x = x_ref[...].astype(BF16)
    lane = lax.broadcasted_iota(jnp.int32, (1, tn), 1)
    for j0 in range(0, GA_COL, tn):
        res = jnp.dot(x, wm_ref[:, j0:j0 + tn], preferred_element_type=F32)
        if j0 == 0:
            res = res * jnp.where(lane < N_HEADS * GLA_DK, GLA_DK ** -0.5, 1.0)
        if j0 == RET_QK_COL:
            cos = jnp.concatenate([rc_ref[...]] * (tn // LANES), axis=1)
            sin = jnp.concatenate([rs_ref[...]] * (tn // LANES), axis=1)
            res = _rotary(res, cos, sin, RET_DK, RET_DK // 2)
            res = res * jnp.where(lane < N_HEADS * RET_DK, 1.0, RET_DK ** -0.5)
        p1_ref[:, j0:j0 + tn] = res.astype(BF16)
    p1_ref[:, GA_COL:] = jnp.dot(x, wa_ref[...], preferred_element_type=F32).astype(BF16)
    _gla_body(p1_ref, wa2_ref, ba_ref, gg_ref, oa_ref, gst_ref, gqs_ref, gkd_ref, gacc_ref)
    _ret_body(p1_ref, rg_ref, ob_ref, rst_ref, rqd_ref, rkd_ref, racc_ref)
    q4 = tm // 4
    q16 = tm // 16
    for j in range(n_slabs):
        slab_ref[j] = x_ref[:, j * LANES:(j + 1) * LANES]
    for c in range(4):
        for j in range(n_slabs):
            part = slab_ref[j, pl.ds(c, q4, stride=4), :]
            slab4_ref[j, c * q4:(c + 1) * q4, :] = part
            xp4_ref[c * q4:(c + 1) * q4, j * LANES:(j + 1) * LANES] = part.astype(BF16)
    for c0 in range(4):
        for c1 in range(4):
            c = 4 * c1 + c0
            for j in range(n_slabs):
                xp16_ref[c * q16:(c + 1) * q16, j * LANES:(j + 1) * LANES] = (
                    slab4_ref[j, pl.ds(c0 * q4 + c1, q16, stride=4), :].astype(BF16))
    lhs_of = {1: lambda: x, 4: lambda: xp4_ref[...], 16: lambda: xp16_ref[...]}
    for g, (_, dil) in enumerate(DIL_GROUPS):
        rows = tm // dil
        lhs = lhs_of[dil]()
        cos = jnp.concatenate([tabs[2 * g][...].reshape(tm, LANES)] * N_HEADS, axis=1)
        sin = jnp.concatenate([tabs[2 * g + 1][...].reshape(tm, LANES)] * N_HEADS, axis=1)
        for part, w_ref in enumerate((wq_ref, wk_ref, wv_ref)):
            res = jnp.dot(lhs, w_ref[:, g * tn:(g + 1) * tn], preferred_element_type=F32)
            if part < 2:
                res = _rotary(res, cos, sin, DIL_HD, ROPE_DIMS // 2)
            if part == 0:
                res = res * (DIL_HD ** -0.5 * math.log2(math.e))
            g_refs[g][:, :, part * tn:(part + 1) * tn] = res.astype(BF16).reshape(dil, rows, tn)


def _cols(w, layer, width, offset):
    assert offset % width == 0
    return pl.BlockSpec((None, w.shape[1], width), lambda *_: (layer, 0, offset // width),
                        pipeline_mode=pl.Buffered(1))


def _proj(x, w, mixer_params, layer, ret_tabs, dil_tabs, batch, seq):
    assert tuple(d for _, d in DIL_GROUPS) == (1, 4, 16)
    t, k = x.shape
    tm = SEQ_BLOCK
    nblk = seq // tm
    w_specs = [_cols(w, layer, GA_COL, W_MAIN_COL)]
    w_specs += [_cols(w, layer, 3 * BRANCH_W, W_DIL_COL + part * 3 * BRANCH_W) for part in range(3)]
    w_specs += [_cols(w, layer, GA_PAD, W_GA_COL)]

    def row(b, i):
        return (b * nblk + i, 0)

    tab_specs, out_specs, out_shapes = [], [], []
    for _, dil in DIL_GROUPS:
        tab_specs += [pl.BlockSpec((dil, tm // dil, LANES), lambda b, i: (0, i, 0))] * 2
        out_specs.append(pl.BlockSpec((None, dil, tm // dil, 3 * BRANCH_W), lambda b, i: (b, 0, i, 0)))
        out_shapes.append(jax.ShapeDtypeStruct((batch, dil, seq // dil, 3 * BRANCH_W), BF16))
    ret_spec = pl.BlockSpec((tm, LANES), lambda b, i: (i, 0))
    branch = pl.BlockSpec((tm, BRANCH_W), row)
    state = pltpu.VMEM((N_HEADS * GLA_DV, N_HEADS * GLA_DK), F32)
    decayed = pltpu.VMEM((tm, N_HEADS * GLA_DK), BF16)
    acc = pltpu.VMEM((tm, BRANCH_W), F32)
    return pl.pallas_call(
        _proj_kernel,
        grid=(batch, nblk),
        in_specs=[pl.BlockSpec((tm, k), row)] + w_specs + [ret_spec, ret_spec] + tab_specs
                 + [_layer(p, layer) for p in mixer_params],
        out_specs=[branch, branch] + out_specs,
        out_shape=[jax.ShapeDtypeStruct((t, BRANCH_W), BF16)] * 2 + out_shapes,
        scratch_shapes=[pltpu.VMEM((k // LANES, tm, LANES), F32), pltpu.VMEM((k // LANES, tm, LANES), F32),
                        pltpu.VMEM((tm, k), BF16), pltpu.VMEM((tm, k), BF16),
                        pltpu.VMEM((tm, GA_COL + GA_PAD), BF16),
                        state, decayed, decayed, acc, state, decayed, decayed, acc],
        compiler_params=pltpu.CompilerParams(dimension_semantics=("parallel", "arbitrary"),
                                             vmem_limit_bytes=VMEM_LIMIT),
        name="in_proj_mixers",
    )(x, *([w] * 5), *ret_tabs, *dil_tabs, *mixer_params)


def _state_step(p1_ref, rows, v_col, st_ref, q_ref, kd_ref, acc_ref, decay, st_diag):
    st = st_ref[...]
    acc_ref[rows, :] += _nt_dot(q_ref[rows, :], st.astype(BF16))
    kv_t = _tn_dot(p1_ref[rows, v_col:v_col + BRANCH_W], kd_ref[rows, :])
    st_ref[...] = st * decay + jnp.where(st_diag, kv_t, 0.0)


def _gla_body(p1_ref, wa_ref, ba_ref, g_ref, o_ref, st_ref, qs_ref, kd_ref, acc_ref):
    c = GLA_CHUNK
    grp = GLA_GROUP
    tb = p1_ref.shape[0]
    width = N_HEADS * GLA_DK
    q_col, k_col, v_col, r_col = P1_GLA_COLS

    z = jnp.dot(p1_ref[:, GA_COL:GA_COL + GA_PAD], wa_ref[...], preferred_element_type=F32) + ba_ref[...]
    log_a = (jnp.minimum(z, 0.0) - jnp.log1p(jnp.exp(-jnp.abs(z)))) * (1.0 / GLA_TAU)
    hi = log_a.astype(BF16)
    lo = (log_a - hi.astype(F32)).astype(BF16)
    hl = jnp.concatenate([hi, lo], axis=1)

    row = lax.broadcasted_iota(jnp.int32, (grp, grp), 0)
    col = lax.broadcasted_iota(jnp.int32, (grp, grp), 1)
    causal = ((row // c) == (col // c)) & (row >= col)
    cum = jnp.where(causal, 1.0, 0.0).astype(BF16)
    lane_head = lax.broadcasted_iota(jnp.int32, (grp, width), 1) // GLA_DK
    st_diag = (lax.broadcasted_iota(jnp.int32, st_ref.shape, 0) // GLA_DV
               == lax.broadcasted_iota(jnp.int32, st_ref.shape, 1) // GLA_DK)

    decays = []
    for g0 in range(0, tb, grp):
        rows = slice(g0, g0 + grp)
        r2 = jnp.dot(cum, hl[rows], preferred_element_type=F32)
        b = r2[:, :width] + r2[:, width:]
        b_tot = jnp.concatenate([jnp.broadcast_to(b[n0 + c - 1:n0 + c, :], (c, width)) for n0 in range(0, grp, c)],
                                axis=0)
        q = p1_ref[rows, q_col:q_col + width].astype(F32)
        k = p1_ref[rows, k_col:k_col + width].astype(F32)
        q_s = (q * jnp.exp(b)).astype(BF16)
        k_s = k * jnp.exp(-b)
        qs_ref[rows, :] = q_s
        kd_ref[rows, :] = (k * jnp.exp(b_tot - b)).astype(BF16)
        for n0 in range(0, grp, c):
            decays.append(jnp.exp(b_tot[n0:n0 + 1, :]))
        for h in range(N_HEADS):
            hs = slice(h * GLA_DV, (h + 1) * GLA_DV)
            k_h = jnp.where(lane_head == h, k_s, 0.0).astype(BF16)
            att = jnp.where(causal, _nt_dot(q_s, k_h), 0.0).astype(BF16)
            v_h = p1_ref[rows, v_col + h * GLA_DV:v_col + (h + 1) * GLA_DV]
            acc_ref[rows, hs] = jnp.dot(att, v_h, preferred_element_type=F32)

    for n, decay in enumerate(decays):
        rows = slice(n * c, (n + 1) * c)
        _state_step(p1_ref, rows, v_col, st_ref, qs_ref, kd_ref, acc_ref, decay, st_diag)

    outs = []
    for h in range(N_HEADS):
        o = acc_ref[:, h * GLA_DV:(h + 1) * GLA_DV]
        outs.append(o * lax.rsqrt(jnp.mean(o * o, axis=-1, keepdims=True) + 1e-6))
    y = jnp.concatenate(outs, axis=-1) * g_ref[...]
    r = p1_ref[:, r_col:r_col + BRANCH_W].astype(F32)
    o_ref[...] = (y * (r * _sigmoid(r))).astype(o_ref.dtype)


def _ret_body(p1_ref, g_ref, o_ref, st_ref, qd_ref, kd_ref, acc_ref):
    c = RET_CHUNK
    grp = RET_GROUP
    tb = p1_ref.shape[0]
    width = N_HEADS * RET_DK
    q_col, k_col, v_col, r_col = P1_RET_COLS

    log_g = [math.log1p(-(2.0 ** (-5.0 - h))) for h in range(N_HEADS)]
    lane_head = lax.broadcasted_iota(jnp.int32, (grp, width), 1) // RET_DK
    lg_lane = jnp.zeros((grp, width), F32)
    for h in range(N_HEADS):
        lg_lane = jnp.where(lane_head == h, log_g[h], lg_lane)
    ridx = (lax.broadcasted_iota(jnp.int32, (grp, width), 0) % c).astype(F32)
    q_scale = jnp.exp((ridx + 1.0) * lg_lane)
    k_scale = jnp.exp((c - 1.0 - ridx) * lg_lane)
    chunk_decay = jnp.exp(c * lg_lane[0:1, :])
    row = lax.broadcasted_iota(jnp.int32, (grp, grp), 0)
    col = lax.broadcasted_iota(jnp.int32, (grp, grp), 1)
    causal = ((row // c) == (col // c)) & (row >= col)
    diff = jnp.maximum(row - col, 0).astype(F32)
    dmats = [jnp.where(causal, jnp.exp(diff * log_g[h]), 0.0) for h in range(N_HEADS)]
    st_diag = (lax.broadcasted_iota(jnp.int32, st_ref.shape, 0) // RET_DV
               == lax.broadcasted_iota(jnp.int32, st_ref.shape, 1) // RET_DK)

    for g0 in range(0, tb, grp):
        rows = slice(g0, g0 + grp)
        qb = p1_ref[rows, q_col:q_col + width]
        q = qb.astype(F32)
        k = p1_ref[rows, k_col:k_col + width].astype(F32)
        qd_ref[rows, :] = (q * q_scale).astype(BF16)
        kd_ref[rows, :] = (k * k_scale).astype(BF16)
        for h in range(N_HEADS):
            hs = slice(h * RET_DV, (h + 1) * RET_DV)
            k_h = jnp.where(lane_head == h, k, 0.0).astype(BF16)
            att = (_nt_dot(qb, k_h) * dmats[h]).astype(BF16)
            v_h = p1_ref[rows, v_col + h * RET_DV:v_col + (h + 1) * RET_DV]
            acc_ref[rows, hs] = jnp.dot(att, v_h, preferred_element_type=F32)

    for n in range(tb // c):
        rows = slice(n * c, (n + 1) * c)
        _state_step(p1_ref, rows, v_col, st_ref, qd_ref, kd_ref, acc_ref, chunk_decay, st_diag)

    outs = []
    for h in range(N_HEADS):
        o = acc_ref[:, h * RET_DV:(h + 1) * RET_DV]
        mu = jnp.mean(o, axis=-1, keepdims=True)
        d = o - mu
        outs.append(d * lax.rsqrt(jnp.mean(d * d, axis=-1, keepdims=True) + 1e-5))
    y = jnp.concatenate(outs, axis=-1) * g_ref[...]
    r = p1_ref[:, r_col:r_col + BRANCH_W].astype(F32)
    o_ref[...] = (y * (r * _sigmoid(r))).astype(o_ref.dtype)


def _dil_kernel(q_ref, k_ref, v_ref, kp_ref, vp_ref, o_ref, l_ref, o_scr, l_scr, *, dilation):
    n = DIL_SPAN
    nj = q_ref.shape[1] // n
    qry = lax.broadcasted_iota(jnp.int32, (n, 2 * n), 0)
    key = lax.broadcasted_iota(jnp.int32, (n, 2 * n), 1)
    dist = qry + n - key
    bias_t = (dist >= 0) & (dist <= n)
    bias_t_first = bias_t & ((key >= n) | (pl.program_id(1) > 0))
    ones = jnp.ones((2 * n, n), BF16)
    lane_head = lax.broadcasted_iota(jnp.int32, (n, LANES), 1) // (LANES // N_HEADS)

    def unit(c, j, bias, keys, values):
        start = j * (n * dilation) + c
        dst = pl.ds(start, n, stride=dilation) if dilation > 1 else pl.ds(start, n)
        q = q_ref[c, pl.ds(j * n, n), :]
        lse_tile = jnp.zeros((n, LANES), F32)
        for h in range(N_HEADS):
            hs = slice(h * DIL_HD, (h + 1) * DIL_HD)
            s = jnp.where(bias, _nt_dot(q[:, hs], keys(hs)), MASK_VALUE)
            m = jnp.max(s, axis=-1, keepdims=True)
            p = jnp.exp2(s - m)
            o_den = jnp.dot(p.astype(BF16), jnp.concatenate([values(hs), ones], axis=1),
                            preferred_element_type=F32)
            den = o_den[:, DIL_HD:]
            o_scr[h, dst, :] = o_den[:, :DIL_HD] / den
            lse_tile = jnp.where(lane_head == h, m * math.log(2.0) + jnp.log(den), lse_tile)
        l_scr[dst, :] = lse_tile

    def first_unit(c, carry):
        unit(c, 0, bias_t_first,
             lambda hs: jnp.concatenate([kp_ref[c, :, hs], k_ref[c, pl.ds(0, n), hs]], axis=0),
             lambda hs: jnp.concatenate([vp_ref[c, :, hs], v_ref[c, pl.ds(0, n), hs]], axis=0))
        return carry

    def later_unit(u, carry):
        c = u // (nj - 1)
        j = 1 + u % (nj - 1)
        r0 = pl.multiple_of((j - 1) * n, n)
        unit(c, j, bias_t,
             lambda hs: k_ref[c, pl.ds(r0, 2 * n), hs],
             lambda hs: v_ref[c, pl.ds(r0, 2 * n), hs])
        return carry

    lax.fori_loop(0, dilation, first_unit, 0, unroll=min(dilation, DIL_UNROLL))
    if nj > 1:
        lax.fori_loop(0, dilation * (nj - 1), later_unit, 0, unroll=DIL_UNROLL)
    for h in range(N_HEADS):
        o_ref[:, h * DIL_HD:(h + 1) * DIL_HD] = o_scr[h].astype(o_ref.dtype)
    l_ref[...] = l_scr[...]


def _dil(pg, dilation, batch, seq):
    t = batch * seq
    blk = DIL_BLOCK
    nblk = seq // blk
    rows = blk // dilation
    prev_per_blk = rows // DIL_SPAN

    def cur(part):
        return pl.BlockSpec((None, dilation, rows, BRANCH_W), lambda b, i: (b, 0, i, part))

    def prev(part):
        return pl.BlockSpec((None, dilation, DIL_SPAN, BRANCH_W),
                            lambda b, i: (b, 0, jnp.maximum(i * prev_per_blk - 1, 0), part))

    return pl.pallas_call(
        functools.partial(_dil_kernel, dilation=dilation),
        grid=(batch, nblk),
        in_specs=[cur(0), cur(1), cur(2), prev(1), prev(2)],
        out_specs=[pl.BlockSpec((blk, BRANCH_W), lambda b, i: (b * nblk + i, 0)),
                   pl.BlockSpec((blk, LANES), lambda b, i: (b * nblk + i, 0))],
        out_shape=[jax.ShapeDtypeStruct((t, BRANCH_W), BF16), jax.ShapeDtypeStruct((t, LANES), F32)],
        scratch_shapes=[pltpu.VMEM((N_HEADS, blk, DIL_HD), F32), pltpu.VMEM((blk, LANES), F32)],
        compiler_params=pltpu.CompilerParams(dimension_semantics=("parallel", "arbitrary"),
                                             vmem_limit_bytes=VMEM_LIMIT),
        name=f"dilated_attn_{dilation}",
    )(pg, pg, pg, pg, pg)


def _layer_norm(v, g, b):
    mu = jnp.mean(v, axis=-1, keepdims=True)
    d = v - mu
    return d * lax.rsqrt(jnp.mean(d * d, axis=-1, keepdims=True) + 1e-5) * g + b


def _merge_kernel(x_ref, oa_ref, ob_ref, o1_ref, o2_ref, o3_ref, l1_ref, l2_ref, l3_ref,
                  wg_ref, bg_ref, wb_ref, wo_ref, lg_ref, lb_ref, y_ref, *, alpha):
    tm, d = x_ref.shape
    lanes_per_head = LANES // N_HEADS
    for r0 in range(0, tm, tm // MERGE_ROW_SPLIT):
        rows = slice(r0, r0 + tm // MERGE_ROW_SPLIT)
        l1, l2, l3 = l1_ref[rows, :], l2_ref[rows, :], l3_ref[rows, :]
        m = jnp.maximum(jnp.maximum(l1, l2), l3)
        e1, e2, e3 = jnp.exp(l1 - m), jnp.exp(l2 - m), jnp.exp(l3 - m)
        tot = e1 + e2 + e3
        w1, w2, w3 = e1 / tot, e2 / tot, e3 / tot
        parts = []
        for h in range(N_HEADS):
            hs = slice(h * DIL_HD, (h + 1) * DIL_HD)
            ls = slice(h * lanes_per_head, h * lanes_per_head + 1)
            parts.append(w1[:, ls] * o1_ref[rows, hs].astype(F32) + w2[:, ls] * o2_ref[rows, hs].astype(F32)
                         + w3[:, ls] * o3_ref[rows, hs].astype(F32))
        oc = jnp.concatenate(parts, axis=-1).astype(BF16)
        x = x_ref[rows, :]
        xb = x.astype(BF16)
        merged = None
        for j, o in enumerate((oa_ref[rows, :], ob_ref[rows, :], oc)):
            gate = _sigmoid(jnp.dot(xb, wg_ref[:, j * d:(j + 1) * d], preferred_element_type=F32)
                            + bg_ref[:, j * d:(j + 1) * d])
            term = gate * jnp.dot(o, wb_ref[j], preferred_element_type=F32)
            merged = term if merged is None else merged + term
        hmix = jnp.dot(merged.astype(BF16), wo_ref[...], preferred_element_type=F32)
        y_ref[rows, :] = _layer_norm(alpha * x + hmix, lg_ref[...], lb_ref[...])


def _merge(x, oa, ob, o_dil, l_dil, w_packed, b_gate, w_branch, w_out, ln_g, ln_b, layer, alpha, tm):
    t, d = x.shape

    def rows(width):
        return pl.BlockSpec((tm, width), lambda i: (i, 0))

    params = (b_gate, w_branch, w_out, ln_g, ln_b)
    return pl.pallas_call(
        functools.partial(_merge_kernel, alpha=alpha),
        grid=(t // tm,),
        in_specs=[rows(d)] + [rows(BRANCH_W)] * 5 + [rows(LANES)] * 3
                 + [_cols(w_packed, layer, 3 * d, W_GATE_COL)] + [_layer(p, layer) for p in params],
        out_specs=rows(d),
        out_shape=jax.ShapeDtypeStruct((t, d), F32),
        compiler_params=pltpu.CompilerParams(dimension_semantics=("parallel",), vmem_limit_bytes=VMEM_LIMIT),
        name="merge_out_ln",
    )(x, oa, ob, *o_dil, *l_dil, w_packed, *params)


def _ffn_kernel(x_ref, wg_ref, wu_ref, wd_ref, lg_ref, lb_ref, y_ref, act_ref, *, alpha):
    d_ff = wg_ref.shape[1]
    tm = x_ref.shape[0]
    for r0 in range(0, tm, tm // FF_ROW_SPLIT):
        rows = slice(r0, r0 + tm // FF_ROW_SPLIT)
        x = x_ref[rows, :]
        xb = x.astype(BF16)
        for j0 in range(0, d_ff, FF_CHUNK):
            cs = slice(j0, min(j0 + FF_CHUNK, d_ff))
            g = jnp.dot(xb, wg_ref[:, cs], preferred_element_type=F32)
            u = jnp.dot(xb, wu_ref[:, cs], preferred_element_type=F32)
            act_ref[rows, cs] = (g * _sigmoid(g) * u).astype(BF16)
        f = jnp.dot(act_ref[rows, :], wd_ref[...], preferred_element_type=F32)
        y_ref[rows, :] = _layer_norm(alpha * x + f, lg_ref[...], lb_ref[...])


def _ffn(x, w_gate, w_up, w_down, ln_g, ln_b, layer, alpha, tm):
    t, d = x.shape
    d_ff = w_gate.shape[2]
    rows = pl.BlockSpec((tm, d), lambda i: (i, 0))
    params = (w_gate, w_up, w_down, ln_g, ln_b)
    return pl.pallas_call(
        functools.partial(_ffn_kernel, alpha=alpha),
        grid=(t // tm,),
        in_specs=[rows] + [_layer(p, layer) for p in params],
        out_specs=rows,
        out_shape=jax.ShapeDtypeStruct((t, d), F32),
        scratch_shapes=[pltpu.VMEM((tm, d_ff), BF16)],
        compiler_params=pltpu.CompilerParams(dimension_semantics=("parallel",), vmem_limit_bytes=VMEM_LIMIT),
        name="swiglu_ln",
    )(x, *params)


def _rope_tables(seq):
    pos = np.arange(seq, dtype=np.float32)

    def tables(n_rot, base, width):
        half = n_rot // 2
        inv = np.float32(base) ** (-np.arange(half, dtype=np.float32) * np.float32(2.0) / np.float32(n_rot))
        ang = (pos[:, None] * inv[None, :]).astype(np.float32)
        cos = np.concatenate([np.cos(ang), np.cos(ang), np.ones((seq, width - n_rot), np.float32)], axis=1)
        sin = np.concatenate([-np.sin(ang), np.sin(ang), np.zeros((seq, width - n_rot), np.float32)], axis=1)
        return cos.astype(np.float32), sin.astype(np.float32)

    rc, rs = tables(RET_DK, RET_ROPE_BASE, RET_DK)
    dc, ds = tables(ROPE_DIMS, ROPE_THETA, DIL_HD)
    dil_tabs = []
    for _, dil in DIL_GROUPS:
        for tab in (dc, ds):
            dil_tabs.append(np.ascontiguousarray(tab.reshape(seq // dil, dil, DIL_HD).transpose(1, 0, 2)))
    return (np.tile(rc, (1, LANES // RET_DK)), np.tile(rs, (1, LANES // RET_DK))), dil_tabs


def _pack_w_in(w_in):
    d = w_in.shape[1]
    widths = (256, 256, 512, 512, GLA_RANK, 256, 256, 512, 512, 1536, 1536, 1536, 3 * d)
    ga0 = sum(widths[:4])
    dil0 = sum(widths[:9])
    gate0 = sum(widths[:12])
    w = w_in.astype(BF16)
    packed = jnp.concatenate(
        [w[:, :, :ga0], w[:, :, ga0 + GLA_RANK:dil0], w[:, :, gate0:], w[:, :, dil0:gate0],
         jnp.pad(w[:, :, ga0:ga0 + GLA_RANK], ((0, 0), (0, 0), (0, GA_PAD - GLA_RANK)))], axis=2)
    assert packed.shape[2] == W_GA_COL + GA_PAD
    return packed


def kernel(x, w_in, w_gla_a2, b_gla_a, gla_norm_g, ret_norm_g, w_branch, b_gate, w_out, ln1_g, ln1_b,
           w_ffn_gate, w_ffn_up, w_ffn_down, ln2_g, ln2_b):
    batch, seq, d = x.shape
    depth = w_in.shape[0]
    alpha = (2 * depth) ** 0.25
    ret_tabs, dil_tabs = _rope_tables(seq)
    w_packed = _pack_w_in(w_in)
    w_a2 = jnp.pad(w_gla_a2, ((0, 0), (0, GA_PAD - GLA_RANK), (0, 0))).astype(BF16)
    w_branch, w_out = w_branch.astype(BF16), w_out.astype(BF16)
    w_ffn_gate, w_ffn_up, w_ffn_down = w_ffn_gate.astype(BF16), w_ffn_up.astype(BF16), w_ffn_down.astype(BF16)
    b_gla_a, gla_norm_g, ret_norm_g, b_gate, ln1_g, ln1_b, ln2_g, ln2_b = (
        p[:, None, :] for p in (b_gla_a, gla_norm_g, ret_norm_g, b_gate, ln1_g, ln1_b, ln2_g, ln2_b))
    xf = x.reshape(batch * seq, d)
    for l in range(depth):
        oa, ob, *p_dil = _proj(xf, w_packed, (w_a2, b_gla_a, gla_norm_g, ret_norm_g), l, ret_tabs, dil_tabs,
                               batch, seq)
        o_dil, l_dil = [], []
        for pg, (_, dilation) in zip(p_dil, DIL_GROUPS):
            o, lse = _dil(pg, dilation, batch, seq)
            o_dil.append(o)
            l_dil.append(lse)
        xf = _merge(xf, oa, ob, o_dil, l_dil, w_packed, b_gate, w_branch, w_out, ln1_g, ln1_b, l, alpha,
                    MERGE_ROW_TILE)
        xf = _ffn(xf, w_ffn_gate, w_ffn_up, w_ffn_down, ln2_g, ln2_b, l, alpha, FF_ROW_TILE)
    return xf.reshape(batch, seq, d)
```

```python
import functools
import math

import jax
import jax.numpy as jnp
import numpy as np
from jax import lax
from jax.experimental import pallas as pl
from jax.experimental.pallas import tpu as pltpu

F32 = jnp.float32
BF16 = jnp.bfloat16

LANES = 128
N_HEADS = 4
GLA_DK = 64
GLA_DV = 128
GLA_RANK = 16
GLA_TAU = 16.0
GLA_CHUNK = 64
GLA_GROUP = 256
RET_DK = 64
RET_DV = 128
RET_CHUNK = 128
RET_GROUP = 256
RET_ROPE_BASE = 10000.0
DIL_HD = 128
DIL_GROUPS = ((128, 1), (512, 4), (2048, 16))
DIL_SPAN = 128
DIL_BLOCK = DIL_SPAN * max(d for _, d in DIL_GROUPS)
ROPE_THETA = 500000.0
ROPE_DIMS = 32
BRANCH_W = 512
GA_PAD = 256
GA_COL = 3072
RET_QK_COL = 1536
P1_GLA_COLS = (0, 256, 512, 1024)
P1_RET_COLS = (1536, 1792, 2048, 2560)
W_MAIN_COL = 0
W_GATE_COL = 3072
W_DIL_COL = 6144
W_GA_COL = 10752
MASK_VALUE = -1e30

VMEM_LIMIT = 56 * 1024 * 1024
SEQ_BLOCK = 512
FF_CHUNK = 256
FF_ROW_TILE = 1024
FF_ROW_SPLIT = 4
MERGE_ROW_TILE = 1024
MERGE_ROW_SPLIT = 4
DIL_UNROLL = 4


def _sigmoid(x):
    return 0.5 * (jnp.tanh(0.5 * x) + 1.0)


def _twice_sigmoid_of_twice(half_x):
    return jnp.tanh(half_x) + 1.0


def _nt_dot(a, b):
    return lax.dot_general(a, b, (((1,), (1,)), ((), ())), preferred_element_type=F32)


def _tn_dot(a, b):
    return lax.dot_general(a, b, (((0,), (0,)), ((), ())), preferred_element_type=F32)


def _layer(stacked, layer):
    rest = stacked.shape[1:]
    return pl.BlockSpec((None,) + rest, lambda *_: (layer,) + (0,) * len(rest), pipeline_mode=pl.Buffered(1))


def _rotary(x, cos, sin, head_dim, half):
    width = x.shape[1]
    low = (lax.broadcasted_iota(jnp.int32, x.shape, 1) % head_dim) < half
    partner = jnp.where(low, pltpu.roll(x, width - half, 1), pltpu.roll(x, half, 1))
    return x * cos + partner * sin


def _proj_kernel(*refs):
    n_groups = len(DIL_GROUPS)
    x_ref, wm_ref, wq_ref, wk_ref, wv_ref, wa_ref, rc_ref, rs_ref = refs[:8]
    tabs = refs[8:8 + 2 * n_groups]
    wa2_ref, ba_ref, gg_ref, rg_ref = refs[8 + 2 * n_groups:12 + 2 * n_groups]
    oa_ref, ob_ref = refs[12 + 2 * n_groups:14 + 2 * n_groups]
    g_refs = refs[14 + 2 * n_groups:14 + 3 * n_groups]
    (slab_ref, slab4_ref, xp4_ref, xp16_ref, p1_ref,
     gst_ref, gqs_ref, gkd_ref, gacc_ref, rst_ref, rqd_ref, rkd_ref, racc_ref) = refs[14 + 3 * n_groups:]
    tm, k = x_ref.shape

    @pl.when(pl.program_id(1) == 0)
    def _():
        gst_ref[...] = jnp.zeros_like(gst_ref)
        rst_ref[...] = jnp.zeros_like(rst_ref)

    n_slabs = k // LANES
    tn = BRANCH_W
    x = x_ref[...].astype(BF16)
    lane = lax.broadcasted_iota(jnp.int32, (1, tn), 1)
    for j0 in range(0, GA_COL, tn):
        res = jnp.dot(x, wm_ref[:, j0:j0 + tn], preferred_element_type=F32)
        if j0 == 0:
            res = res * jnp.where(lane < N_HEADS * GLA_DK, GLA_DK ** -0.5, 1.0)
        if j0 == RET_QK_COL:
            cos = jnp.concatenate([rc_ref[...]] * (tn // LANES), axis=1)
            sin = jnp.concatenate([rs_ref[...]] * (tn // LANES), axis=1)
            res = _rotary(res, cos, sin, RET_DK, RET_DK // 2)
            res = res * jnp.where(lane < N_HEADS * RET_DK, 1.0, RET_DK ** -0.5)
        p1_ref[:, j0:j0 + tn] = res.astype(BF16)
    p1_ref[:, GA_COL:] = jnp.dot(x, wa_ref[...], preferred_element_type=F32).astype(BF16)
    _gla_body(p1_ref, wa2_ref, ba_ref, gg_ref, oa_ref, gst_ref, gqs_ref, gkd_ref, gacc_ref)
    _ret_body(p1_ref, rg_ref, ob_ref, rst_ref, rqd_ref, rkd_ref, racc_ref)
    q4 = tm // 4
    q16 = tm // 16
    for j in range(n_slabs):
        slab_ref[j] = x_ref[:, j * LANES:(j + 1) * LANES]
    for c in range(4):
        for j in range(n_slabs):
            part = slab_ref[j, pl.ds(c, q4, stride=4), :]
            slab4_ref[j, c * q4:(c + 1) * q4, :] = part
            xp4_ref[c * q4:(c + 1) * q4, j * LANES:(j + 1) * LANES] = part.astype(BF16)
    for c0 in range(4):
        for c1 in range(4):
            c = 4 * c1 + c0
            for j in range(n_slabs):
                xp16_ref[c * q16:(c + 1) * q16, j * LANES:(j + 1) * LANES] = (
                    slab4_ref[j, pl.ds(c0 * q4 + c1, q16, stride=4), :].astype(BF16))
    lhs_of = {1: lambda: x, 4: lambda: xp4_ref[...], 16: lambda: xp16_ref[...]}
    for g, (_, dil) in enumerate(DIL_GROUPS):
        rows = tm // dil
        lhs = lhs_of[dil]()
        cos = jnp.concatenate([tabs[2 * g][...].reshape(tm, LANES)] * N_HEADS, axis=1)
        sin = jnp.concatenate([tabs[2 * g + 1][...].reshape(tm, LANES)] * N_HEADS, axis=1)
        for part, w_ref in enumerate((wq_ref, wk_ref, wv_ref)):
            res = jnp.dot(lhs, w_ref[:, g * tn:(g + 1) * tn], preferred_element_type=F32)
            if part < 2:
                res = _rotary(res, cos, sin, DIL_HD, ROPE_DIMS // 2)
            if part == 0:
                res = res * (DIL_HD ** -0.5 * math.log2(math.e))
            g_refs[g][:, :, part * tn:(part + 1) * tn] = res.astype(BF16).reshape(dil, rows, tn)


def _cols(w, layer, width, offset):
    assert offset % width == 0
    return pl.BlockSpec((None, w.shape[1], width), lambda *_: (layer, 0, offset // width),
                        pipeline_mode=pl.Buffered(1))


def _proj(x, w, mixer_params, layer, ret_tabs, dil_tabs, batch, seq):
    assert tuple(d for _, d in DIL_GROUPS) == (1, 4, 16)
    t, k = x.shape
    tm = SEQ_BLOCK
    nblk = seq // tm
    w_specs = [_cols(w, layer, GA_COL, W_MAIN_COL)]
    w_specs += [_cols(w, layer, 3 * BRANCH_W, W_DIL_COL + part * 3 * BRANCH_W) for part in range(3)]
    w_specs += [_cols(w, layer, GA_PAD, W_GA_COL)]

    def row(b, i):
        return (b * nblk + i, 0)

    tab_specs, out_specs, out_shapes = [], [], []
    for _, dil in DIL_GROUPS:
        tab_specs += [pl.BlockSpec((dil, tm // dil, LANES), lambda b, i: (0, i, 0))] * 2
        out_specs.append(pl.BlockSpec((None, dil, tm // dil, 3 * BRANCH_W), lambda b, i: (b, 0, i, 0)))
        out_shapes.append(jax.ShapeDtypeStruct((batch, dil, seq // dil, 3 * BRANCH_W), BF16))
    ret_spec = pl.BlockSpec((tm, LANES), lambda b, i: (i, 0))
    branch = pl.BlockSpec((tm, BRANCH_W), row)
    state = pltpu.VMEM((N_HEADS * GLA_DV, N_HEADS * GLA_DK), F32)
    decayed = pltpu.VMEM((tm, N_HEADS * GLA_DK), BF16)
    acc = pltpu.VMEM((tm, BRANCH_W), F32)
    return pl.pallas_call(
        _proj_kernel,
        grid=(batch, nblk),
        in_specs=[pl.BlockSpec((tm, k), row)] + w_specs + [ret_spec, ret_spec] + tab_specs
                 + [_layer(p, layer) for p in mixer_params],
        out_specs=[branch, branch] + out_specs,
        out_shape=[jax.ShapeDtypeStruct((t, BRANCH_W), BF16)] * 2 + out_shapes,
        scratch_shapes=[pltpu.VMEM((k // LANES, tm, LANES), F32), pltpu.VMEM((k // LANES, tm, LANES), F32),
                        pltpu.VMEM((tm, k), BF16), pltpu.VMEM((tm, k), BF16),
                        pltpu.VMEM((tm, GA_COL + GA_PAD), BF16),
                        state, decayed, decayed, acc, state, decayed, decayed, acc],
        compiler_params=pltpu.CompilerParams(dimension_semantics=("parallel", "arbitrary"),
                                             vmem_limit_bytes=VMEM_LIMIT),
        name="in_proj_mixers",
    )(x, *([w] * 5), *ret_tabs, *dil_tabs, *mixer_params)


def _state_step(p1_ref, rows, v_col, st_ref, q_ref, kd_ref, acc_ref, decay, st_diag):
    st = st_ref[...]
    acc_ref[rows, :] += _nt_dot(q_ref[rows, :], st.astype(BF16))
    kv_t = _tn_dot(p1_ref[rows, v_col:v_col + BRANCH_W], kd_ref[rows, :])
    st_ref[...] = st * decay + jnp.where(st_diag, kv_t, 0.0)


def _gla_body(p1_ref, wa_ref, ba_ref, g_ref, o_ref, st_ref, qs_ref, kd_ref, acc_ref):
    c = GLA_CHUNK
    grp = GLA_GROUP
    tb = p1_ref.shape[0]
    width = N_HEADS * GLA_DK
    q_col, k_col, v_col, r_col = P1_GLA_COLS

    z = jnp.dot(p1_ref[:, GA_COL:GA_COL + GA_PAD], wa_ref[...], preferred_element_type=F32) + ba_ref[...]
    log_a = (jnp.minimum(z, 0.0) - jnp.log1p(jnp.exp(-jnp.abs(z)))) * (1.0 / GLA_TAU)
    hi = log_a.astype(BF16)
    lo = (log_a - hi.astype(F32)).astype(BF16)
    hl = jnp.concatenate([hi, lo], axis=1)

    row = lax.broadcasted_iota(jnp.int32, (grp, grp), 0)
    col = lax.broadcasted_iota(jnp.int32, (grp, grp), 1)
    causal = ((row // c) == (col // c)) & (row >= col)
    cum = jnp.where(causal, 1.0, 0.0).astype(BF16)
    lane_head = lax.broadcasted_iota(jnp.int32, (grp, width), 1) // GLA_DK
    st_diag = (lax.broadcasted_iota(jnp.int32, st_ref.shape, 0) // GLA_DV
               == lax.broadcasted_iota(jnp.int32, st_ref.shape, 1) // GLA_DK)

    decays = []
    for g0 in range(0, tb, grp):
        rows = slice(g0, g0 + grp)
        r2 = jnp.dot(cum, hl[rows], preferred_element_type=F32)
        b = r2[:, :width] + r2[:, width:]
        b_tot = jnp.concatenate([jnp.broadcast_to(b[n0 + c - 1:n0 + c, :], (c, width)) for n0 in range(0, grp, c)],
                                axis=0)
        q = p1_ref[rows, q_col:q_col + width].astype(F32)
        k = p1_ref[rows, k_col:k_col + width].astype(F32)
        q_s = (q * jnp.exp(b)).astype(BF16)
        k_s = k * jnp.exp(-b)
        qs_ref[rows, :] = q_s
        kd_ref[rows, :] = (k * jnp.exp(b_tot - b)).astype(BF16)
        for n0 in range(0, grp, c):
            decays.append(jnp.exp(b_tot[n0:n0 + 1, :]))
        for h in range(N_HEADS):
            hs = slice(h * GLA_DV, (h + 1) * GLA_DV)
            k_h = jnp.where(lane_head == h, k_s, 0.0).astype(BF16)
            att = jnp.where(causal, _nt_dot(q_s, k_h), 0.0).astype(BF16)
            v_h = p1_ref[rows, v_col + h * GLA_DV:v_col + (h + 1) * GLA_DV]
            acc_ref[rows, hs] = jnp.dot(att, v_h, preferred_element_type=F32)

    for n, decay in enumerate(decays):
        rows = slice(n * c, (n + 1) * c)
        _state_step(p1_ref, rows, v_col, st_ref, qs_ref, kd_ref, acc_ref, decay, st_diag)

    outs = []
    for h in range(N_HEADS):
        o = acc_ref[:, h * GLA_DV:(h + 1) * GLA_DV]
        outs.append(o * lax.rsqrt(jnp.mean(o * o, axis=-1, keepdims=True) + 1e-6))
    y = jnp.concatenate(outs, axis=-1) * g_ref[...]
    r = p1_ref[:, r_col:r_col + BRANCH_W].astype(F32)
    o_ref[...] = (y * (r * _sigmoid(r))).astype(o_ref.dtype)


def _ret_body(p1_ref, g_ref, o_ref, st_ref, qd_ref, kd_ref, acc_ref):
    c = RET_CHUNK
    grp = RET_GROUP
    tb = p1_ref.shape[0]
    width = N_HEADS * RET_DK
    q_col, k_col, v_col, r_col = P1_RET_COLS

    log_g = [math.log1p(-(2.0 ** (-5.0 - h))) for h in range(N_HEADS)]
    lane_head = lax.broadcasted_iota(jnp.int32, (grp, width), 1) // RET_DK
    lg_lane = jnp.zeros((grp, width), F32)
    for h in range(N_HEADS):
        lg_lane = jnp.where(lane_head == h, log_g[h], lg_lane)
    ridx = (lax.broadcasted_iota(jnp.int32, (grp, width), 0) % c).astype(F32)
    q_scale = jnp.exp((ridx + 1.0) * lg_lane)
    k_scale = jnp.exp((c - 1.0 - ridx) * lg_lane)
    chunk_decay = jnp.exp(c * lg_lane[0:1, :])
    row = lax.broadcasted_iota(jnp.int32, (grp, grp), 0)
    col = lax.broadcasted_iota(jnp.int32, (grp, grp), 1)
    causal = ((row // c) == (col // c)) & (row >= col)
    diff = jnp.maximum(row - col, 0).astype(F32)
    dmats = [jnp.where(causal, jnp.exp(diff * log_g[h]), 0.0) for h in range(N_HEADS)]
    st_diag = (lax.broadcasted_iota(jnp.int32, st_ref.shape, 0) // RET_DV
               == lax.broadcasted_iota(jnp.int32, st_ref.shape, 1) // RET_DK)

    for g0 in range(0, tb, grp):
        rows = slice(g0, g0 + grp)
        qb = p1_ref[rows, q_col:q_col + width]
        q = qb.astype(F32)
        k = p1_ref[rows, k_col:k_col + width].astype(F32)
        qd_ref[rows, :] = (q * q_scale).astype(BF16)
        kd_ref[rows, :] = (k * k_scale).astype(BF16)
        for h in range(N_HEADS):
            hs = slice(h * RET_DV, (h + 1) * RET_DV)
            k_h = jnp.where(lane_head == h, k, 0.0).astype(BF16)
            att = (_nt_dot(qb, k_h) * dmats[h]).astype(BF16)
            v_h = p1_ref[rows, v_col + h * RET_DV:v_col + (h + 1) * RET_DV]
            acc_ref[rows, hs] = jnp.dot(att, v_h, preferred_element_type=F32)

    for n in range(tb // c):
        rows = slice(n * c, (n + 1) * c)
        _state_step(p1_ref, rows, v_col, st_ref, qd_ref, kd_ref, acc_ref, chunk_decay, st_diag)

    outs = []
    for h in range(N_HEADS):
        o = acc_ref[:, h * RET_DV:(h + 1) * RET_DV]
        mu = jnp.mean(o, axis=-1, keepdims=True)
        d = o - mu
        outs.append(d * lax.rsqrt(jnp.mean(d * d, axis=-1, keepdims=True) + 1e-5))
    y = jnp.concatenate(outs, axis=-1) * g_ref[...]
    r = p1_ref[:, r_col:r_col + BRANCH_W].astype(F32)
    o_ref[...] = (y * (r * _sigmoid(r))).astype(o_ref.dtype)


def _dil_kernel(q_ref, k_ref, v_ref, kp_ref, vp_ref, o_ref, l_ref, o_scr, l_scr, *, dilation):
    n = DIL_SPAN
    nj = q_ref.shape[1] // n
    qry = lax.broadcasted_iota(jnp.int32, (n, 2 * n), 0)
    key = lax.broadcasted_iota(jnp.int32, (n, 2 * n), 1)
    dist = qry + n - key
    bias_t = (dist >= 0) & (dist <= n)
    bias_t_first = bias_t & ((key >= n) | (pl.program_id(1) > 0))
    ones = jnp.ones((2 * n, n), BF16)
    lane_head = lax.broadcasted_iota(jnp.int32, (n, LANES), 1) // (LANES // N_HEADS)

    def unit(c, j, bias, keys, values):
        start = j * (n * dilation) + c
        dst = pl.ds(start, n, stride=dilation) if dilation > 1 else pl.ds(start, n)
        q = q_ref[c, pl.ds(j * n, n), :]
        lse_tile = jnp.zeros((n, LANES), F32)
        for h in range(N_HEADS):
            hs = slice(h * DIL_HD, (h + 1) * DIL_HD)
            s = jnp.where(bias, _nt_dot(q[:, hs], keys(hs)), MASK_VALUE)
            m = jnp.max(s, axis=-1, keepdims=True)
            p = jnp.exp2(s - m)
            o_den = jnp.dot(p.astype(BF16), jnp.concatenate([values(hs), ones], axis=1),
                            preferred_element_type=F32)
            den = o_den[:, DIL_HD:]
            o_scr[h, dst, :] = o_den[:, :DIL_HD] / den
            lse_tile = jnp.where(lane_head == h, m * math.log(2.0) + jnp.log(den), lse_tile)
        l_scr[dst, :] = lse_tile

    def first_unit(c, carry):
        unit(c, 0, bias_t_first,
             lambda hs: jnp.concatenate([kp_ref[c, :, hs], k_ref[c, pl.ds(0, n), hs]], axis=0),
             lambda hs: jnp.concatenate([vp_ref[c, :, hs], v_ref[c, pl.ds(0, n), hs]], axis=0))
        return carry

    def later_unit(u, carry):
        c = u // (nj - 1)
        j = 1 + u % (nj - 1)
        r0 = pl.multiple_of((j - 1) * n, n)
        unit(c, j, bias_t,
             lambda hs: k_ref[c, pl.ds(r0, 2 * n), hs],
             lambda hs: v_ref[c, pl.ds(r0, 2 * n), hs])
        return carry

    lax.fori_loop(0, dilation, first_unit, 0, unroll=min(dilation, DIL_UNROLL))
    if nj > 1:
        lax.fori_loop(0, dilation * (nj - 1), later_unit, 0, unroll=DIL_UNROLL)
    for h in range(N_HEADS):
        o_ref[:, h * DIL_HD:(h + 1) * DIL_HD] = o_scr[h].astype(o_ref.dtype)
    l_ref[...] = l_scr[...]


def _dil(pg, dilation, batch, seq):
    t = batch * seq
    blk = DIL_BLOCK
    nblk = seq // blk
    rows = blk // dilation
    prev_per_blk = rows // DIL_SPAN

    def cur(part):
        return pl.BlockSpec((None, dilation, rows, BRANCH_W), lambda b, i: (b, 0, i, part))

    def prev(part):
        return pl.BlockSpec((None, dilation, DIL_SPAN, BRANCH_W),
                            lambda b, i: (b, 0, jnp.maximum(i * prev_per_blk - 1, 0), part))

    return pl.pallas_call(
        functools.partial(_dil_kernel, dilation=dilation),
        grid=(batch, nblk),
        in_specs=[cur(0), cur(1), cur(2), prev(1), prev(2)],
        out_specs=[pl.BlockSpec((blk, BRANCH_W), lambda b, i: (b * nblk + i, 0)),
                   pl.BlockSpec((blk, LANES), lambda b, i: (b * nblk + i, 0))],
        out_shape=[jax.ShapeDtypeStruct((t, BRANCH_W), BF16), jax.ShapeDtypeStruct((t, LANES), F32)],
        scratch_shapes=[pltpu.VMEM((N_HEADS, blk, DIL_HD), F32), pltpu.VMEM((blk, LANES), F32)],
        compiler_params=pltpu.CompilerParams(dimension_semantics=("parallel", "arbitrary"),
                                             vmem_limit_bytes=VMEM_LIMIT),
        name=f"dilated_attn_{dilation}",
    )(pg, pg, pg, pg, pg)


def _layer_norm(v, g, b):
    mu = jnp.mean(v, axis=-1, keepdims=True)
    d = v - mu
    return d * lax.rsqrt(jnp.mean(d * d, axis=-1, keepdims=True) + 1e-5) * g + b


def _merge_kernel(x_ref, oa_ref, ob_ref, o1_ref, o2_ref, o3_ref, l1_ref, l2_ref, l3_ref,
                  wg_ref, bg_ref, wb_ref, wo_ref, lg_ref, lb_ref, y_ref, *, alpha):
    tm, d = x_ref.shape
    lanes_per_head = LANES // N_HEADS
    for r0 in range(0, tm, tm // MERGE_ROW_SPLIT):
        rows = slice(r0, r0 + tm // MERGE_ROW_SPLIT)
        l1, l2, l3 = l1_ref[rows, :], l2_ref[rows, :], l3_ref[rows, :]
        m = jnp.maximum(jnp.maximum(l1, l2), l3)
        e1, e2, e3 = jnp.exp(l1 - m), jnp.exp(l2 - m), jnp.exp(l3 - m)
        inv = 1.0 / (e1 + e2 + e3)
        w1, w2, w3 = e1 * inv, e2 * inv, e3 * inv
        parts = []
        for h in range(N_HEADS):
            hs = slice(h * DIL_HD, (h + 1) * DIL_HD)
            ls = slice(h * lanes_per_head, h * lanes_per_head + 1)
            parts.append(w1[:, ls] * o1_ref[rows, hs].astype(F32) + w2[:, ls] * o2_ref[rows, hs].astype(F32)
                         + w3[:, ls] * o3_ref[rows, hs].astype(F32))
        oc = jnp.concatenate(parts, axis=-1).astype(BF16)
        x = x_ref[rows, :]
        xb = x.astype(BF16)
        merged = None
        for j, o in enumerate((oa_ref[rows, :], ob_ref[rows, :], oc)):
            gate = _twice_sigmoid_of_twice(jnp.dot(xb, wg_ref[:, j * d:(j + 1) * d], preferred_element_type=F32)
                                           + bg_ref[:, j * d:(j + 1) * d])
            term = gate * jnp.dot(o, wb_ref[j], preferred_element_type=F32)
            merged = term if merged is None else merged + term
        hmix = jnp.dot(merged.astype(BF16), wo_ref[...], preferred_element_type=F32)
        y_ref[rows, :] = _layer_norm(alpha * x + hmix, lg_ref[...], lb_ref[...])


def _merge(x, oa, ob, o_dil, l_dil, w_packed, b_gate, w_branch, w_out, ln_g, ln_b, layer, alpha, tm):
    t, d = x.shape

    def rows(width):
        return pl.BlockSpec((tm, width), lambda i: (i, 0))

    params = (b_gate, w_branch, w_out, ln_g, ln_b)
    return pl.pallas_call(
        functools.partial(_merge_kernel, alpha=alpha),
        grid=(t // tm,),
        in_specs=[rows(d)] + [rows(BRANCH_W)] * 5 + [rows(LANES)] * 3
                 + [_cols(w_packed, layer, 3 * d, W_GATE_COL)] + [_layer(p, layer) for p in params],
        out_specs=rows(d),
        out_shape=jax.ShapeDtypeStruct((t, d), F32),
        compiler_params=pltpu.CompilerParams(dimension_semantics=("parallel",), vmem_limit_bytes=VMEM_LIMIT),
        name="merge_out_ln",
    )(x, oa, ob, *o_dil, *l_dil, w_packed, *params)


def _ffn_kernel(x_ref, wg_ref, wu_ref, wd_ref, lg_ref, lb_ref, y_ref, act_ref, *, alpha):
    d_ff = wg_ref.shape[1]
    tm = x_ref.shape[0]
    for r0 in range(0, tm, tm // FF_ROW_SPLIT):
        rows = slice(r0, r0 + tm // FF_ROW_SPLIT)
        x = x_ref[rows, :]
        xb = x.astype(BF16)
        for j0 in range(0, d_ff, FF_CHUNK):
            cs = slice(j0, min(j0 + FF_CHUNK, d_ff))
            half_g = jnp.dot(xb, wg_ref[:, cs], preferred_element_type=F32)
            u = jnp.dot(xb, wu_ref[:, cs], preferred_element_type=F32)
            act_ref[rows, cs] = (half_g * _twice_sigmoid_of_twice(half_g) * u).astype(BF16)
        f = jnp.dot(act_ref[rows, :], wd_ref[...], preferred_element_type=F32)
        y_ref[rows, :] = _layer_norm(alpha * x + f, lg_ref[...], lb_ref[...])


def _ffn(x, w_gate, w_up, w_down, ln_g, ln_b, layer, alpha, tm):
    t, d = x.shape
    d_ff = w_gate.shape[2]
    rows = pl.BlockSpec((tm, d), lambda i: (i, 0))
    params = (w_gate, w_up, w_down, ln_g, ln_b)
    return pl.pallas_call(
        functools.partial(_ffn_kernel, alpha=alpha),
        grid=(t // tm,),
        in_specs=[rows] + [_layer(p, layer) for p in params],
        out_specs=rows,
        out_shape=jax.ShapeDtypeStruct((t, d), F32),
        scratch_shapes=[pltpu.VMEM((tm, d_ff), BF16)],
        compiler_params=pltpu.CompilerParams(dimension_semantics=("parallel",), vmem_limit_bytes=VMEM_LIMIT),
        name="swiglu_ln",
    )(x, *params)


def _rope_tables(seq):
    pos = np.arange(seq, dtype=np.float32)

    def tables(n_rot, base, width):
        half = n_rot // 2
        inv = np.float32(base) ** (-np.arange(half, dtype=np.float32) * np.float32(2.0) / np.float32(n_rot))
        ang = (pos[:, None] * inv[None, :]).astype(np.float32)
        cos = np.concatenate([np.cos(ang), np.cos(ang), np.ones((seq, width - n_rot), np.float32)], axis=1)
        sin = np.concatenate([-np.sin(ang), np.sin(ang), np.zeros((seq, width - n_rot), np.float32)], axis=1)
        return cos.astype(np.float32), sin.astype(np.float32)

    rc, rs = tables(RET_DK, RET_ROPE_BASE, RET_DK)
    dc, ds = tables(ROPE_DIMS, ROPE_THETA, DIL_HD)
    dil_tabs = []
    for _, dil in DIL_GROUPS:
        for tab in (dc, ds):
            dil_tabs.append(np.ascontiguousarray(tab.reshape(seq // dil, dil, DIL_HD).transpose(1, 0, 2)))
    return (np.tile(rc, (1, LANES // RET_DK)), np.tile(rs, (1, LANES // RET_DK))), dil_tabs


def _pack_w_in(w_in):
    d = w_in.shape[1]
    widths = (256, 256, 512, 512, GLA_RANK, 256, 256, 512, 512, 1536, 1536, 1536, 3 * d)
    ga0 = sum(widths[:4])
    dil0 = sum(widths[:9])
    gate0 = sum(widths[:12])
    w = w_in.astype(BF16)
    packed = jnp.concatenate(
        [w[:, :, :ga0], w[:, :, ga0 + GLA_RANK:dil0], w[:, :, gate0:] * 0.5, w[:, :, dil0:gate0],
         jnp.pad(w[:, :, ga0:ga0 + GLA_RANK], ((0, 0), (0, 0), (0, GA_PAD - GLA_RANK)))], axis=2)
    assert packed.shape[2] == W_GA_COL + GA_PAD
    return packed


def kernel(x, w_in, w_gla_a2, b_gla_a, gla_norm_g, ret_norm_g, w_branch, b_gate, w_out, ln1_g, ln1_b,
           w_ffn_gate, w_ffn_up, w_ffn_down, ln2_g, ln2_b):
    batch, seq, d = x.shape
    depth = w_in.shape[0]
    alpha = (2 * depth) ** 0.25
    ret_tabs, dil_tabs = _rope_tables(seq)
    w_packed = _pack_w_in(w_in)
    w_a2 = jnp.pad(w_gla_a2, ((0, 0), (0, GA_PAD - GLA_RANK), (0, 0))).astype(BF16)
    w_branch, w_out = (0.5 * w_branch).astype(BF16), w_out.astype(BF16)
    w_ffn_gate, w_ffn_up, w_ffn_down = (0.5 * w_ffn_gate).astype(BF16), w_ffn_up.astype(BF16), w_ffn_down.astype(BF16)
    b_gla_a, gla_norm_g, ret_norm_g, b_gate, ln1_g, ln1_b, ln2_g, ln2_b = (
        p[:, None, :] for p in (b_gla_a, gla_norm_g, ret_norm_g, 0.5 * b_gate, ln1_g, ln1_b, ln2_g, ln2_b))
    xf = x.reshape(batch * seq, d)
    for l in range(depth):
        oa, ob, *p_dil = _proj(xf, w_packed, (w_a2, b_gla_a, gla_norm_g, ret_norm_g), l, ret_tabs, dil_tabs,
                               batch, seq)
        o_dil, l_dil = [], []
        for pg, (_, dilation) in zip(p_dil, DIL_GROUPS):
            o, lse = _dil(pg, dilation, batch, seq)
            o_dil.append(o)
            l_dil.append(lse)
        xf = _merge(xf, oa, ob, o_dil, l_dil, w_packed, b_gate, w_branch, w_out, ln1_g, ln1_b, l, alpha,
                    MERGE_ROW_TILE)
        xf = _ffn(xf, w_ffn_gate, w_ffn_up, w_ffn_down, ln2_g, ln2_b, l, alpha, FF_ROW_TILE)
    return xf.reshape(batch, seq, d)
```

```python
import functools
import math

import jax
import jax.numpy as jnp
import numpy as np
from jax import lax
from jax.experimental import pallas as pl
from jax.experimental.pallas import tpu as pltpu

F32 = jnp.float32
BF16 = jnp.bfloat16

LANES = 128
N_HEADS = 4
GLA_DK = 64
GLA_DV = 128
GLA_RANK = 16
GLA_TAU = 16.0
GLA_CHUNK = 64
GLA_GROUP = 256
RET_DK = 64
RET_DV = 128
RET_CHUNK = 128
RET_GROUP = 256
RET_ROPE_BASE = 10000.0
DIL_HD = 128
DIL_GROUPS = ((128, 1), (512, 4), (2048, 16))
DIL_SPAN = 128
DIL_BLOCK = DIL_SPAN * max(d for _, d in DIL_GROUPS)
ROPE_THETA = 500000.0
ROPE_DIMS = 32
BRANCH_W = 512
GA_PAD = 256
GA_COL = 3072
RET_QK_COL = 1536
P1_GLA_COLS = (0, 256, 512, 1024)
P1_RET_COLS = (1536, 1792, 2048, 2560)
W_MAIN_COL = 0
W_GATE_COL = 3072
W_DIL_COL = 6144
W_GA_COL = 10752
MASK_VALUE = -1e30

VMEM_LIMIT = 56 * 1024 * 1024
SEQ_BLOCK = 512
FF_CHUNK = 256
FF_ROW_TILE = 1024
FF_ROW_SPLIT = 4
MERGE_ROW_TILE = 1024
MERGE_ROW_SPLIT = 4
DIL_UNROLL = 4


def _sigmoid(x):
    return 0.5 * (jnp.tanh(0.5 * x) + 1.0)


def _nt_dot(a, b):
    return lax.dot_general(a, b, (((1,), (1,)), ((), ())), preferred_element_type=F32)


def _tn_dot(a, b):
    return lax.dot_general(a, b, (((0,), (0,)), ((), ())), preferred_element_type=F32)


def _layer(stacked, layer):
    rest = stacked.shape[1:]
    return pl.BlockSpec((None,) + rest, lambda *_: (layer,) + (0,) * len(rest), pipeline_mode=pl.Buffered(1))


def _rotary(x, cos, sin, head_dim, half):
    width = x.shape[1]
    low = (lax.broadcasted_iota(jnp.int32, x.shape, 1) % head_dim) < half
    partner = jnp.where(low, pltpu.roll(x, width - half, 1), pltpu.roll(x, half, 1))
    return x * cos + partner * sin


def _proj_kernel(*refs):
    n_groups = len(DIL_GROUPS)
    x_ref, wm_ref, wq_ref, wk_ref, wv_ref, wa_ref, rc_ref, rs_ref = refs[:8]
    tabs = refs[8:8 + 2 * n_groups]
    wa2_ref, ba_ref, gg_ref, rg_ref = refs[8 + 2 * n_groups:12 + 2 * n_groups]
    ret_consts = refs[12 + 2 * n_groups:16 + 2 * n_groups]
    oa_ref, ob_ref = refs[16 + 2 * n_groups:18 + 2 * n_groups]
    g_refs = refs[18 + 2 * n_groups:18 + 3 * n_groups]
    (slab_ref, slab4_ref, xp4_ref, xp16_ref, p1_ref,
     gst_ref, gqs_ref, gkd_ref, gacc_ref, rst_ref, rqd_ref, rkd_ref, racc_ref) = refs[18 + 3 * n_groups:]
    tm, k = x_ref.shape

    @pl.when(pl.program_id(1) == 0)
    def _():
        gst_ref[...] = jnp.zeros_like(gst_ref)
        rst_ref[...] = jnp.zeros_like(rst_ref)

    n_slabs = k // LANES
    tn = BRANCH_W
    x = x_ref[...].astype(BF16)
    lane = lax.broadcasted_iota(jnp.int32, (1, tn), 1)
    for j0 in range(0, GA_COL, tn):
        res = jnp.dot(x, wm_ref[:, j0:j0 + tn], preferred_element_type=F32)
        if j0 == 0:
            res = res * jnp.where(lane < N_HEADS * GLA_DK, GLA_DK ** -0.5, 1.0)
        if j0 == RET_QK_COL:
            cos = jnp.concatenate([rc_ref[...]] * (tn // LANES), axis=1)
            sin = jnp.concatenate([rs_ref[...]] * (tn // LANES), axis=1)
            res = _rotary(res, cos, sin, RET_DK, RET_DK // 2)
            res = res * jnp.where(lane < N_HEADS * RET_DK, 1.0, RET_DK ** -0.5)
        p1_ref[:, j0:j0 + tn] = res.astype(BF16)
    p1_ref[:, GA_COL:] = jnp.dot(x, wa_ref[...], preferred_element_type=F32).astype(BF16)
    _gla_body(p1_ref, wa2_ref, ba_ref, gg_ref, oa_ref, gst_ref, gqs_ref, gkd_ref, gacc_ref)
    _ret_body(p1_ref, rg_ref, *ret_consts, ob_ref, rst_ref, rqd_ref, rkd_ref, racc_ref)
    q4 = tm // 4
    q16 = tm // 16
    for j in range(n_slabs):
        slab_ref[j] = x_ref[:, j * LANES:(j + 1) * LANES]
    for c in range(4):
        for j in range(n_slabs):
            part = slab_ref[j, pl.ds(c, q4, stride=4), :]
            slab4_ref[j, c * q4:(c + 1) * q4, :] = part
            xp4_ref[c * q4:(c + 1) * q4, j * LANES:(j + 1) * LANES] = part.astype(BF16)
    for c0 in range(4):
        for c1 in range(4):
            c = 4 * c1 + c0
            for j in range(n_slabs):
                xp16_ref[c * q16:(c + 1) * q16, j * LANES:(j + 1) * LANES] = (
                    slab4_ref[j, pl.ds(c0 * q4 + c1, q16, stride=4), :].astype(BF16))
    lhs_of = {1: lambda: x, 4: lambda: xp4_ref[...], 16: lambda: xp16_ref[...]}
    for g, (_, dil) in enumerate(DIL_GROUPS):
        rows = tm // dil
        lhs = lhs_of[dil]()
        cos = jnp.concatenate([tabs[2 * g][...].reshape(tm, LANES)] * N_HEADS, axis=1)
        sin = jnp.concatenate([tabs[2 * g + 1][...].reshape(tm, LANES)] * N_HEADS, axis=1)
        for part, w_ref in enumerate((wq_ref, wk_ref, wv_ref)):
            res = jnp.dot(lhs, w_ref[:, g * tn:(g + 1) * tn], preferred_element_type=F32)
            if part < 2:
                res = _rotary(res, cos, sin, DIL_HD, ROPE_DIMS // 2)
            if part == 0:
                res = res * (DIL_HD ** -0.5 * math.log2(math.e))
            g_refs[g][:, :, part * tn:(part + 1) * tn] = res.astype(BF16).reshape(dil, rows, tn)


def _cols(w, layer, width, offset):
    assert offset % width == 0
    return pl.BlockSpec((None, w.shape[1], width), lambda *_: (layer, 0, offset // width),
                        pipeline_mode=pl.Buffered(1))


def _proj(x, w, mixer_params, layer, ret_tabs, dil_tabs, batch, seq):
    assert tuple(d for _, d in DIL_GROUPS) == (1, 4, 16)
    t, k = x.shape
    tm = SEQ_BLOCK
    nblk = seq // tm
    w_specs = [_cols(w, layer, GA_COL, W_MAIN_COL)]
    w_specs += [_cols(w, layer, 3 * BRANCH_W, W_DIL_COL + part * 3 * BRANCH_W) for part in range(3)]
    w_specs += [_cols(w, layer, GA_PAD, W_GA_COL)]

    def row(b, i):
        return (b * nblk + i, 0)

    tab_specs, out_specs, out_shapes = [], [], []
    for _, dil in DIL_GROUPS:
        tab_specs += [pl.BlockSpec((dil, tm // dil, LANES), lambda b, i: (0, i, 0))] * 2
        out_specs.append(pl.BlockSpec((None, dil, tm // dil, 3 * BRANCH_W), lambda b, i: (b, 0, i, 0)))
        out_shapes.append(jax.ShapeDtypeStruct((batch, dil, seq // dil, 3 * BRANCH_W), BF16))
    ret_spec = pl.BlockSpec((tm, LANES), lambda b, i: (i, 0))
    branch = pl.BlockSpec((tm, BRANCH_W), row)
    state = pltpu.VMEM((N_HEADS * GLA_DV, N_HEADS * GLA_DK), F32)
    decayed = pltpu.VMEM((tm, N_HEADS * GLA_DK), BF16)
    acc = pltpu.VMEM((tm, BRANCH_W), F32)
    ret_consts = _ret_constants()
    const_specs = [pl.BlockSpec(a.shape, functools.partial(lambda *_, nd: (0,) * nd, nd=a.ndim),
                                pipeline_mode=pl.Buffered(1)) for a in ret_consts]
    return pl.pallas_call(
        _proj_kernel,
        grid=(batch, nblk),
        in_specs=[pl.BlockSpec((tm, k), row)] + w_specs + [ret_spec, ret_spec] + tab_specs
                 + [_layer(p, layer) for p in mixer_params] + const_specs,
        out_specs=[branch, branch] + out_specs,
        out_shape=[jax.ShapeDtypeStruct((t, BRANCH_W), BF16)] * 2 + out_shapes,
        scratch_shapes=[pltpu.VMEM((k // LANES, tm, LANES), F32), pltpu.VMEM((k // LANES, tm, LANES), F32),
                        pltpu.VMEM((tm, k), BF16), pltpu.VMEM((tm, k), BF16),
                        pltpu.VMEM((tm, GA_COL + GA_PAD), BF16),
                        state, decayed, decayed, acc, state, decayed, decayed, acc],
        compiler_params=pltpu.CompilerParams(dimension_semantics=("parallel", "arbitrary"),
                                             vmem_limit_bytes=VMEM_LIMIT),
        name="in_proj_mixers",
    )(x, *([w] * 5), *ret_tabs, *dil_tabs, *mixer_params, *ret_consts)


def _state_step(p1_ref, rows, v_col, st_ref, q_ref, kd_ref, acc_ref, decay, st_diag):
    st = st_ref[...]
    acc_ref[rows, :] += _nt_dot(q_ref[rows, :], st.astype(BF16))
    kv_t = _tn_dot(p1_ref[rows, v_col:v_col + BRANCH_W], kd_ref[rows, :])
    st_ref[...] = st * decay + jnp.where(st_diag, kv_t, 0.0)


def _gla_body(p1_ref, wa_ref, ba_ref, g_ref, o_ref, st_ref, qs_ref, kd_ref, acc_ref):
    c = GLA_CHUNK
    grp = GLA_GROUP
    tb = p1_ref.shape[0]
    width = N_HEADS * GLA_DK
    q_col, k_col, v_col, r_col = P1_GLA_COLS

    z = jnp.dot(p1_ref[:, GA_COL:GA_COL + GA_PAD], wa_ref[...], preferred_element_type=F32) + ba_ref[...]
    log_a = (jnp.minimum(z, 0.0) - jnp.log1p(jnp.exp(-jnp.abs(z)))) * (1.0 / GLA_TAU)
    hi = log_a.astype(BF16)
    lo = (log_a - hi.astype(F32)).astype(BF16)
    hl = jnp.concatenate([hi, lo], axis=1)

    row = lax.broadcasted_iota(jnp.int32, (grp, grp), 0)
    col = lax.broadcasted_iota(jnp.int32, (grp, grp), 1)
    causal = ((row // c) == (col // c)) & (row >= col)
    cum = jnp.where(causal, 1.0, 0.0).astype(BF16)
    lane_head = lax.broadcasted_iota(jnp.int32, (grp, width), 1) // GLA_DK
    st_diag = (lax.broadcasted_iota(jnp.int32, st_ref.shape, 0) // GLA_DV
               == lax.broadcasted_iota(jnp.int32, st_ref.shape, 1) // GLA_DK)

    decays = []
    for g0 in range(0, tb, grp):
        rows = slice(g0, g0 + grp)
        r2 = jnp.dot(cum, hl[rows], preferred_element_type=F32)
        b = r2[:, :width] + r2[:, width:]
        b_tot = jnp.concatenate([jnp.broadcast_to(b[n0 + c - 1:n0 + c, :], (c, width)) for n0 in range(0, grp, c)],
                                axis=0)
        q = p1_ref[rows, q_col:q_col + width].astype(F32)
        k = p1_ref[rows, k_col:k_col + width].astype(F32)
        q_s = (q * jnp.exp(b)).astype(BF16)
        k_s = k * jnp.exp(-b)
        qs_ref[rows, :] = q_s
        kd_ref[rows, :] = (k * jnp.exp(b_tot - b)).astype(BF16)
        for n0 in range(0, grp, c):
            decays.append(jnp.exp(b_tot[n0:n0 + 1, :]))
        for h in range(N_HEADS):
            hs = slice(h * GLA_DV, (h + 1) * GLA_DV)
            k_h = jnp.where(lane_head == h, k_s, 0.0).astype(BF16)
            att = jnp.where(causal, _nt_dot(q_s, k_h), 0.0).astype(BF16)
            v_h = p1_ref[rows, v_col + h * GLA_DV:v_col + (h + 1) * GLA_DV]
            acc_ref[rows, hs] = jnp.dot(att, v_h, preferred_element_type=F32)

    for n, decay in enumerate(decays):
        rows = slice(n * c, (n + 1) * c)
        _state_step(p1_ref, rows, v_col, st_ref, qs_ref, kd_ref, acc_ref, decay, st_diag)

    outs = []
    for h in range(N_HEADS):
        o = acc_ref[:, h * GLA_DV:(h + 1) * GLA_DV]
        outs.append(o * lax.rsqrt(jnp.mean(o * o, axis=-1, keepdims=True) + 1e-6))
    y = jnp.concatenate(outs, axis=-1) * g_ref[...]
    r = p1_ref[:, r_col:r_col + BRANCH_W].astype(F32)
    o_ref[...] = (y * (r * _sigmoid(r))).astype(o_ref.dtype)


def _ret_constants():
    c, grp, width = RET_CHUNK, RET_GROUP, N_HEADS * RET_DK
    log_g = np.array([math.log1p(-(2.0 ** (-5.0 - h))) for h in range(N_HEADS)], np.float32)
    row = np.arange(grp)[:, None]
    col = np.arange(grp)[None, :]
    causal = ((row // c) == (col // c)) & (row >= col)
    diff = np.maximum(row - col, 0).astype(np.float32)
    dmats = np.stack([np.where(causal, np.exp(diff * log_g[h]), 0.0) for h in range(N_HEADS)]).astype(np.float32)
    lg_lane = np.repeat(log_g, RET_DK)[None, :]
    ridx = (np.arange(grp) % c).astype(np.float32)[:, None]
    q_scale = np.exp((ridx + np.float32(1.0)) * lg_lane).astype(np.float32)
    k_scale = np.exp((np.float32(c - 1.0) - ridx) * lg_lane).astype(np.float32)
    chunk_decay = np.broadcast_to(np.exp(np.float32(c) * lg_lane), (8, width)).astype(np.float32)
    return dmats, q_scale, k_scale, chunk_decay


def _ret_body(p1_ref, g_ref, dm_ref, qsc_ref, ksc_ref, dec_ref, o_ref, st_ref, qd_ref, kd_ref, acc_ref):
    c = RET_CHUNK
    grp = RET_GROUP
    tb = p1_ref.shape[0]
    width = N_HEADS * RET_DK
    q_col, k_col, v_col, r_col = P1_RET_COLS

    lane_head = lax.broadcasted_iota(jnp.int32, (grp, width), 1) // RET_DK
    q_scale = qsc_ref[...]
    k_scale = ksc_ref[...]
    chunk_decay = dec_ref[0:1, :]
    st_diag = (lax.broadcasted_iota(jnp.int32, st_ref.shape, 0) // RET_DV
               == lax.broadcasted_iota(jnp.int32, st_ref.shape, 1) // RET_DK)

    for g0 in range(0, tb, grp):
        rows = slice(g0, g0 + grp)
        qb = p1_ref[rows, q_col:q_col + width]
        q = qb.astype(F32)
        k = p1_ref[rows, k_col:k_col + width].astype(F32)
        qd_ref[rows, :] = (q * q_scale).astype(BF16)
        kd_ref[rows, :] = (k * k_scale).astype(BF16)
        for h in range(N_HEADS):
            hs = slice(h * RET_DV, (h + 1) * RET_DV)
            k_h = jnp.where(lane_head == h, k, 0.0).astype(BF16)
            att = (_nt_dot(qb, k_h) * dm_ref[h]).astype(BF16)
            v_h = p1_ref[rows, v_col + h * RET_DV:v_col + (h + 1) * RET_DV]
            acc_ref[rows, hs] = jnp.dot(att, v_h, preferred_element_type=F32)

    for n in range(tb // c):
        rows = slice(n * c, (n + 1) * c)
        _state_step(p1_ref, rows, v_col, st_ref, qd_ref, kd_ref, acc_ref, chunk_decay, st_diag)

    outs = []
    for h in range(N_HEADS):
        o = acc_ref[:, h * RET_DV:(h + 1) * RET_DV]
        mu = jnp.mean(o, axis=-1, keepdims=True)
        d = o - mu
        outs.append(d * lax.rsqrt(jnp.mean(d * d, axis=-1, keepdims=True) + 1e-5))
    y = jnp.concatenate(outs, axis=-1) * g_ref[...]
    r = p1_ref[:, r_col:r_col + BRANCH_W].astype(F32)
    o_ref[...] = (y * (r * _sigmoid(r))).astype(o_ref.dtype)


def _dil_kernel(q_ref, k_ref, v_ref, kp_ref, vp_ref, o_ref, l_ref, o_scr, l_scr, *, dilation):
    n = DIL_SPAN
    nj = q_ref.shape[1] // n
    qry = lax.broadcasted_iota(jnp.int32, (n, 2 * n), 0)
    key = lax.broadcasted_iota(jnp.int32, (n, 2 * n), 1)
    dist = qry + n - key
    bias_t = (dist >= 0) & (dist <= n)
    bias_t_first = bias_t & ((key >= n) | (pl.program_id(1) > 0))
    ones = jnp.ones((2 * n, n), BF16)
    lane_head = lax.broadcasted_iota(jnp.int32, (n, LANES), 1) // (LANES // N_HEADS)

    def unit(c, j, bias, keys, values):
        start = j * (n * dilation) + c
        dst = pl.ds(start, n, stride=dilation) if dilation > 1 else pl.ds(start, n)
        q = q_ref[c, pl.ds(j * n, n), :]
        lse_tile = jnp.zeros((n, LANES), F32)
        for h in range(N_HEADS):
            hs = slice(h * DIL_HD, (h + 1) * DIL_HD)
            s = jnp.where(bias, _nt_dot(q[:, hs], keys(hs)), MASK_VALUE)
            m = jnp.max(s, axis=-1, keepdims=True)
            p = jnp.exp2(s - m)
            o_den = jnp.dot(p.astype(BF16), jnp.concatenate([values(hs), ones], axis=1),
                            preferred_element_type=F32)
            den = o_den[:, DIL_HD:]
            o_scr[h, dst, :] = o_den[:, :DIL_HD] / den
            lse_tile = jnp.where(lane_head == h, m * math.log(2.0) + jnp.log(den), lse_tile)
        l_scr[dst, :] = lse_tile

    def first_unit(c, carry):
        unit(c, 0, bias_t_first,
             lambda hs: jnp.concatenate([kp_ref[c, :, hs], k_ref[c, pl.ds(0, n), hs]], axis=0),
             lambda hs: jnp.concatenate([vp_ref[c, :, hs], v_ref[c, pl.ds(0, n), hs]], axis=0))
        return carry

    def later_unit(u, carry):
        c = u // (nj - 1)
        j = 1 + u % (nj - 1)
        r0 = pl.multiple_of((j - 1) * n, n)
        unit(c, j, bias_t,
             lambda hs: k_ref[c, pl.ds(r0, 2 * n), hs],
             lambda hs: v_ref[c, pl.ds(r0, 2 * n), hs])
        return carry

    lax.fori_loop(0, dilation, first_unit, 0, unroll=min(dilation, DIL_UNROLL))
    if nj > 1:
        lax.fori_loop(0, dilation * (nj - 1), later_unit, 0, unroll=DIL_UNROLL)
    for h in range(N_HEADS):
        o_ref[:, h * DIL_HD:(h + 1) * DIL_HD] = o_scr[h].astype(o_ref.dtype)
    l_ref[...] = l_scr[...]


def _dil(pg, dilation, batch, seq):
    t = batch * seq
    blk = DIL_BLOCK
    nblk = seq // blk
    rows = blk // dilation
    prev_per_blk = rows // DIL_SPAN

    def cur(part):
        return pl.BlockSpec((None, dilation, rows, BRANCH_W), lambda b, i: (b, 0, i, part))

    def prev(part):
        return pl.BlockSpec((None, dilation, DIL_SPAN, BRANCH_W),
                            lambda b, i: (b, 0, jnp.maximum(i * prev_per_blk - 1, 0), part))

    return pl.pallas_call(
        functools.partial(_dil_kernel, dilation=dilation),
        grid=(batch, nblk),
        in_specs=[cur(0), cur(1), cur(2), prev(1), prev(2)],
        out_specs=[pl.BlockSpec((blk, BRANCH_W), lambda b, i: (b * nblk + i, 0)),
                   pl.BlockSpec((blk, LANES), lambda b, i: (b * nblk + i, 0))],
        out_shape=[jax.ShapeDtypeStruct((t, BRANCH_W), BF16), jax.ShapeDtypeStruct((t, LANES), F32)],
        scratch_shapes=[pltpu.VMEM((N_HEADS, blk, DIL_HD), F32), pltpu.VMEM((blk, LANES), F32)],
        compiler_params=pltpu.CompilerParams(dimension_semantics=("parallel", "arbitrary"),
                                             vmem_limit_bytes=VMEM_LIMIT),
        name=f"dilated_attn_{dilation}",
    )(pg, pg, pg, pg, pg)


def _layer_norm(v, g, b):
    mu = jnp.mean(v, axis=-1, keepdims=True)
    d = v - mu
    return d * lax.rsqrt(jnp.mean(d * d, axis=-1, keepdims=True) + 1e-5) * g + b


def _merge_kernel(x_ref, oa_ref, ob_ref, o1_ref, o2_ref, o3_ref, l1_ref, l2_ref, l3_ref,
                  wg_ref, bg_ref, wb_ref, wo_ref, lg_ref, lb_ref, y_ref, *, alpha):
    tm, d = x_ref.shape
    lanes_per_head = LANES // N_HEADS
    for r0 in range(0, tm, tm // MERGE_ROW_SPLIT):
        rows = slice(r0, r0 + tm // MERGE_ROW_SPLIT)
        l1, l2, l3 = l1_ref[rows, :], l2_ref[rows, :], l3_ref[rows, :]
        m = jnp.maximum(jnp.maximum(l1, l2), l3)
        e1, e2, e3 = jnp.exp(l1 - m), jnp.exp(l2 - m), jnp.exp(l3 - m)
        tot = e1 + e2 + e3
        w1, w2, w3 = e1 / tot, e2 / tot, e3 / tot
        parts = []
        for h in range(N_HEADS):
            hs = slice(h * DIL_HD, (h + 1) * DIL_HD)
            ls = slice(h * lanes_per_head, h * lanes_per_head + 1)
            parts.append(w1[:, ls] * o1_ref[rows, hs].astype(F32) + w2[:, ls] * o2_ref[rows, hs].astype(F32)
                         + w3[:, ls] * o3_ref[rows, hs].astype(F32))
        oc = jnp.concatenate(parts, axis=-1).astype(BF16)
        x = x_ref[rows, :]
        xb = x.astype(BF16)
        merged = None
        for j, o in enumerate((oa_ref[rows, :], ob_ref[rows, :], oc)):
            gate = _sigmoid(jnp.dot(xb, wg_ref[:, j * d:(j + 1) * d], preferred_element_type=F32)
                            + bg_ref[:, j * d:(j + 1) * d])
            term = gate * jnp.dot(o, wb_ref[j], preferred_element_type=F32)
            merged = term if merged is None else merged + term
        hmix = jnp.dot(merged.astype(BF16), wo_ref[...], preferred_element_type=F32)
        y_ref[rows, :] = _layer_norm(alpha * x + hmix, lg_ref[...], lb_ref[...])


def _merge(x, oa, ob, o_dil, l_dil, w_packed, b_gate, w_branch, w_out, ln_g, ln_b, layer, alpha, tm):
    t, d = x.shape

    def rows(width):
        return pl.BlockSpec((tm, width), lambda i: (i, 0))

    params = (b_gate, w_branch, w_out, ln_g, ln_b)
    return pl.pallas_call(
        functools.partial(_merge_kernel, alpha=alpha),
        grid=(t // tm,),
        in_specs=[rows(d)] + [rows(BRANCH_W)] * 5 + [rows(LANES)] * 3
                 + [_cols(w_packed, layer, 3 * d, W_GATE_COL)] + [_layer(p, layer) for p in params],
        out_specs=rows(d),
        out_shape=jax.ShapeDtypeStruct((t, d), F32),
        compiler_params=pltpu.CompilerParams(dimension_semantics=("parallel",), vmem_limit_bytes=VMEM_LIMIT),
        name="merge_out_ln",
    )(x, oa, ob, *o_dil, *l_dil, w_packed, *params)


def _ffn_kernel(x_ref, wg_ref, wu_ref, wd_ref, lg_ref, lb_ref, y_ref, act_ref, *, alpha):
    d_ff = wg_ref.shape[1]
    tm = x_ref.shape[0]
    for r0 in range(0, tm, tm // FF_ROW_SPLIT):
        rows = slice(r0, r0 + tm // FF_ROW_SPLIT)
        x = x_ref[rows, :]
        xb = x.astype(BF16)
        for j0 in range(0, d_ff, FF_CHUNK):
            cs = slice(j0, min(j0 + FF_CHUNK, d_ff))
            g = jnp.dot(xb, wg_ref[:, cs], preferred_element_type=F32)
            u = jnp.dot(xb, wu_ref[:, cs], preferred_element_type=F32)
            act_ref[rows, cs] = (g * _sigmoid(g) * u).astype(BF16)
        f = jnp.dot(act_ref[rows, :], wd_ref[...], preferred_element_type=F32)
        y_ref[rows, :] = _layer_norm(alpha * x + f, lg_ref[...], lb_ref[...])


def _ffn(x, w_gate, w_up, w_down, ln_g, ln_b, layer, alpha, tm):
    t, d = x.shape
    d_ff = w_gate.shape[2]
    rows = pl.BlockSpec((tm, d), lambda i: (i, 0))
    params = (w_gate, w_up, w_down, ln_g, ln_b)
    return pl.pallas_call(
        functools.partial(_ffn_kernel, alpha=alpha),
        grid=(t // tm,),
        in_specs=[rows] + [_layer(p, layer) for p in params],
        out_specs=rows,
        out_shape=jax.ShapeDtypeStruct((t, d), F32),
        scratch_shapes=[pltpu.VMEM((tm, d_ff), BF16)],
        compiler_params=pltpu.CompilerParams(dimension_semantics=("parallel",), vmem_limit_bytes=VMEM_LIMIT),
        name="swiglu_ln",
    )(x, *params)


def _rope_tables(seq):
    pos = np.arange(seq, dtype=np.float32)

    def tables(n_rot, base, width):
        half = n_rot // 2
        inv = np.float32(base) ** (-np.arange(half, dtype=np.float32) * np.float32(2.0) / np.float32(n_rot))
        ang = (pos[:, None] * inv[None, :]).astype(np.float32)
        cos = np.concatenate([np.cos(ang), np.cos(ang), np.ones((seq, width - n_rot), np.float32)], axis=1)
        sin = np.concatenate([-np.sin(ang), np.sin(ang), np.zeros((seq, width - n_rot), np.float32)], axis=1)
        return cos.astype(np.float32), sin.astype(np.float32)

    rc, rs = tables(RET_DK, RET_ROPE_BASE, RET_DK)
    dc, ds = tables(ROPE_DIMS, ROPE_THETA, DIL_HD)
    dil_tabs = []
    for _, dil in DIL_GROUPS:
        for tab in (dc, ds):
            dil_tabs.append(np.ascontiguousarray(tab.reshape(seq // dil, dil, DIL_HD).transpose(1, 0, 2)))
    return (np.tile(rc, (1, LANES // RET_DK)), np.tile(rs, (1, LANES // RET_DK))), dil_tabs


def _pack_w_in(w_in):
    d = w_in.shape[1]
    widths = (256, 256, 512, 512, GLA_RANK, 256, 256, 512, 512, 1536, 1536, 1536, 3 * d)
    ga0 = sum(widths[:4])
    dil0 = sum(widths[:9])
    gate0 = sum(widths[:12])
    w = w_in.astype(BF16)
    packed = jnp.concatenate(
        [w[:, :, :ga0], w[:, :, ga0 + GLA_RANK:dil0], w[:, :, gate0:], w[:, :, dil0:gate0],
         jnp.pad(w[:, :, ga0:ga0 + GLA_RANK], ((0, 0), (0, 0), (0, GA_PAD - GLA_RANK)))], axis=2)
    assert packed.shape[2] == W_GA_COL + GA_PAD
    return packed


def kernel(x, w_in, w_gla_a2, b_gla_a, gla_norm_g, ret_norm_g, w_branch, b_gate, w_out, ln1_g, ln1_b,
           w_ffn_gate, w_ffn_up, w_ffn_down, ln2_g, ln2_b):
    batch, seq, d = x.shape
    depth = w_in.shape[0]
    alpha = (2 * depth) ** 0.25
    ret_tabs, dil_tabs = _rope_tables(seq)
    w_packed = _pack_w_in(w_in)
    w_a2 = jnp.pad(w_gla_a2, ((0, 0), (0, GA_PAD - GLA_RANK), (0, 0))).astype(BF16)
    w_branch, w_out = w_branch.astype(BF16), w_out.astype(BF16)
    w_ffn_gate, w_ffn_up, w_ffn_down = w_ffn_gate.astype(BF16), w_ffn_up.astype(BF16), w_ffn_down.astype(BF16)
    b_gla_a, gla_norm_g, ret_norm_g, b_gate, ln1_g, ln1_b, ln2_g, ln2_b = (
        p[:, None, :] for p in (b_gla_a, gla_norm_g, ret_norm_g, b_gate, ln1_g, ln1_b, ln2_g, ln2_b))
    xf = x.reshape(batch * seq, d)
    for l in range(depth):
        oa, ob, *p_dil = _proj(xf, w_packed, (w_a2, b_gla_a, gla_norm_g, ret_norm_g), l, ret_tabs, dil_tabs,
                               batch, seq)
        o_dil, l_dil = [], []
        for pg, (_, dilation) in zip(p_dil, DIL_GROUPS):
            o, lse = _dil(pg, dilation, batch, seq)
            o_dil.append(o)
            l_dil.append(lse)
        xf = _merge(xf, oa, ob, o_dil, l_dil, w_packed, b_gate, w_branch, w_out, ln1_g, ln1_b, l, alpha,
                    MERGE_ROW_TILE)
        xf = _ffn(xf, w_ffn_gate, w_ffn_up, w_ffn_down, ln2_g, ln2_b, l, alpha, FF_ROW_TILE)
    return xf.reshape(batch, seq, d)
```

```python
import functools
import math

import jax
import jax.numpy as jnp
import numpy as np
from jax import lax
from jax.experimental import pallas as pl
from jax.experimental.pallas import tpu as pltpu

F32 = jnp.float32
BF16 = jnp.bfloat16

LANES = 128
N_HEADS = 4
GLA_DK = 64
GLA_DV = 128
GLA_RANK = 16
GLA_TAU = 16.0
GLA_CHUNK = 64
GLA_GROUP = 256
RET_DK = 64
RET_DV = 128
RET_CHUNK = 128
RET_GROUP = 256
RET_ROPE_BASE = 10000.0
DIL_HD = 128
DIL_GROUPS = ((128, 1), (512, 4), (2048, 16))
DIL_SPAN = 128
DIL_BLOCK = DIL_SPAN * max(d for _, d in DIL_GROUPS)
ROPE_THETA = 500000.0
ROPE_DIMS = 32
BRANCH_W = 512
GA_PAD = 256
GA_COL = 3072
RET_QK_COL = 1536
P1_GLA_COLS = (0, 256, 512, 1024)
P1_RET_COLS = (1536, 1792, 2048, 2560)
WT_MAIN_ROWS = (0, 512, 1024, 1552, 2064, 2576)
WT_GA_ROW = 1536
WT_DIL_ROW = 3088
WT_GATE_ROW = 7696
MASK_VALUE = -1e30

VMEM_LIMIT = 56 * 1024 * 1024
SEQ_BLOCK = 512
FF_CHUNK = 256
FF_ROW_TILE = 1024
FF_ROW_SPLIT = 4
MERGE_ROW_TILE = 1024
MERGE_ROW_SPLIT = 4
DIL_UNROLL = 4


def _sigmoid(x):
    return 0.5 * (jnp.tanh(0.5 * x) + 1.0)


def _nt_dot(a, b):
    return lax.dot_general(a, b, (((1,), (1,)), ((), ())), preferred_element_type=F32)


def _tn_dot(a, b):
    return lax.dot_general(a, b, (((0,), (0,)), ((), ())), preferred_element_type=F32)


def _layer(stacked, layer):
    rest = stacked.shape[1:]
    return pl.BlockSpec((None,) + rest, lambda *_: (layer,) + (0,) * len(rest), pipeline_mode=pl.Buffered(1))


def _rotary(x, cos, sin, head_dim, half):
    width = x.shape[1]
    low = (lax.broadcasted_iota(jnp.int32, x.shape, 1) % head_dim) < half
    partner = jnp.where(low, pltpu.roll(x, width - half, 1), pltpu.roll(x, half, 1))
    return x * cos + partner * sin


def _proj_kernel(*refs):
    n_groups = len(DIL_GROUPS)
    x_ref, wt_ref, rc_ref, rs_ref = refs[:4]
    tabs = refs[4:4 + 2 * n_groups]
    wa2_ref, ba_ref, gg_ref, rg_ref = refs[4 + 2 * n_groups:8 + 2 * n_groups]
    ret_consts = refs[8 + 2 * n_groups:12 + 2 * n_groups]
    oa_ref, ob_ref = refs[12 + 2 * n_groups:14 + 2 * n_groups]
    g_refs = refs[14 + 2 * n_groups:14 + 3 * n_groups]
    (slab_ref, slab4_ref, xp4_ref, xp16_ref, p1_ref,
     gst_ref, gqs_ref, gkd_ref, gacc_ref, rst_ref, rqd_ref, rkd_ref, racc_ref) = refs[14 + 3 * n_groups:]
    tm, k = x_ref.shape

    @pl.when(pl.program_id(1) == 0)
    def _():
        gst_ref[...] = jnp.zeros_like(gst_ref)
        rst_ref[...] = jnp.zeros_like(rst_ref)

    n_slabs = k // LANES
    tn = BRANCH_W
    x = x_ref[...].astype(BF16)
    lane = lax.broadcasted_iota(jnp.int32, (1, tn), 1)
    for j0, r0 in zip(range(0, GA_COL, tn), WT_MAIN_ROWS):
        res = _nt_dot(x, wt_ref[r0:r0 + tn, :])
        if j0 == 0:
            res = res * jnp.where(lane < N_HEADS * GLA_DK, GLA_DK ** -0.5, 1.0)
        if j0 == RET_QK_COL:
            cos = jnp.concatenate([rc_ref[...]] * (tn // LANES), axis=1)
            sin = jnp.concatenate([rs_ref[...]] * (tn // LANES), axis=1)
            res = _rotary(res, cos, sin, RET_DK, RET_DK // 2)
            res = res * jnp.where(lane < N_HEADS * RET_DK, 1.0, RET_DK ** -0.5)
        p1_ref[:, j0:j0 + tn] = res.astype(BF16)
    p1_ref[:, GA_COL:] = _nt_dot(x, wt_ref[WT_GA_ROW:WT_GA_ROW + GA_PAD, :]).astype(BF16)
    _gla_body(p1_ref, wa2_ref, ba_ref, gg_ref, oa_ref, gst_ref, gqs_ref, gkd_ref, gacc_ref)
    _ret_body(p1_ref, rg_ref, *ret_consts, ob_ref, rst_ref, rqd_ref, rkd_ref, racc_ref)
    q4 = tm // 4
    q16 = tm // 16
    for j in range(n_slabs):
        slab_ref[j] = x_ref[:, j * LANES:(j + 1) * LANES]
    for c in range(4):
        for j in range(n_slabs):
            part = slab_ref[j, pl.ds(c, q4, stride=4), :]
            slab4_ref[j, c * q4:(c + 1) * q4, :] = part
            xp4_ref[c * q4:(c + 1) * q4, j * LANES:(j + 1) * LANES] = part.astype(BF16)
    for c0 in range(4):
        for c1 in range(4):
            c = 4 * c1 + c0
            for j in range(n_slabs):
                xp16_ref[c * q16:(c + 1) * q16, j * LANES:(j + 1) * LANES] = (
                    slab4_ref[j, pl.ds(c0 * q4 + c1, q16, stride=4), :].astype(BF16))
    lhs_of = {1: lambda: x, 4: lambda: xp4_ref[...], 16: lambda: xp16_ref[...]}
    for g, (_, dil) in enumerate(DIL_GROUPS):
        rows = tm // dil
        lhs = lhs_of[dil]()
        cos = jnp.concatenate([tabs[2 * g][...].reshape(tm, LANES)] * N_HEADS, axis=1)
        sin = jnp.concatenate([tabs[2 * g + 1][...].reshape(tm, LANES)] * N_HEADS, axis=1)
        for part in range(3):
            r0 = WT_DIL_ROW + part * len(DIL_GROUPS) * tn + g * tn
            res = _nt_dot(lhs, wt_ref[r0:r0 + tn, :])
            if part < 2:
                res = _rotary(res, cos, sin, DIL_HD, ROPE_DIMS // 2)
            if part == 0:
                res = res * (DIL_HD ** -0.5 * math.log2(math.e))
            g_refs[g][:, :, part * tn:(part + 1) * tn] = res.astype(BF16).reshape(dil, rows, tn)


def _proj(x, w, mixer_params, layer, ret_tabs, dil_tabs, batch, seq):
    assert tuple(d for _, d in DIL_GROUPS) == (1, 4, 16)
    t, k = x.shape
    tm = SEQ_BLOCK
    nblk = seq // tm
    w_specs = [pl.BlockSpec((None, WT_GATE_ROW, k), lambda *_: (layer, 0, 0), pipeline_mode=pl.Buffered(1))]

    def row(b, i):
        return (b * nblk + i, 0)

    tab_specs, out_specs, out_shapes = [], [], []
    for _, dil in DIL_GROUPS:
        tab_specs += [pl.BlockSpec((dil, tm // dil, LANES), lambda b, i: (0, i, 0))] * 2
        out_specs.append(pl.BlockSpec((None, dil, tm // dil, 3 * BRANCH_W), lambda b, i: (b, 0, i, 0)))
        out_shapes.append(jax.ShapeDtypeStruct((batch, dil, seq // dil, 3 * BRANCH_W), BF16))
    ret_spec = pl.BlockSpec((tm, LANES), lambda b, i: (i, 0))
    branch = pl.BlockSpec((tm, BRANCH_W), row)
    state = pltpu.VMEM((N_HEADS * GLA_DV, N_HEADS * GLA_DK), F32)
    decayed = pltpu.VMEM((tm, N_HEADS * GLA_DK), BF16)
    acc = pltpu.VMEM((tm, BRANCH_W), F32)
    ret_consts = _ret_constants()
    const_specs = [pl.BlockSpec(a.shape, functools.partial(lambda *_, nd: (0,) * nd, nd=a.ndim),
                                pipeline_mode=pl.Buffered(1)) for a in ret_consts]
    return pl.pallas_call(
        _proj_kernel,
        grid=(batch, nblk),
        in_specs=[pl.BlockSpec((tm, k), row)] + w_specs + [ret_spec, ret_spec] + tab_specs
                 + [_layer(p, layer) for p in mixer_params] + const_specs,
        out_specs=[branch, branch] + out_specs,
        out_shape=[jax.ShapeDtypeStruct((t, BRANCH_W), BF16)] * 2 + out_shapes,
        scratch_shapes=[pltpu.VMEM((k // LANES, tm, LANES), F32), pltpu.VMEM((k // LANES, tm, LANES), F32),
                        pltpu.VMEM((tm, k), BF16), pltpu.VMEM((tm, k), BF16),
                        pltpu.VMEM((tm, GA_COL + GA_PAD), BF16),
                        state, decayed, decayed, acc, state, decayed, decayed, acc],
        compiler_params=pltpu.CompilerParams(dimension_semantics=("parallel", "arbitrary"),
                                             vmem_limit_bytes=VMEM_LIMIT),
        name="in_proj_mixers",
    )(x, w, *ret_tabs, *dil_tabs, *mixer_params, *ret_consts)


def _state_step(p1_ref, rows, v_col, st_ref, q_ref, kd_ref, acc_ref, decay, st_diag):
    st = st_ref[...]
    acc_ref[rows, :] += _nt_dot(q_ref[rows, :], st.astype(BF16))
    kv_t = _tn_dot(p1_ref[rows, v_col:v_col + BRANCH_W], kd_ref[rows, :])
    st_ref[...] = st * decay + jnp.where(st_diag, kv_t, 0.0)


def _gla_body(p1_ref, wa_ref, ba_ref, g_ref, o_ref, st_ref, qs_ref, kd_ref, acc_ref):
    c = GLA_CHUNK
    grp = GLA_GROUP
    tb = p1_ref.shape[0]
    width = N_HEADS * GLA_DK
    q_col, k_col, v_col, r_col = P1_GLA_COLS

    z = jnp.dot(p1_ref[:, GA_COL:GA_COL + GA_PAD], wa_ref[...], preferred_element_type=F32) + ba_ref[...]
    log_a = (jnp.minimum(z, 0.0) - jnp.log1p(jnp.exp(-jnp.abs(z)))) * (1.0 / GLA_TAU)
    hi = log_a.astype(BF16)
    lo = (log_a - hi.astype(F32)).astype(BF16)
    hl = jnp.concatenate([hi, lo], axis=1)

    row = lax.broadcasted_iota(jnp.int32, (grp, grp), 0)
    col = lax.broadcasted_iota(jnp.int32, (grp, grp), 1)
    causal = ((row // c) == (col // c)) & (row >= col)
    cum = jnp.where(causal, 1.0, 0.0).astype(BF16)
    lane_head = lax.broadcasted_iota(jnp.int32, (grp, width), 1) // GLA_DK
    st_diag = (lax.broadcasted_iota(jnp.int32, st_ref.shape, 0) // GLA_DV
               == lax.broadcasted_iota(jnp.int32, st_ref.shape, 1) // GLA_DK)

    decays = []
    for g0 in range(0, tb, grp):
        rows = slice(g0, g0 + grp)
        r2 = jnp.dot(cum, hl[rows], preferred_element_type=F32)
        b = r2[:, :width] + r2[:, width:]
        b_tot = jnp.concatenate([jnp.broadcast_to(b[n0 + c - 1:n0 + c, :], (c, width)) for n0 in range(0, grp, c)],
                                axis=0)
        q = p1_ref[rows, q_col:q_col + width].astype(F32)
        k = p1_ref[rows, k_col:k_col + width].astype(F32)
        q_s = (q * jnp.exp(b)).astype(BF16)
        k_s = k * jnp.exp(-b)
        qs_ref[rows, :] = q_s
        kd_ref[rows, :] = (k * jnp.exp(b_tot - b)).astype(BF16)
        for n0 in range(0, grp, c):
            decays.append(jnp.exp(b_tot[n0:n0 + 1, :]))
        for h in range(N_HEADS):
            hs = slice(h * GLA_DV, (h + 1) * GLA_DV)
            k_h = jnp.where(lane_head == h, k_s, 0.0).astype(BF16)
            att = jnp.where(causal, _nt_dot(q_s, k_h), 0.0).astype(BF16)
            v_h = p1_ref[rows, v_col + h * GLA_DV:v_col + (h + 1) * GLA_DV]
            acc_ref[rows, hs] = jnp.dot(att, v_h, preferred_element_type=F32)

    for n, decay in enumerate(decays):
        rows = slice(n * c, (n + 1) * c)
        _state_step(p1_ref, rows, v_col, st_ref, qs_ref, kd_ref, acc_ref, decay, st_diag)

    outs = []
    for h in range(N_HEADS):
        o = acc_ref[:, h * GLA_DV:(h + 1) * GLA_DV]
        outs.append(o * lax.rsqrt(jnp.mean(o * o, axis=-1, keepdims=True) + 1e-6))
    y = jnp.concatenate(outs, axis=-1) * g_ref[...]
    r = p1_ref[:, r_col:r_col + BRANCH_W].astype(F32)
    o_ref[...] = (y * (r * _sigmoid(r))).astype(o_ref.dtype)


def _ret_constants():
    c, grp, width = RET_CHUNK, RET_GROUP, N_HEADS * RET_DK
    log_g = np.array([math.log1p(-(2.0 ** (-5.0 - h))) for h in range(N_HEADS)], np.float32)
    row = np.arange(grp)[:, None]
    col = np.arange(grp)[None, :]
    causal = ((row // c) == (col // c)) & (row >= col)
    diff = np.maximum(row - col, 0).astype(np.float32)
    dmats = np.stack([np.where(causal, np.exp(diff * log_g[h]), 0.0) for h in range(N_HEADS)]).astype(np.float32)
    lg_lane = np.repeat(log_g, RET_DK)[None, :]
    ridx = (np.arange(grp) % c).astype(np.float32)[:, None]
    q_scale = np.exp((ridx + np.float32(1.0)) * lg_lane).astype(np.float32)
    k_scale = np.exp((np.float32(c - 1.0) - ridx) * lg_lane).astype(np.float32)
    chunk_decay = np.broadcast_to(np.exp(np.float32(c) * lg_lane), (8, width)).astype(np.float32)
    return dmats, q_scale, k_scale, chunk_decay


def _ret_body(p1_ref, g_ref, dm_ref, qsc_ref, ksc_ref, dec_ref, o_ref, st_ref, qd_ref, kd_ref, acc_ref):
    c = RET_CHUNK
    grp = RET_GROUP
    tb = p1_ref.shape[0]
    width = N_HEADS * RET_DK
    q_col, k_col, v_col, r_col = P1_RET_COLS

    lane_head = lax.broadcasted_iota(jnp.int32, (grp, width), 1) // RET_DK
    q_scale = qsc_ref[...]
    k_scale = ksc_ref[...]
    chunk_decay = dec_ref[0:1, :]
    st_diag = (lax.broadcasted_iota(jnp.int32, st_ref.shape, 0) // RET_DV
               == lax.broadcasted_iota(jnp.int32, st_ref.shape, 1) // RET_DK)

    for g0 in range(0, tb, grp):
        rows = slice(g0, g0 + grp)
        qb = p1_ref[rows, q_col:q_col + width]
        q = qb.astype(F32)
        k = p1_ref[rows, k_col:k_col + width].astype(F32)
        qd_ref[rows, :] = (q * q_scale).astype(BF16)
        kd_ref[rows, :] = (k * k_scale).astype(BF16)
        for h in range(N_HEADS):
            hs = slice(h * RET_DV, (h + 1) * RET_DV)
            k_h = jnp.where(lane_head == h, k, 0.0).astype(BF16)
            att = (_nt_dot(qb, k_h) * dm_ref[h]).astype(BF16)
            v_h = p1_ref[rows, v_col + h * RET_DV:v_col + (h + 1) * RET_DV]
            acc_ref[rows, hs] = jnp.dot(att, v_h, preferred_element_type=F32)

    for n in range(tb // c):
        rows = slice(n * c, (n + 1) * c)
        _state_step(p1_ref, rows, v_col, st_ref, qd_ref, kd_ref, acc_ref, chunk_decay, st_diag)

    outs = []
    for h in range(N_HEADS):
        o = acc_ref[:, h * RET_DV:(h + 1) * RET_DV]
        mu = jnp.mean(o, axis=-1, keepdims=True)
        d = o - mu
        outs.append(d * lax.rsqrt(jnp.mean(d * d, axis=-1, keepdims=True) + 1e-5))
    y = jnp.concatenate(outs, axis=-1) * g_ref[...]
    r = p1_ref[:, r_col:r_col + BRANCH_W].astype(F32)
    o_ref[...] = (y * (r * _sigmoid(r))).astype(o_ref.dtype)


def _dil_kernel(q_ref, k_ref, v_ref, kp_ref, vp_ref, o_ref, l_ref, o_scr, l_scr, *, dilation):
    n = DIL_SPAN
    nj = q_ref.shape[1] // n
    qry = lax.broadcasted_iota(jnp.int32, (n, 2 * n), 0)
    key = lax.broadcasted_iota(jnp.int32, (n, 2 * n), 1)
    dist = qry + n - key
    bias_t = (dist >= 0) & (dist <= n)
    bias_t_first = bias_t & ((key >= n) | (pl.program_id(1) > 0))
    ones = jnp.ones((2 * n, n), BF16)
    lane_head = lax.broadcasted_iota(jnp.int32, (n, LANES), 1) // (LANES // N_HEADS)

    def unit(c, j, bias, keys, values):
        start = j * (n * dilation) + c
        dst = pl.ds(start, n, stride=dilation) if dilation > 1 else pl.ds(start, n)
        q = q_ref[c, pl.ds(j * n, n), :]
        lse_tile = jnp.zeros((n, LANES), F32)
        for h in range(N_HEADS):
            hs = slice(h * DIL_HD, (h + 1) * DIL_HD)
            s = jnp.where(bias, _nt_dot(q[:, hs], keys(hs)), MASK_VALUE)
            m = jnp.max(s, axis=-1, keepdims=True)
            p = jnp.exp2(s - m)
            o_den = jnp.dot(p.astype(BF16), jnp.concatenate([values(hs), ones], axis=1),
                            preferred_element_type=F32)
            den = o_den[:, DIL_HD:]
            o_scr[h, dst, :] = o_den[:, :DIL_HD] / den
            lse_tile = jnp.where(lane_head == h, m * math.log(2.0) + jnp.log(den), lse_tile)
        l_scr[dst, :] = lse_tile

    def first_unit(c, carry):
        unit(c, 0, bias_t_first,
             lambda hs: jnp.concatenate([kp_ref[c, :, hs], k_ref[c, pl.ds(0, n), hs]], axis=0),
             lambda hs: jnp.concatenate([vp_ref[c, :, hs], v_ref[c, pl.ds(0, n), hs]], axis=0))
        return carry

    def later_unit(u, carry):
        c = u // (nj - 1)
        j = 1 + u % (nj - 1)
        r0 = pl.multiple_of((j - 1) * n, n)
        unit(c, j, bias_t,
             lambda hs: k_ref[c, pl.ds(r0, 2 * n), hs],
             lambda hs: v_ref[c, pl.ds(r0, 2 * n), hs])
        return carry

    lax.fori_loop(0, dilation, first_unit, 0, unroll=min(dilation, DIL_UNROLL))
    if nj > 1:
        lax.fori_loop(0, dilation * (nj - 1), later_unit, 0, unroll=DIL_UNROLL)
    for h in range(N_HEADS):
        o_ref[:, h * DIL_HD:(h + 1) * DIL_HD] = o_scr[h].astype(o_ref.dtype)
    l_ref[...] = l_scr[...]


def _dil(pg, dilation, batch, seq):
    t = batch * seq
    blk = DIL_BLOCK
    nblk = seq // blk
    rows = blk // dilation
    prev_per_blk = rows // DIL_SPAN

    def cur(part):
        return pl.BlockSpec((None, dilation, rows, BRANCH_W), lambda b, i: (b, 0, i, part))

    def prev(part):
        return pl.BlockSpec((None, dilation, DIL_SPAN, BRANCH_W),
                            lambda b, i: (b, 0, jnp.maximum(i * prev_per_blk - 1, 0), part))

    return pl.pallas_call(
        functools.partial(_dil_kernel, dilation=dilation),
        grid=(batch, nblk),
        in_specs=[cur(0), cur(1), cur(2), prev(1), prev(2)],
        out_specs=[pl.BlockSpec((blk, BRANCH_W), lambda b, i: (b * nblk + i, 0)),
                   pl.BlockSpec((blk, LANES), lambda b, i: (b * nblk + i, 0))],
        out_shape=[jax.ShapeDtypeStruct((t, BRANCH_W), BF16), jax.ShapeDtypeStruct((t, LANES), F32)],
        scratch_shapes=[pltpu.VMEM((N_HEADS, blk, DIL_HD), F32), pltpu.VMEM((blk, LANES), F32)],
        compiler_params=pltpu.CompilerParams(dimension_semantics=("parallel", "arbitrary"),
                                             vmem_limit_bytes=VMEM_LIMIT),
        name=f"dilated_attn_{dilation}",
    )(pg, pg, pg, pg, pg)


def _layer_norm(v, g, b):
    mu = jnp.mean(v, axis=-1, keepdims=True)
    d = v - mu
    return d * lax.rsqrt(jnp.mean(d * d, axis=-1, keepdims=True) + 1e-5) * g + b


def _merge_kernel(x_ref, oa_ref, ob_ref, o1_ref, o2_ref, o3_ref, l1_ref, l2_ref, l3_ref,
                  wg_ref, bg_ref, wb_ref, wo_ref, lg_ref, lb_ref, y_ref, *, alpha):
    tm, d = x_ref.shape
    lanes_per_head = LANES // N_HEADS
    for r0 in range(0, tm, tm // MERGE_ROW_SPLIT):
        rows = slice(r0, r0 + tm // MERGE_ROW_SPLIT)
        l1, l2, l3 = l1_ref[rows, :], l2_ref[rows, :], l3_ref[rows, :]
        m = jnp.maximum(jnp.maximum(l1, l2), l3)
        e1, e2, e3 = jnp.exp(l1 - m), jnp.exp(l2 - m), jnp.exp(l3 - m)
        tot = e1 + e2 + e3
        w1, w2, w3 = e1 / tot, e2 / tot, e3 / tot
        parts = []
        for h in range(N_HEADS):
            hs = slice(h * DIL_HD, (h + 1) * DIL_HD)
            ls = slice(h * lanes_per_head, h * lanes_per_head + 1)
            parts.append(w1[:, ls] * o1_ref[rows, hs].astype(F32) + w2[:, ls] * o2_ref[rows, hs].astype(F32)
                         + w3[:, ls] * o3_ref[rows, hs].astype(F32))
        oc = jnp.concatenate(parts, axis=-1).astype(BF16)
        x = x_ref[rows, :]
        xb = x.astype(BF16)
        merged = None
        for j, o in enumerate((oa_ref[rows, :], ob_ref[rows, :], oc)):
            gate = _sigmoid(_nt_dot(xb, wg_ref[0, j * d:(j + 1) * d, :]) + bg_ref[:, j * d:(j + 1) * d])
            term = gate * jnp.dot(o, wb_ref[j], preferred_element_type=F32)
            merged = term if merged is None else merged + term
        hmix = jnp.dot(merged.astype(BF16), wo_ref[...], preferred_element_type=F32)
        y_ref[rows, :] = _layer_norm(alpha * x + hmix, lg_ref[...], lb_ref[...])


def _merge(x, oa, ob, o_dil, l_dil, w_gate, b_gate, w_branch, w_out, ln_g, ln_b, layer, alpha, tm):
    t, d = x.shape

    def rows(width):
        return pl.BlockSpec((tm, width), lambda i: (i, 0))

    params = (b_gate, w_branch, w_out, ln_g, ln_b)
    gate_spec = pl.BlockSpec((pl.Element(1), pl.Element(3 * d), pl.Element(d)), lambda *_: (layer, WT_GATE_ROW, 0),
                             pipeline_mode=pl.Buffered(1))
    return pl.pallas_call(
        functools.partial(_merge_kernel, alpha=alpha),
        grid=(t // tm,),
        in_specs=[rows(d)] + [rows(BRANCH_W)] * 5 + [rows(LANES)] * 3 + [gate_spec]
                 + [_layer(p, layer) for p in params],
        out_specs=rows(d),
        out_shape=jax.ShapeDtypeStruct((t, d), F32),
        compiler_params=pltpu.CompilerParams(dimension_semantics=("parallel",), vmem_limit_bytes=VMEM_LIMIT),
        name="merge_out_ln",
    )(x, oa, ob, *o_dil, *l_dil, w_gate, *params)


def _ffn_kernel(x_ref, wg_ref, wu_ref, wd_ref, lg_ref, lb_ref, y_ref, act_ref, *, alpha):
    d_ff = wg_ref.shape[1]
    tm = x_ref.shape[0]
    for r0 in range(0, tm, tm // FF_ROW_SPLIT):
        rows = slice(r0, r0 + tm // FF_ROW_SPLIT)
        x = x_ref[rows, :]
        xb = x.astype(BF16)
        for j0 in range(0, d_ff, FF_CHUNK):
            cs = slice(j0, min(j0 + FF_CHUNK, d_ff))
            g = jnp.dot(xb, wg_ref[:, cs], preferred_element_type=F32)
            u = jnp.dot(xb, wu_ref[:, cs], preferred_element_type=F32)
            act_ref[rows, cs] = (g * _sigmoid(g) * u).astype(BF16)
        f = jnp.dot(act_ref[rows, :], wd_ref[...], preferred_element_type=F32)
        y_ref[rows, :] = _layer_norm(alpha * x + f, lg_ref[...], lb_ref[...])


def _ffn(x, w_gate, w_up, w_down, ln_g, ln_b, layer, alpha, tm):
    t, d = x.shape
    d_ff = w_gate.shape[2]
    rows = pl.BlockSpec((tm, d), lambda i: (i, 0))
    params = (w_gate, w_up, w_down, ln_g, ln_b)
    return pl.pallas_call(
        functools.partial(_ffn_kernel, alpha=alpha),
        grid=(t // tm,),
        in_specs=[rows] + [_layer(p, layer) for p in params],
        out_specs=rows,
        out_shape=jax.ShapeDtypeStruct((t, d), F32),
        scratch_shapes=[pltpu.VMEM((tm, d_ff), BF16)],
        compiler_params=pltpu.CompilerParams(dimension_semantics=("parallel",), vmem_limit_bytes=VMEM_LIMIT),
        name="swiglu_ln",
    )(x, *params)


def _rope_tables(seq):
    pos = np.arange(seq, dtype=np.float32)

    def tables(n_rot, base, width):
        half = n_rot // 2
        inv = np.float32(base) ** (-np.arange(half, dtype=np.float32) * np.float32(2.0) / np.float32(n_rot))
        ang = (pos[:, None] * inv[None, :]).astype(np.float32)
        cos = np.concatenate([np.cos(ang), np.cos(ang), np.ones((seq, width - n_rot), np.float32)], axis=1)
        sin = np.concatenate([-np.sin(ang), np.sin(ang), np.zeros((seq, width - n_rot), np.float32)], axis=1)
        return cos.astype(np.float32), sin.astype(np.float32)

    rc, rs = tables(RET_DK, RET_ROPE_BASE, RET_DK)
    dc, ds = tables(ROPE_DIMS, ROPE_THETA, DIL_HD)
    dil_tabs = []
    for _, dil in DIL_GROUPS:
        for tab in (dc, ds):
            dil_tabs.append(np.ascontiguousarray(tab.reshape(seq // dil, dil, DIL_HD).transpose(1, 0, 2)))
    return (np.tile(rc, (1, LANES // RET_DK)), np.tile(rs, (1, LANES // RET_DK))), dil_tabs


def kernel(x, w_in, w_gla_a2, b_gla_a, gla_norm_g, ret_norm_g, w_branch, b_gate, w_out, ln1_g, ln1_b,
           w_ffn_gate, w_ffn_up, w_ffn_down, ln2_g, ln2_b):
    batch, seq, d = x.shape
    depth = w_in.shape[0]
    alpha = (2 * depth) ** 0.25
    ret_tabs, dil_tabs = _rope_tables(seq)
    assert w_in.shape[2] == WT_GATE_ROW + 3 * d
    w_in_t = jnp.swapaxes(w_in, 1, 2).astype(BF16)
    w_gate = w_in_t
    w_a2 = jnp.pad(w_gla_a2, ((0, 0), (0, GA_PAD - GLA_RANK), (0, 0))).astype(BF16)
    w_branch, w_out = w_branch.astype(BF16), w_out.astype(BF16)
    w_ffn_gate, w_ffn_up, w_ffn_down = w_ffn_gate.astype(BF16), w_ffn_up.astype(BF16), w_ffn_down.astype(BF16)
    b_gla_a, gla_norm_g, ret_norm_g, b_gate, ln1_g, ln1_b, ln2_g, ln2_b = (
        p[:, None, :] for p in (b_gla_a, gla_norm_g, ret_norm_g, b_gate, ln1_g, ln1_b, ln2_g, ln2_b))
    xf = x.reshape(batch * seq, d)
    for l in range(depth):
        oa, ob, *p_dil = _proj(xf, w_in_t, (w_a2, b_gla_a, gla_norm_g, ret_norm_g), l, ret_tabs, dil_tabs,
                               batch, seq)
        o_dil, l_dil = [], []
        for pg, (_, dilation) in zip(p_dil, DIL_GROUPS):
            o, lse = _dil(pg, dilation, batch, seq)
            o_dil.append(o)
            l_dil.append(lse)
        xf = _merge(xf, oa, ob, o_dil, l_dil, w_gate, b_gate, w_branch, w_out, ln1_g, ln1_b, l, alpha,
                    MERGE_ROW_TILE)
        xf = _ffn(xf, w_ffn_gate, w_ffn_up, w_ffn_down, ln2_g, ln2_b, l, alpha, FF_ROW_TILE)
    return xf.reshape(batch, seq, d)
```
